```python
import jax, jax.numpy as jnp
from jax import lax
import numpy as np

D_MODEL = 1024
BATCH = 8
SEQ = 2048
DEPTH = 1
DEC_BATCH = 128
DEC_SEQ = 1
PAST_LEN = 16384
PAGE_SIZE = 128

N_RET_HEADS = 4
RET_DK = 256
RET_DV = 512
RET_QK = N_RET_HEADS * RET_DK
RET_V = N_RET_HEADS * RET_DV
CHUNK = 128
ROPE_BASE = 10000.0
CONV_CH = D_MODEL
CONV_WIDTH = 31
CONV_STATE = CONV_WIDTH - 1
D_FF = -(-(8 * D_MODEL) // (3 * 256)) * 256
EPS = 1e-6
SPLITS = [RET_QK, RET_QK, RET_V, RET_V, CONV_CH, CONV_CH, D_MODEL, D_MODEL]
D_IN = sum(SPLITS)
SPLIT_IDX = [int(v) for v in np.cumsum(SPLITS)[:-1]]

kernel_name = "retention_conformer_parallel_decoder_step"


def rms_norm(x, g):
    xf = x.astype(jnp.float32)
    out = xf * lax.rsqrt(jnp.mean(xf * xf, axis=-1, keepdims=True) + EPS)
    return (out * g.astype(jnp.float32)).astype(x.dtype)


def layer_norm(x, g, b):
    xf = x.astype(jnp.float32)
    mu = jnp.mean(xf, axis=-1, keepdims=True)
    var = jnp.mean(jnp.square(xf - mu), axis=-1, keepdims=True)
    out = (xf - mu) * lax.rsqrt(var + EPS)
    return (out * g.astype(jnp.float32) + b.astype(jnp.float32)).astype(x.dtype)


def rotary(t, pos):
    inv_freq = ROPE_BASE ** (-jnp.arange(0, RET_DK, 2, dtype=jnp.float32) / RET_DK)
    ang = pos[:, None] * inv_freq[None, :]
    cos = jnp.cos(ang)[None, :, None, :]
    sin = jnp.sin(ang)[None, :, None, :]
    t1, t2 = jnp.split(t, 2, axis=-1)
    return jnp.concatenate([t1 * cos - t2 * sin, t2 * cos + t1 * sin], axis=-1)


def retention_log_gamma():
    return jnp.log(1.0 - jnp.exp(jnp.linspace(jnp.log(1.0 / 32.0), jnp.log(1.0 / 512.0), N_RET_HEADS)))


def retention_chunk(q, k, v, s0, log_gamma):
    L = q.shape[2]
    idx = jnp.arange(L, dtype=jnp.float32)
    diff = idx[:, None] - idx[None, :]
    decay = jnp.where(diff >= 0, jnp.exp(log_gamma[:, None, None] * jnp.maximum(diff, 0.0)), 0.0)
    scores = jnp.einsum('bhid,bhjd->bhij', q, k) * decay[None]
    y = jnp.einsum('bhij,bhjv->bhiv', scores, v)
    cross_decay = jnp.exp(log_gamma[:, None] * (idx + 1.0))[None, :, :, None]
    y = y + jnp.einsum('bhid,bhdv->bhiv', q, s0) * cross_decay
    k_dec = k * jnp.exp(log_gamma[:, None] * (L - 1.0 - idx))[None, :, :, None]
    s_new = jnp.exp(log_gamma * L)[None, :, None, None] * s0 + jnp.einsum('bhjd,bhjv->bhdv', k_dec, v)
    return y, s_new


def retention_seq(q, k, v, s0, log_gamma):
    B, H, L, _ = q.shape
    if L <= CHUNK:
        return retention_chunk(q, k, v, s0, log_gamma)
    nc = L // CHUNK

    def split(t):
        return t.reshape(B, H, nc, CHUNK, t.shape[-1]).transpose(2, 0, 1, 3, 4)

    def step(s, xs):
        qc, kc, vc = xs
        y, s = retention_chunk(qc, kc, vc, s, log_gamma)
        return s, y

    s_fin, ys = lax.scan(step, s0, (split(q), split(k), split(v)))
    y = ys.transpose(1, 2, 0, 3, 4).reshape(B, H, L, ys.shape[-1])
    return y, s_fin


def mixer_sublayer(h, pos, conv_prev, ret_prev, w_in, conv_w, conv_b, conv_ln_g, conv_ln_b,
                   w_conv_out, ret_gn_g, w_ret_out, w_out):
    B, L, _ = h.shape
    z = h @ w_in
    q, k, v, g, u, a, gr, gc = jnp.split(z, SPLIT_IDX, axis=-1)
    qf = rotary(q.astype(jnp.float32).reshape(B, L, N_RET_HEADS, RET_DK), pos) * (RET_DK ** -0.5)
    kf = rotary(k.astype(jnp.float32).reshape(B, L, N_RET_HEADS, RET_DK), pos)
    vf = v.astype(jnp.float32).reshape(B, L, N_RET_HEADS, RET_DV)
    y, ret_new = retention_seq(qf.transpose(0, 2, 1, 3), kf.transpose(0, 2, 1, 3),
                               vf.transpose(0, 2, 1, 3), ret_prev.astype(jnp.float32),
                               retention_log_gamma())
    mu = jnp.mean(y, axis=-1, keepdims=True)
    var = jnp.mean(jnp.square(y - mu), axis=-1, keepdims=True)
    y = ((y - mu) * lax.rsqrt(var + EPS)).transpose(0, 2, 1, 3).reshape(B, L, RET_V)
    y = y * ret_gn_g.astype(jnp.float32)
    ret_out = (jax.nn.silu(g.astype(jnp.float32)) * y).astype(h.dtype) @ w_ret_out
    glu = u * jax.nn.sigmoid(a)
    full = jnp.concatenate([conv_prev.astype(glu.dtype), glu], axis=1)
    conv = lax.conv_general_dilated(full, conv_w[:, None, :].astype(glu.dtype), window_strides=(1,),
                                    padding='VALID', dimension_numbers=('NWC', 'WIO', 'NWC'),
                                    feature_group_count=CONV_CH) + conv_b
    conv_out = jax.nn.silu(layer_norm(conv, conv_ln_g, conv_ln_b)) @ w_conv_out
    conv_new = full[:, -CONV_STATE:, :]
    merged = jax.nn.sigmoid(gr) * ret_out + jax.nn.sigmoid(gc) * conv_out
    return merged @ w_out, conv_new, ret_new


def swiglu(h, w_gate, w_up, w_down):
    return (jax.nn.silu(h @ w_gate) * (h @ w_up)) @ w_down


def decoder_layer(x, c, pos, conv_prev, ret_prev, w_in, w_ada, b_ada, g_pre1, g_post1, g_pre2, g_post2,
                  conv_w, conv_b, conv_ln_g, conv_ln_b, w_conv_out, ret_gn_g, w_ret_out, w_out,
                  w_ffn_gate, w_ffn_up, w_ffn_down):
    mod = jax.nn.silu(c) @ w_ada + b_ada
    sh1, sc1, gt1, sh2, sc2, gt2 = [m[:, None, :] for m in jnp.split(mod, 6, axis=-1)]
    h = rms_norm(x, g_pre1) * (1.0 + sc1) + sh1
    mix, conv_new, ret_new = mixer_sublayer(h, pos, conv_prev, ret_prev, w_in, conv_w, conv_b,
                                            conv_ln_g, conv_ln_b, w_conv_out, ret_gn_g, w_ret_out, w_out)
    x = x + gt1 * rms_norm(mix, g_post1)
    h = rms_norm(x, g_pre2) * (1.0 + sc2) + sh2
    x = x + gt2 * rms_norm(swiglu(h, w_ffn_gate, w_ffn_up, w_ffn_down), g_post2)
    return x, conv_new, ret_new


def setup_inputs(seed: int = 0) -> dict:
    key = jax.random.key(seed)
    ks = jax.random.split(key, 32)
    f32 = jnp.float32

    def nrm(k, shape, scale):
        return jax.random.normal(k, shape, f32) * scale

    def gain(k, shape):
        return 1.0 + 0.01 * jax.random.normal(k, shape, f32)

    return {
        "x_prompt": nrm(ks[0], (BATCH, SEQ, D_MODEL), 1.0),
        "x_sample": nrm(ks[1], (DEC_BATCH, DEC_SEQ, D_MODEL), 1.0),
        "c_prompt": nrm(ks[2], (BATCH, D_MODEL), 1.0),
        "c_sample": nrm(ks[3], (DEC_BATCH, D_MODEL), 1.0),
        "state_ret": nrm(ks[4], (DEPTH, DEC_BATCH, N_RET_HEADS, RET_DK, RET_DV), 4.0),
        "state_conv": nrm(ks[5], (DEPTH, DEC_BATCH, CONV_STATE, CONV_CH), 0.5),
        "w_in": nrm(ks[6], (DEPTH, D_MODEL, D_IN), D_MODEL ** -0.5),
        "w_ada": nrm(ks[7], (DEPTH, D_MODEL, 6 * D_MODEL), 0.5 * D_MODEL ** -0.5),
        "b_ada": nrm(ks[8], (DEPTH, 6 * D_MODEL), 0.02),
        "g_pre1": gain(ks[9], (DEPTH, D_MODEL)),
        "g_post1": gain(ks[10], (DEPTH, D_MODEL)),
        "g_pre2": gain(ks[11], (DEPTH, D_MODEL)),
        "g_post2": gain(ks[12], (DEPTH, D_MODEL)),
        "conv_w": nrm(ks[13], (DEPTH, CONV_WIDTH, CONV_CH), CONV_WIDTH ** -0.5),
        "conv_b": nrm(ks[14], (DEPTH, CONV_CH), 0.02),
        "conv_ln_g": gain(ks[15], (DEPTH, CONV_CH)),
        "conv_ln_b": nrm(ks[16], (DEPTH, CONV_CH), 0.02),
        "w_conv_out": nrm(ks[17], (DEPTH, CONV_CH, D_MODEL), CONV_CH ** -0.5),
        "ret_gn_g": gain(ks[18], (DEPTH, RET_V)),
        "w_ret_out": nrm(ks[19], (DEPTH, RET_V, D_MODEL), RET_V ** -0.5),
        "w_out": nrm(ks[20], (DEPTH, D_MODEL, D_MODEL), D_MODEL ** -0.5),
        "w_ffn_gate": nrm(ks[21], (DEPTH, D_MODEL, D_FF), D_MODEL ** -0.5),
        "w_ffn_up": nrm(ks[22], (DEPTH, D_MODEL, D_FF), D_MODEL ** -0.5),
        "w_ffn_down": nrm(ks[23], (DEPTH, D_FF, D_MODEL), D_FF ** -0.5),
    }


def reference(x_prompt, x_sample, c_prompt, c_sample, state_ret, state_conv, w_in, w_ada, b_ada,
              g_pre1, g_post1, g_pre2, g_post2, conv_w, conv_b, conv_ln_g, conv_ln_b, w_conv_out,
              ret_gn_g, w_ret_out, w_out, w_ffn_gate, w_ffn_up, w_ffn_down):
    pos_p = jnp.arange(SEQ, dtype=jnp.float32)
    pos_s = PAST_LEN + jnp.arange(DEC_SEQ, dtype=jnp.float32)
    xp, xs = x_prompt, x_sample
    ret_p_list, ret_s_list, conv_p_list, conv_s_list = [], [], [], []
    for l in range(DEPTH):
        layer_w = (w_in[l], w_ada[l], b_ada[l], g_pre1[l], g_post1[l], g_pre2[l], g_post2[l],
                   conv_w[l], conv_b[l], conv_ln_g[l], conv_ln_b[l], w_conv_out[l], ret_gn_g[l],
                   w_ret_out[l], w_out[l], w_ffn_gate[l], w_ffn_up[l], w_ffn_down[l])
        conv0 = jnp.zeros((BATCH, CONV_STATE, CONV_CH), xp.dtype)
        ret0 = jnp.zeros((BATCH, N_RET_HEADS, RET_DK, RET_DV), jnp.float32)
        xp, conv_p, ret_p = decoder_layer(xp, c_prompt, pos_p, conv0, ret0, *layer_w)
        xs, conv_s, ret_s = decoder_layer(xs, c_sample, pos_s, state_conv[l], state_ret[l], *layer_w)
        ret_p_list.append(ret_p.astype(x_prompt.dtype))
        ret_s_list.append(ret_s.astype(state_ret.dtype))
        conv_p_list.append(conv_p.astype(x_prompt.dtype))
        conv_s_list.append(conv_s.astype(state_conv.dtype))
    ret_state_prompt = jnp.stack(ret_p_list, axis=0)
    ret_state_sample = jnp.stack(ret_s_list, axis=0)
    conv_state_prompt = jnp.stack(conv_p_list, axis=0)
    conv_state_sample = jnp.stack(conv_s_list, axis=0)
    return (xp, xs, ret_state_prompt, ret_state_sample, conv_state_prompt, conv_state_sample)
```

```python
import functools

import jax
import jax.numpy as jnp
from jax import lax
from jax.experimental import pallas as pl
from jax.experimental.pallas import tpu as pltpu

F32 = jnp.float32
BF16 = jnp.bfloat16

D_MODEL = 1024
N_HEADS = 4
DK = 256
DV = 512
HALF = DK // 2
RET_QK = N_HEADS * DK
RET_V = N_HEADS * DV
CONV_WIDTH = 31
CONV_STATE = CONV_WIDTH - 1
EPS = 1e-6
ROPE_BASE = 10000.0
PAST_LEN = 16384

OFF_Q = 0
OFF_K = RET_QK
OFF_V = 2 * RET_QK
OFF_G = OFF_V + RET_V
OFF_U = OFF_G + RET_V
OFF_A = OFF_U + D_MODEL
OFF_GR = OFF_A + D_MODEL
OFF_GC = OFF_GR + D_MODEL
D_IN = OFF_GC + D_MODEL

LANES = 128
SUBLANES = 8
T_MIX = 256
HIST = 32
ROW_CHUNK = 64
T_FFN = 512
CONV_B = 16
MIB = 1024 * 1024


def _rms(x, g):
    ms = jnp.mean(x * x, axis=-1, keepdims=True)
    return x * lax.rsqrt(ms + EPS) * g


def _rotary(t, cos, sin):
    t1, t2 = t[:, :HALF], t[:, HALF:]
    return jnp.concatenate([t1 * cos - t2 * sin, t2 * cos + t1 * sin], axis=-1)


def _standardize(y):
    mu = jnp.mean(y, axis=-1, keepdims=True)
    yc = y - mu
    var = jnp.mean(yc * yc, axis=-1, keepdims=True)
    return yc * lax.rsqrt(var + EPS)


def _dot(a, b):
    return jnp.dot(a, b, preferred_element_type=F32)


def _const_spec(shape):
    return pl.BlockSpec(shape, lambda *_: (0,) * len(shape), pipeline_mode=pl.Buffered(1))


def _mod_kernel(c_ref, w_ref, b_ref, o_ref):
    s = jax.nn.silu(c_ref[...]).astype(BF16)
    o_ref[...] = _dot(s, w_ref[...].astype(BF16)) + b_ref[...]


def _modulation(c_all, w_ada, b_ada):
    rows = c_all.shape[0]
    n_out = w_ada.shape[1]
    tn = D_MODEL
    return pl.pallas_call(
        _mod_kernel,
        grid=(n_out // tn,),
        in_specs=[
            pl.BlockSpec((rows, D_MODEL), lambda j: (0, 0)),
            pl.BlockSpec((D_MODEL, tn), lambda j: (0, j)),
            pl.BlockSpec((1, tn), lambda j: (0, j)),
        ],
        out_specs=pl.BlockSpec((rows, tn), lambda j: (0, j)),
        out_shape=jax.ShapeDtypeStruct((rows, n_out), F32),
        compiler_params=pltpu.CompilerParams(dimension_semantics=("arbitrary",)),
        name="adaln_mod",
    )(c_all, w_ada, b_ada)


def _mixer_kernel(x_ref, mod_ref, gpre_ref, gpost_ref, cos_ref, sin_ref, decay_ref, rs_ref,
                  w_in_ref, convw_ref, convb_ref, lng_ref, lnb_ref, wco_ref, gng_ref, wro_ref, wo_ref,
                  x1_ref, rstate_ref, cstate_ref, full_ref, conv_ref):
    t = pl.program_id(1)
    T = T_MIX
    n_col = D_MODEL // LANES

    @pl.when(t == 0)
    def _():
        rstate_ref[...] = jnp.zeros_like(rstate_ref)
        full_ref[:, 0:HIST, :] = jnp.zeros((n_col, HIST, LANES), F32)

    x = x_ref[0]
    mod = mod_ref[0]
    sh1 = mod[:, 0:D_MODEL]
    sc1 = mod[:, D_MODEL:2 * D_MODEL]
    gt1 = mod[:, 2 * D_MODEL:3 * D_MODEL]
    hb = (_rms(x, gpre_ref[...]) * (1.0 + sc1) + sh1).astype(BF16)

    def proj(lo, width):
        return _dot(hb, w_in_ref[:, lo:lo + width])

    cos = cos_ref[...]
    sin = sin_ref[...]
    ret_out = jnp.zeros((T, D_MODEL), F32)
    for h in range(N_HEADS):
        q = _rotary(proj(OFF_Q + h * DK, DK), cos, sin) * (DK ** -0.5)
        k = _rotary(proj(OFF_K + h * DK, DK), cos, sin)
        vb = proj(OFF_V + h * DV, DV).astype(BF16)
        qb = q.astype(BF16)
        scores = lax.dot_general(qb, k.astype(BF16), (((1,), (1,)), ((), ())),
                                 preferred_element_type=F32) * decay_ref[h]
        s_prev = rstate_ref[0, h]
        y = _dot(scores.astype(BF16), vb)
        y = y + _dot(qb, s_prev.astype(BF16)) * rs_ref[:, h:h + 1]
        k_dec = (k * rs_ref[:, N_HEADS + h:N_HEADS + h + 1]).astype(BF16)
        upd = lax.dot_general(k_dec, vb, (((0,), (0,)), ((), ())), preferred_element_type=F32)
        rstate_ref[0, h] = rs_ref[0:1, 2 * N_HEADS + h:2 * N_HEADS + h + 1] * s_prev + upd
        yn = _standardize(y) * gng_ref[:, h * DV:(h + 1) * DV]
        gated = (jax.nn.silu(proj(OFF_G + h * DV, DV)) * yn).astype(BF16)
        ret_out = ret_out + _dot(gated, wro_ref[h * DV:(h + 1) * DV, :])

    glu = proj(OFF_U, D_MODEL) * jax.nn.sigmoid(proj(OFF_A, D_MODEL))
    for c in range(n_col):
        full_ref[c, HIST:HIST + T, :] = glu[:, c * LANES:(c + 1) * LANES]

    def conv_cols(c, carry):
        for r0 in range(0, T, ROW_CHUNK):
            acc = jnp.broadcast_to(convb_ref[c], (ROW_CHUNK, LANES))
            for j in range(CONV_WIDTH):
                start = HIST - CONV_STATE + j + r0
                acc = acc + full_ref[c, start:start + ROW_CHUNK, :] * convw_ref[c, j:j + 1, :]
            conv_ref[c, r0:r0 + ROW_CHUNK, :] = acc
        return carry

    lax.fori_loop(0, n_col, conv_cols, 0)

    @pl.when(t == pl.num_programs(1) - 1)
    def _():
        for c in range(n_col):
            cstate_ref[0, :, c * LANES:(c + 1) * LANES] = full_ref[c, HIST + T - CONV_STATE:HIST + T, :]

    full_ref[:, 0:HIST, :] = full_ref[:, T:T + HIST, :]

    conv = jnp.concatenate([conv_ref[c] for c in range(n_col)], axis=-1)
    ln = _standardize(conv) * lng_ref[...] + lnb_ref[...]
    conv_out = _dot(jax.nn.silu(ln).astype(BF16), wco_ref[...])

    merged = (jax.nn.sigmoid(proj(OFF_GR, D_MODEL)) * ret_out
              + jax.nn.sigmoid(proj(OFF_GC, D_MODEL)) * conv_out)
    mix = _dot(merged.astype(BF16), wo_ref[...])
    x1_ref[0] = x + gt1 * _rms(mix, gpost_ref[...])


def _prompt_mixer(x, mod_p, g_pre1, g_post1, cos_p, sin_p, decay, rs_tab, w_in_b, conv_w_c, conv_b_c,
                  ln_g, ln_b, w_conv_out_b, gn_g, w_ret_out_b, w_out_b):
    B, L, _ = x.shape
    T = T_MIX
    n_col = D_MODEL // LANES
    row = lambda b, t: (0, 0)
    in_specs = [
        pl.BlockSpec((1, T, D_MODEL), lambda b, t: (b, t, 0)),
        pl.BlockSpec((1, 1, 6 * D_MODEL), lambda b, t: (b, 0, 0)),
        pl.BlockSpec((1, D_MODEL), row),
        pl.BlockSpec((1, D_MODEL), row),
        pl.BlockSpec((T, HALF), lambda b, t: (t, 0)),
        pl.BlockSpec((T, HALF), lambda b, t: (t, 0)),
        _const_spec((N_HEADS, T, T)),
        _const_spec((T, LANES)),
        _const_spec((D_MODEL, D_IN)),
        _const_spec((n_col, CONV_WIDTH, LANES)),
        _const_spec((n_col, 1, LANES)),
        pl.BlockSpec((1, D_MODEL), row),
        pl.BlockSpec((1, D_MODEL), row),
        _const_spec((D_MODEL, D_MODEL)),
        pl.BlockSpec((1, RET_V), row),
        _const_spec((RET_V, D_MODEL)),
        _const_spec((D_MODEL, D_MODEL)),
    ]
    out_specs = [
        pl.BlockSpec((1, T, D_MODEL), lambda b, t: (b, t, 0)),
        pl.BlockSpec((1, N_HEADS, DK, DV), lambda b, t: (b, 0, 0, 0)),
        pl.BlockSpec((1, CONV_STATE, D_MODEL), lambda b, t: (b, 0, 0)),
    ]
    out_shape = [
        jax.ShapeDtypeStruct((B, L, D_MODEL), F32),
        jax.ShapeDtypeStruct((B, N_HEADS, DK, DV), F32),
        jax.ShapeDtypeStruct((B, CONV_STATE, D_MODEL), F32),
    ]
    return pl.pallas_call(
        _mixer_kernel,
        grid=(B, L // T),
        in_specs=in_specs,
        out_specs=out_specs,
        out_shape=out_shape,
        scratch_shapes=[
            pltpu.VMEM((n_col, HIST + T, LANES), F32),
            pltpu.VMEM((n_col, T, LANES), F32),
        ],
        compiler_params=pltpu.CompilerParams(
            dimension_semantics=("arbitrary", "arbitrary"),
            vmem_limit_bytes=56 * MIB),
        name="prompt_mixer",
    )(x, mod_p, g_pre1, g_post1, cos_p, sin_p, decay, rs_tab, w_in_b, conv_w_c, conv_b_c,
      ln_g, ln_b, w_conv_out_b, gn_g, w_ret_out_b, w_out_b)


def _ffn_body(x, sh2, sc2, gt2, gpre, gpost, wg_ref, wu_ref, wd_ref):
    hb = (_rms(x, gpre) * (1.0 + sc2) + sh2).astype(BF16)
    act = (jax.nn.silu(_dot(hb, wg_ref[...])) * _dot(hb, wu_ref[...])).astype(BF16)
    return x + gt2 * _rms(_dot(act, wd_ref[...]), gpost)


def _ffn_kernel(x_ref, mod_ref, gpre_ref, gpost_ref, wg_ref, wu_ref, wd_ref, o_ref):
    mod = mod_ref[0]
    o_ref[0] = _ffn_body(x_ref[0], mod[:, 3 * D_MODEL:4 * D_MODEL], mod[:, 4 * D_MODEL:5 * D_MODEL],
                         mod[:, 5 * D_MODEL:6 * D_MODEL], gpre_ref[...], gpost_ref[...],
                         wg_ref, wu_ref, wd_ref)


def _prompt_ffn(x1, mod_p, g_pre2, g_post2, wg_b, wu_b, wd_b):
    B, L, _ = x1.shape
    d_ff = wg_b.shape[1]
    row = lambda b, t: (0, 0)
    return pl.pallas_call(
        _ffn_kernel,
        grid=(B, L // T_FFN),
        in_specs=[
            pl.BlockSpec((1, T_FFN, D_MODEL), lambda b, t: (b, t, 0)),
            pl.BlockSpec((1, 1, 6 * D_MODEL), lambda b, t: (b, 0, 0)),
            pl.BlockSpec((1, D_MODEL), row),
            pl.BlockSpec((1, D_MODEL), row),
            _const_spec((D_MODEL, d_ff)),
            _const_spec((D_MODEL, d_ff)),
            _const_spec((d_ff, D_MODEL)),
        ],
        out_specs=pl.BlockSpec((1, T_FFN, D_MODEL), lambda b, t: (b, t, 0)),
        out_shape=jax.ShapeDtypeStruct((B, L, D_MODEL), F32),
        compiler_params=pltpu.CompilerParams(
            dimension_semantics=("arbitrary", "arbitrary"),
            vmem_limit_bytes=56 * MIB),
        name="prompt_ffn",
    )(x1, mod_p, g_pre2, g_post2, wg_b, wu_b, wd_b)


def _sample_in_kernel(x_ref, sh_ref, sc_ref, gpre_ref, w_ref, z_ref):
    hb = (_rms(x_ref[...], gpre_ref[...]) * (1.0 + sc_ref[...]) + sh_ref[...]).astype(BF16)
    z_ref[...] = _dot(hb, w_ref[...])


def _sample_in(xs, mod_s, g_pre1, w_in_b):
    rows = xs.shape[0]
    tn = D_MODEL
    return pl.pallas_call(
        _sample_in_kernel,
        grid=(D_IN // tn,),
        in_specs=[
            pl.BlockSpec((rows, D_MODEL), lambda j: (0, 0)),
            pl.BlockSpec((rows, D_MODEL), lambda j: (0, 0)),
            pl.BlockSpec((rows, D_MODEL), lambda j: (0, 1)),
            pl.BlockSpec((1, D_MODEL), lambda j: (0, 0)),
            pl.BlockSpec((D_MODEL, tn), lambda j: (0, j)),
        ],
        out_specs=pl.BlockSpec((rows, tn), lambda j: (0, j)),
        out_shape=jax.ShapeDtypeStruct((rows, D_IN), F32),
        compiler_params=pltpu.CompilerParams(dimension_semantics=("arbitrary",)),
        name="sample_in_proj",
    )(xs, mod_s, mod_s, g_pre1, w_in_b)


def _sample_ret_kernel(q_ref, k_ref, v_ref, cos_ref, sin_ref, gam_ref, s0_ref, snew_ref, y_ref):
    cos = cos_ref[...]
    sin = sin_ref[...]
    q = _rotary(q_ref[0], cos, sin) * (DK ** -0.5)
    k = _rotary(k_ref[0], cos, sin)
    v = v_ref[0]
    qk = jnp.sum(q * k, axis=-1, keepdims=True)
    k_cols = jnp.concatenate([k, jnp.zeros((LANES - N_HEADS, DK), F32)], axis=0).T.astype(BF16)
    q_rows = jnp.concatenate([q, jnp.zeros((SUBLANES - N_HEADS, DK), F32)], axis=0).astype(BF16)
    row_id = lax.broadcasted_iota(jnp.int32, (LANES, DV), 0)
    for h in range(N_HEADS):
        s0 = s0_ref[0, h]
        v_h = v[h:h + 1, :]
        gam = gam_ref[h:h + 1, :]
        v_sel = jnp.where(row_id == h, v_h, 0.0).astype(BF16)
        snew_ref[0, h] = gam * s0 + _dot(k_cols, v_sel)
        qs = _dot(q_rows, s0.astype(BF16))
        y_ref[0, h:h + 1, :] = qk[h:h + 1, :] * v_h + gam * qs[h:h + 1, :]


def _sample_retention(q, k, v, cos_s, sin_s, gam_tab, state):
    nb = q.shape[0]
    one = lambda b: (0, 0)
    return pl.pallas_call(
        _sample_ret_kernel,
        grid=(nb,),
        in_specs=[
            pl.BlockSpec((1, N_HEADS, DK), lambda b: (b, 0, 0)),
            pl.BlockSpec((1, N_HEADS, DK), lambda b: (b, 0, 0)),
            pl.BlockSpec((1, N_HEADS, DV), lambda b: (b, 0, 0)),
            pl.BlockSpec((1, HALF), one),
            pl.BlockSpec((1, HALF), one),
            pl.BlockSpec((N_HEADS, DV), one),
            pl.BlockSpec((1, N_HEADS, DK, DV), lambda b: (b, 0, 0, 0)),
        ],
        out_specs=[
            pl.BlockSpec((1, N_HEADS, DK, DV), lambda b: (b, 0, 0, 0)),
            pl.BlockSpec((1, N_HEADS, DV), lambda b: (b, 0, 0)),
        ],
        out_shape=[
            jax.ShapeDtypeStruct((nb, N_HEADS, DK, DV), F32),
            jax.ShapeDtypeStruct((nb, N_HEADS, DV), F32),
        ],
        compiler_params=pltpu.CompilerParams(dimension_semantics=("arbitrary",)),
        name="sample_retention",
    )(q, k, v, cos_s, sin_s, gam_tab, state)


def _sample_conv_kernel(u_ref, a_ref, st_ref, w_ref, b_ref, new_ref, conv_ref):
    glu = u_ref[...] * jax.nn.sigmoid(a_ref[...])
    st = st_ref[...]
    w = w_ref[...]
    conv = jnp.sum(st * w[None, 0:CONV_STATE, :], axis=1)
    conv_ref[...] = conv + glu * w[CONV_STATE:CONV_WIDTH, :] + b_ref[...]
    new_ref[:, 0:CONV_STATE - 1, :] = st[:, 1:CONV_STATE, :]
    new_ref[:, CONV_STATE - 1:CONV_STATE, :] = glu[:, None, :]


def _sample_conv(z_s, state_conv, conv_w, conv_b):
    nb = z_s.shape[0]
    return pl.pallas_call(
        _sample_conv_kernel,
        grid=(nb // CONV_B,),
        in_specs=[
            pl.BlockSpec((CONV_B, D_MODEL), lambda i: (i, OFF_U // D_MODEL)),
            pl.BlockSpec((CONV_B, D_MODEL), lambda i: (i, OFF_A // D_MODEL)),
            pl.BlockSpec((CONV_B, CONV_STATE, D_MODEL), lambda i: (i, 0, 0)),
            pl.BlockSpec((CONV_WIDTH, D_MODEL), lambda i: (0, 0)),
            pl.BlockSpec((1, D_MODEL), lambda i: (0, 0)),
        ],
        out_specs=[
            pl.BlockSpec((CONV_B, CONV_STATE, D_MODEL), lambda i: (i, 0, 0)),
            pl.BlockSpec((CONV_B, D_MODEL), lambda i: (i, 0)),
        ],
        out_shape=[
            jax.ShapeDtypeStruct((nb, CONV_STATE, D_MODEL), F32),
            jax.ShapeDtypeStruct((nb, D_MODEL), F32),
        ],
        compiler_params=pltpu.CompilerParams(dimension_semantics=("arbitrary",)),
        name="sample_conv",
    )(z_s, z_s, state_conv, conv_w, conv_b)


def _sample_post_kernel(x_ref, mod_ref, z_ref, y_ref, conv_ref, gng_ref, wro_ref, lng_ref, lnb_ref,
                        wco_ref, wo_ref, gpost1_ref, gpre2_ref, gpost2_ref, wg_ref, wu_ref, wd_ref, o_ref):
    x = x_ref[...]
    rows = x.shape[0]
    ret_out = jnp.zeros((rows, D_MODEL), F32)
    for h in range(N_HEADS):
        sl = slice(h * DV, (h + 1) * DV)
        yn = _standardize(y_ref[:, sl]) * gng_ref[:, sl]
        gated = (jax.nn.silu(z_ref[:, OFF_G + h * DV:OFF_G + (h + 1) * DV]) * yn).astype(BF16)
        ret_out = ret_out + _dot(gated, wro_ref[sl, :])
    ln = _standardize(conv_ref[...]) * lng_ref[...] + lnb_ref[...]
    conv_out = _dot(jax.nn.silu(ln).astype(BF16), wco_ref[...])
    merged = (jax.nn.sigmoid(z_ref[:, OFF_GR:OFF_GR + D_MODEL]) * ret_out
              + jax.nn.sigmoid(z_ref[:, OFF_GC:OFF_GC + D_MODEL]) * conv_out)
    mix = _dot(merged.astype(BF16), wo_ref[...])
    x1 = x + mod_ref[:, 2 * D_MODEL:3 * D_MODEL] * _rms(mix, gpost1_ref[...])
    o_ref[...] = _ffn_body(x1, mod_ref[:, 3 * D_MODEL:4 * D_MODEL], mod_ref[:, 4 * D_MODEL:5 * D_MODEL],
                           mod_ref[:, 5 * D_MODEL:6 * D_MODEL], gpre2_ref[...], gpost2_ref[...],
                           wg_ref, wu_ref, wd_ref)


def _sample_post(xs, mod_s, z_s, y_s, conv_s, gn_g, w_ret_out_b, ln_g, ln_b, w_conv_out_b, w_out_b,
                 g_post1, g_pre2, g_post2, wg_b, wu_b, wd_b):
    return pl.pallas_call(
        _sample_post_kernel,
        out_shape=jax.ShapeDtypeStruct(xs.shape, F32),
        compiler_params=pltpu.CompilerParams(vmem_limit_bytes=56 * MIB),
        name="sample_post",
    )(xs, mod_s, z_s, y_s, conv_s, gn_g, w_ret_out_b, ln_g, ln_b, w_conv_out_b, w_out_b,
      g_post1, g_pre2, g_post2, wg_b, wu_b, wd_b)


def _rope_tables(pos):
    inv_freq = ROPE_BASE ** (-jnp.arange(0, DK, 2, dtype=F32) / DK)
    ang = pos[:, None] * inv_freq[None, :]
    return jnp.cos(ang), jnp.sin(ang)


def _log_gamma():
    return jnp.log(1.0 - jnp.exp(jnp.linspace(jnp.log(1.0 / 32.0), jnp.log(1.0 / 512.0), N_HEADS)))


def _decay_tables(chunk):
    lg = _log_gamma()
    idx = jnp.arange(chunk, dtype=F32)
    diff = idx[:, None] - idx[None, :]
    decay = jnp.where(diff >= 0, jnp.exp(lg[:, None, None] * jnp.maximum(diff, 0.0)), 0.0)
    cross = jnp.exp(lg[None, :] * (idx[:, None] + 1.0))
    k_dec = jnp.exp(lg[None, :] * (chunk - 1.0 - idx[:, None]))
    full = jnp.broadcast_to(jnp.exp(lg * chunk)[None, :], (chunk, N_HEADS))
    rs = jnp.concatenate([cross, k_dec, full], axis=1)
    rs = jnp.pad(rs, ((0, 0), (0, LANES - rs.shape[1])))
    return decay.astype(F32), rs.astype(F32)


def kernel(x_prompt, x_sample, c_prompt, c_sample, state_ret, state_conv, w_in, w_ada, b_ada, g_pre1, g_post1, g_pre2, g_post2, conv_w, conv_b, conv_ln_g, conv_ln_b, w_conv_out, ret_gn_g, w_ret_out, w_out, w_ffn_gate, w_ffn_up, w_ffn_down):
    depth = w_in.shape[0]
    assert depth == 1, "single-layer step"
    B, L, _ = x_prompt.shape
    nb = x_sample.shape[0]
    n_col = D_MODEL // LANES

    w_in_b = w_in[0].astype(BF16)
    w_ret_out_b = w_ret_out[0].astype(BF16)
    w_conv_out_b = w_conv_out[0].astype(BF16)
    w_out_b = w_out[0].astype(BF16)
    wg_b = w_ffn_gate[0].astype(BF16)
    wu_b = w_ffn_up[0].astype(BF16)
    wd_b = w_ffn_down[0].astype(BF16)

    cos_p, sin_p = _rope_tables(jnp.arange(L, dtype=F32))
    cos_s, sin_s = _rope_tables(PAST_LEN + jnp.arange(1, dtype=F32))
    decay, rs_tab = _decay_tables(T_MIX)
    gam_tab = jnp.broadcast_to(jnp.exp(_log_gamma())[:, None], (N_HEADS, DV)).astype(F32)

    c_all = jnp.concatenate([c_prompt, c_sample], axis=0)
    mod = _modulation(c_all, w_ada[0], b_ada)
    mod_p = mod[:B].reshape(B, 1, 6 * D_MODEL)
    mod_s = mod[B:]

    conv_w_c = conv_w[0].reshape(CONV_WIDTH, n_col, LANES).transpose(1, 0, 2)
    conv_b_c = conv_b[0].reshape(n_col, 1, LANES)

    x1_p, ret_p, conv_p = _prompt_mixer(
        x_prompt, mod_p, g_pre1, g_post1, cos_p, sin_p, decay, rs_tab, w_in_b, conv_w_c, conv_b_c,
        conv_ln_g, conv_ln_b, w_conv_out_b, ret_gn_g, w_ret_out_b, w_out_b)
    y_p = _prompt_ffn(x1_p, mod_p, g_pre2, g_post2, wg_b, wu_b, wd_b)

    xs = x_sample.reshape(nb, D_MODEL)
    z_s = _sample_in(xs, mod_s, g_pre1, w_in_b)
    q_s = z_s[:, OFF_Q:OFF_Q + RET_QK].reshape(nb, N_HEADS, DK)
    k_s = z_s[:, OFF_K:OFF_K + RET_QK].reshape(nb, N_HEADS, DK)
    v_s = z_s[:, OFF_V:OFF_V + RET_V].reshape(nb, N_HEADS, DV)
    ret_s, y_s = _sample_retention(q_s, k_s, v_s, cos_s, sin_s, gam_tab, state_ret[0])
    conv_new_s, conv_s = _sample_conv(z_s, state_conv[0], conv_w[0], conv_b)
    y_s = _sample_post(xs, mod_s, z_s, y_s.reshape(nb, RET_V), conv_s, ret_gn_g, w_ret_out_b,
                       conv_ln_g, conv_ln_b, w_conv_out_b, w_out_b, g_post1, g_pre2, g_post2,
                       wg_b, wu_b, wd_b)

    return (y_p, y_s.reshape(x_sample.shape), ret_p[None], ret_s[None], conv_p[None], conv_new_s[None])
```

```python
import jax
import jax.numpy as jnp
from jax import lax
from jax.experimental import pallas as pl
from jax.experimental.pallas import tpu as pltpu

F32 = jnp.float32
BF16 = jnp.bfloat16

D_MODEL = 1024
N_HEADS = 4
DK = 256
DV = 512
HALF = DK // 2
RET_QK = N_HEADS * DK
RET_V = N_HEADS * DV
CONV_WIDTH = 31
CONV_STATE = CONV_WIDTH - 1
EPS = 1e-6
ROPE_BASE = 10000.0
PAST_LEN = 16384

OFF_Q = 0
OFF_K = RET_QK
OFF_V = 2 * RET_QK
OFF_G = OFF_V + RET_V
OFF_U = OFF_G + RET_V
OFF_A = OFF_U + D_MODEL
OFF_GR = OFF_A + D_MODEL
OFF_GC = OFF_GR + D_MODEL
D_IN = OFF_GC + D_MODEL

LANES = 128
T_MIX = 256
HIST = 32
ROW_CHUNK = 64
T_FFN = 256
CONV_B = 16
MIB = 1024 * 1024


def _rms(x, g):
    ms = jnp.mean(x * x, axis=-1, keepdims=True)
    return x * lax.rsqrt(ms + EPS) * g


def _rotary(t, cos, sin):
    t1, t2 = t[:, :HALF], t[:, HALF:]
    return jnp.concatenate([t1 * cos - t2 * sin, t2 * cos + t1 * sin], axis=-1)


def _standardize(y):
    mu = jnp.mean(y, axis=-1, keepdims=True)
    yc = y - mu
    var = jnp.mean(yc * yc, axis=-1, keepdims=True)
    return yc * lax.rsqrt(var + EPS)


def _dot(a, b):
    return jnp.dot(a, b, preferred_element_type=F32)


def _const_spec(shape):
    return pl.BlockSpec(shape, lambda *_: (0,) * len(shape), pipeline_mode=pl.Buffered(1))


def _mod_kernel(c_ref, w_ref, b_ref, o_ref):
    s = jax.nn.silu(c_ref[...]).astype(BF16)
    o_ref[...] = _dot(s, w_ref[...].astype(BF16)) + b_ref[...]


def _modulation(c_all, w_ada, b_ada):
    rows = c_all.shape[0]
    n_out = w_ada.shape[1]
    tn = D_MODEL
    return pl.pallas_call(
        _mod_kernel,
        grid=(n_out // tn,),
        in_specs=[
            pl.BlockSpec((rows, D_MODEL), lambda j: (0, 0)),
            pl.BlockSpec((D_MODEL, tn), lambda j: (0, j)),
            pl.BlockSpec((1, tn), lambda j: (0, j)),
        ],
        out_specs=pl.BlockSpec((rows, tn), lambda j: (0, j)),
        out_shape=jax.ShapeDtypeStruct((rows, n_out), F32),
        compiler_params=pltpu.CompilerParams(dimension_semantics=("arbitrary",)),
        name="adaln_mod",
    )(c_all, w_ada, b_ada)


def _mixer_kernel(x_ref, mod_ref, gpre_ref, gpost_ref, cos_ref, sin_ref, decay_ref, rs_ref,
                  w_in_ref, convw_ref, convb_ref, lng_ref, lnb_ref, wco_ref, gng_ref, wro_ref, wo_ref,
                  x1_ref, rstate_ref, cstate_ref, full_ref, conv_ref, gated_ref):
    t = pl.program_id(1)
    T = T_MIX
    n_col = D_MODEL // LANES

    @pl.when(t == 0)
    def _():
        rstate_ref[...] = jnp.zeros_like(rstate_ref)
        full_ref[:, 0:HIST, :] = jnp.zeros((n_col, HIST, LANES), F32)

    mod = mod_ref[0]
    sh1 = mod[:, 0:D_MODEL]
    sc1 = mod[:, D_MODEL:2 * D_MODEL]
    gt1 = mod[:, 2 * D_MODEL:3 * D_MODEL]
    hb = (_rms(x_ref[0], gpre_ref[...]) * (1.0 + sc1) + sh1).astype(BF16)

    def proj(lo, width):
        return _dot(hb, w_in_ref[:, lo:lo + width])

    glu = proj(OFF_U, D_MODEL) * jax.nn.sigmoid(proj(OFF_A, D_MODEL))
    for c in range(n_col):
        full_ref[c, HIST:HIST + T, :] = glu[:, c * LANES:(c + 1) * LANES]

    @pl.when(t == pl.num_programs(1) - 1)
    def _():
        for c in range(n_col):
            cstate_ref[0, :, c * LANES:(c + 1) * LANES] = full_ref[c, HIST + T - CONV_STATE:HIST + T, :]

    def conv_block(c):
        lanes = slice(c * LANES, (c + 1) * LANES)
        for r0 in range(0, T, ROW_CHUNK):
            acc = jnp.broadcast_to(convb_ref[:, lanes], (ROW_CHUNK, LANES))
            for j in range(CONV_WIDTH):
                start = HIST - CONV_STATE + j + r0
                acc = acc + full_ref[c, start:start + ROW_CHUNK, :] * convw_ref[j:j + 1, lanes]
            conv_ref[c, r0:r0 + ROW_CHUNK, :] = acc

    blocks_per_head = n_col // N_HEADS
    for h in range(N_HEADS):
        q = _rotary(proj(OFF_Q + h * DK, DK), cos_ref[...], sin_ref[...]) * (DK ** -0.5)
        k = _rotary(proj(OFF_K + h * DK, DK), cos_ref[...], sin_ref[...])
        vb = proj(OFF_V + h * DV, DV).astype(BF16)
        qb = q.astype(BF16)
        conv_block(blocks_per_head * h)
        scores = lax.dot_general(qb, k.astype(BF16), (((1,), (1,)), ((), ())),
                                 preferred_element_type=F32) * decay_ref[h]
        s_prev = rstate_ref[0, h]
        y = _dot(scores.astype(BF16), vb)
        y = y + _dot(qb, s_prev.astype(BF16)) * rs_ref[:, h:h + 1]
        k_dec = (k * rs_ref[:, N_HEADS + h:N_HEADS + h + 1]).astype(BF16)
        upd = lax.dot_general(k_dec, vb, (((0,), (0,)), ((), ())), preferred_element_type=F32)
        rstate_ref[0, h] = rs_ref[0:1, 2 * N_HEADS + h:2 * N_HEADS + h + 1] * s_prev + upd
        for c in range(blocks_per_head * h + 1, blocks_per_head * (h + 1)):
            conv_block(c)
        yn = _standardize(y) * gng_ref[:, h * DV:(h + 1) * DV]
        gated_ref[:, h * DV:(h + 1) * DV] = (jax.nn.silu(proj(OFF_G + h * DV, DV)) * yn).astype(BF16)

    full_ref[:, 0:HIST, :] = full_ref[:, T:T + HIST, :]
    conv = jnp.concatenate([conv_ref[c] for c in range(n_col)], axis=-1)
    ln = _standardize(conv) * lng_ref[...] + lnb_ref[...]
    conv_out = _dot(jax.nn.silu(ln).astype(BF16), wco_ref[...])
    ret_out = _dot(gated_ref[...], wro_ref[...])

    merged = (jax.nn.sigmoid(proj(OFF_GR, D_MODEL)) * ret_out
              + jax.nn.sigmoid(proj(OFF_GC, D_MODEL)) * conv_out)
    mix = _dot(merged.astype(BF16), wo_ref[...])
    x1_ref[0] = x_ref[0] + gt1 * _rms(mix, gpost_ref[...])


def _prompt_mixer(x, mod_p, g_pre1, g_post1, cos_p, sin_p, decay, rs_tab, w_in_b, conv_w, conv_b,
                  ln_g, ln_b, w_conv_out_b, gn_g, w_ret_out_b, w_out_b):
    B, L, _ = x.shape
    T = T_MIX
    row = lambda b, t: (0, 0)
    in_specs = [
        pl.BlockSpec((1, T, D_MODEL), lambda b, t: (b, t, 0)),
        pl.BlockSpec((1, 1, 6 * D_MODEL), lambda b, t: (b, 0, 0)),
        pl.BlockSpec((1, D_MODEL), row),
        pl.BlockSpec((1, D_MODEL), row),
        pl.BlockSpec((T, HALF), lambda b, t: (t, 0)),
        pl.BlockSpec((T, HALF), lambda b, t: (t, 0)),
        _const_spec((N_HEADS, T, T)),
        _const_spec((T, LANES)),
        _const_spec((D_MODEL, D_IN)),
        pl.BlockSpec((CONV_WIDTH, D_MODEL), row),
        pl.BlockSpec((1, D_MODEL), row),
        pl.BlockSpec((1, D_MODEL), row),
        pl.BlockSpec((1, D_MODEL), row),
        _const_spec((D_MODEL, D_MODEL)),
        pl.BlockSpec((1, RET_V), row),
        _const_spec((RET_V, D_MODEL)),
        _const_spec((D_MODEL, D_MODEL)),
    ]
    out_specs = [
        pl.BlockSpec((1, T, D_MODEL), lambda b, t: (b, t, 0)),
        pl.BlockSpec((1, N_HEADS, DK, DV), lambda b, t: (b, 0, 0, 0)),
        pl.BlockSpec((1, CONV_STATE, D_MODEL), lambda b, t: (b, 0, 0)),
    ]
    out_shape = [
        jax.ShapeDtypeStruct((B, L, D_MODEL), F32),
        jax.ShapeDtypeStruct((B, N_HEADS, DK, DV), F32),
        jax.ShapeDtypeStruct((B, CONV_STATE, D_MODEL), F32),
    ]
    return pl.pallas_call(
        _mixer_kernel,
        grid=(B, L // T),
        in_specs=in_specs,
        out_specs=out_specs,
        out_shape=out_shape,
        scratch_shapes=[
            pltpu.VMEM((D_MODEL // LANES, HIST + T, LANES), F32),
            pltpu.VMEM((D_MODEL // LANES, T, LANES), F32),
            pltpu.VMEM((T, RET_V), BF16),
        ],
        compiler_params=pltpu.CompilerParams(
            dimension_semantics=("arbitrary", "arbitrary"),
            vmem_limit_bytes=56 * MIB),
        name="prompt_mixer",
    )(x, mod_p, g_pre1, g_post1, cos_p, sin_p, decay, rs_tab, w_in_b, conv_w, conv_b,
      ln_g, ln_b, w_conv_out_b, gn_g, w_ret_out_b, w_out_b)


def _ffn_body(x, sh2, sc2, gt2, gpre, gpost, wg_ref, wu_ref, wd_ref):
    hb = (_rms(x, gpre) * (1.0 + sc2) + sh2).astype(BF16)
    act = (jax.nn.silu(_dot(hb, wg_ref[...])) * _dot(hb, wu_ref[...])).astype(BF16)
    return x + gt2 * _rms(_dot(act, wd_ref[...]), gpost)


def _sample_ret_step(q, k, v, cos, sin, gam_ref, s0_ref, snew_ref, y_ref, i):
    q = _rotary(q, cos, sin) * (DK ** -0.5)
    k = _rotary(k, cos, sin)
    qk = jnp.sum(q * k, axis=-1, keepdims=True)
    cols = jnp.concatenate([k, q, jnp.zeros((LANES - 2 * N_HEADS, DK), F32)], axis=0).T
    for h in range(N_HEADS):
        s0 = s0_ref[i, h]
        v_h = v[h:h + 1, :]
        gam = gam_ref[h:h + 1, :]
        snew_ref[i, h] = gam * s0 + cols[:, h:h + 1] * v_h
        qs = jnp.sum(cols[:, N_HEADS + h:N_HEADS + h + 1] * s0, axis=0, keepdims=True)
        y_ref[i, h:h + 1, :] = qk[h:h + 1, :] * v_h + gam * qs


def _ffn_ret_kernel(x_ref, mod_ref, gpre_ref, gpost_ref, wg_ref, wu_ref, wd_ref,
                    q_ref, k_ref, v_ref, cos_ref, sin_ref, gam_ref, s0_ref,
                    o_ref, snew_ref, y_ref):
    mod = mod_ref[0]
    o_ref[0] = _ffn_body(x_ref[0], mod[:, 3 * D_MODEL:4 * D_MODEL], mod[:, 4 * D_MODEL:5 * D_MODEL],
                         mod[:, 5 * D_MODEL:6 * D_MODEL], gpre_ref[...], gpost_ref[...],
                         wg_ref, wu_ref, wd_ref)
    for i in range(q_ref.shape[0]):
        _sample_ret_step(q_ref[i], k_ref[i], v_ref[i], cos_ref[...], sin_ref[...], gam_ref,
                         s0_ref, snew_ref, y_ref, i)


def _prompt_ffn_sample_ret(x1, mod_p, g_pre2, g_post2, wg_b, wu_b, wd_b,
                           q_s, k_s, v_s, cos_s, sin_s, gam_tab, state):
    B, L, _ = x1.shape
    nb = q_s.shape[0]
    d_ff = wg_b.shape[1]
    nt = L // T_FFN
    steps = B * nt
    sb = nb // steps
    assert sb * steps == nb
    row = lambda b, t: (0, 0)
    blk = lambda b, t: (b * nt + t, 0, 0)
    return pl.pallas_call(
        _ffn_ret_kernel,
        grid=(B, nt),
        in_specs=[
            pl.BlockSpec((1, T_FFN, D_MODEL), lambda b, t: (b, t, 0)),
            pl.BlockSpec((1, 1, 6 * D_MODEL), lambda b, t: (b, 0, 0)),
            pl.BlockSpec((1, D_MODEL), row),
            pl.BlockSpec((1, D_MODEL), row),
            _const_spec((D_MODEL, d_ff)),
            _const_spec((D_MODEL, d_ff)),
            _const_spec((d_ff, D_MODEL)),
            pl.BlockSpec((sb, N_HEADS, DK), blk),
            pl.BlockSpec((sb, N_HEADS, DK), blk),
            pl.BlockSpec((sb, N_HEADS, DV), blk),
            pl.BlockSpec((1, HALF), row),
            pl.BlockSpec((1, HALF), row),
            pl.BlockSpec((N_HEADS, DV), row),
            pl.BlockSpec((sb, N_HEADS, DK, DV), lambda b, t: (b * nt + t, 0, 0, 0)),
        ],
        out_specs=[
            pl.BlockSpec((1, T_FFN, D_MODEL), lambda b, t: (b, t, 0)),
            pl.BlockSpec((sb, N_HEADS, DK, DV), lambda b, t: (b * nt + t, 0, 0, 0)),
            pl.BlockSpec((sb, N_HEADS, DV), blk),
        ],
        out_shape=[
            jax.ShapeDtypeStruct((B, L, D_MODEL), F32),
            jax.ShapeDtypeStruct((nb, N_HEADS, DK, DV), F32),
            jax.ShapeDtypeStruct((nb, N_HEADS, DV), F32),
        ],
        compiler_params=pltpu.CompilerParams(
            dimension_semantics=("arbitrary", "arbitrary"),
            vmem_limit_bytes=56 * MIB),
        name="prompt_ffn_sample_ret",
    )(x1, mod_p, g_pre2, g_post2, wg_b, wu_b, wd_b, q_s, k_s, v_s, cos_s, sin_s, gam_tab, state)


def _sample_in_kernel(x_ref, sh_ref, sc_ref, gpre_ref, w_ref, z_ref):
    hb = (_rms(x_ref[...], gpre_ref[...]) * (1.0 + sc_ref[...]) + sh_ref[...]).astype(BF16)
    z_ref[...] = _dot(hb, w_ref[...])


def _sample_in(xs, mod_s, g_pre1, w_in_b):
    rows = xs.shape[0]
    tn = D_MODEL
    return pl.pallas_call(
        _sample_in_kernel,
        grid=(D_IN // tn,),
        in_specs=[
            pl.BlockSpec((rows, D_MODEL), lambda j: (0, 0)),
            pl.BlockSpec((rows, D_MODEL), lambda j: (0, 0)),
            pl.BlockSpec((rows, D_MODEL), lambda j: (0, 1)),
            pl.BlockSpec((1, D_MODEL), lambda j: (0, 0)),
            pl.BlockSpec((D_MODEL, tn), lambda j: (0, j)),
        ],
        out_specs=pl.BlockSpec((rows, tn), lambda j: (0, j)),
        out_shape=jax.ShapeDtypeStruct((rows, D_IN), F32),
        compiler_params=pltpu.CompilerParams(dimension_semantics=("arbitrary",)),
        name="sample_in_proj",
    )(xs, mod_s, mod_s, g_pre1, w_in_b)


def _sample_conv_kernel(u_ref, a_ref, st_ref, w_ref, b_ref, new_ref, conv_ref):
    glu = u_ref[...] * jax.nn.sigmoid(a_ref[...])
    st = st_ref[...]
    w = w_ref[...]
    conv = jnp.sum(st * w[None, 0:CONV_STATE, :], axis=1)
    conv_ref[...] = conv + glu * w[CONV_STATE:CONV_WIDTH, :] + b_ref[...]
    new_ref[:, 0:CONV_STATE - 1, :] = st[:, 1:CONV_STATE, :]
    new_ref[:, CONV_STATE - 1:CONV_STATE, :] = glu[:, None, :]


def _sample_conv(z_s, state_conv, conv_w, conv_b):
    nb = z_s.shape[0]
    return pl.pallas_call(
        _sample_conv_kernel,
        grid=(nb // CONV_B,),
        in_specs=[
            pl.BlockSpec((CONV_B, D_MODEL), lambda i: (i, OFF_U // D_MODEL)),
            pl.BlockSpec((CONV_B, D_MODEL), lambda i: (i, OFF_A // D_MODEL)),
            pl.BlockSpec((CONV_B, CONV_STATE, D_MODEL), lambda i: (i, 0, 0)),
            pl.BlockSpec((CONV_WIDTH, D_MODEL), lambda i: (0, 0)),
            pl.BlockSpec((1, D_MODEL), lambda i: (0, 0)),
        ],
        out_specs=[
            pl.BlockSpec((CONV_B, CONV_STATE, D_MODEL), lambda i: (i, 0, 0)),
            pl.BlockSpec((CONV_B, D_MODEL), lambda i: (i, 0)),
        ],
        out_shape=[
            jax.ShapeDtypeStruct((nb, CONV_STATE, D_MODEL), F32),
            jax.ShapeDtypeStruct((nb, D_MODEL), F32),
        ],
        compiler_params=pltpu.CompilerParams(dimension_semantics=("arbitrary",)),
        name="sample_conv",
    )(z_s, z_s, state_conv, conv_w, conv_b)


def _sample_post_kernel(x_ref, mod_ref, z_ref, y_ref, conv_ref, gng_ref, wro_ref, lng_ref, lnb_ref,
                        wco_ref, wo_ref, gpost1_ref, gpre2_ref, gpost2_ref, wg_ref, wu_ref, wd_ref, o_ref):
    x = x_ref[...]
    rows = x.shape[0]
    ret_out = jnp.zeros((rows, D_MODEL), F32)
    for h in range(N_HEADS):
        sl = slice(h * DV, (h + 1) * DV)
        yn = _standardize(y_ref[:, sl]) * gng_ref[:, sl]
        gated = (jax.nn.silu(z_ref[:, OFF_G + h * DV:OFF_G + (h + 1) * DV]) * yn).astype(BF16)
        ret_out = ret_out + _dot(gated, wro_ref[sl, :])
    ln = _standardize(conv_ref[...]) * lng_ref[...] + lnb_ref[...]
    conv_out = _dot(jax.nn.silu(ln).astype(BF16), wco_ref[...])
    merged = (jax.nn.sigmoid(z_ref[:, OFF_GR:OFF_GR + D_MODEL]) * ret_out
              + jax.nn.sigmoid(z_ref[:, OFF_GC:OFF_GC + D_MODEL]) * conv_out)
    mix = _dot(merged.astype(BF16), wo_ref[...])
    x1 = x + mod_ref[:, 2 * D_MODEL:3 * D_MODEL] * _rms(mix, gpost1_ref[...])
    o_ref[...] = _ffn_body(x1, mod_ref[:, 3 * D_MODEL:4 * D_MODEL], mod_ref[:, 4 * D_MODEL:5 * D_MODEL],
                           mod_ref[:, 5 * D_MODEL:6 * D_MODEL], gpre2_ref[...], gpost2_ref[...],
                           wg_ref, wu_ref, wd_ref)


def _sample_post(xs, mod_s, z_s, y_s, conv_s, gn_g, w_ret_out_b, ln_g, ln_b, w_conv_out_b, w_out_b,
                 g_post1, g_pre2, g_post2, wg_b, wu_b, wd_b):
    return pl.pallas_call(
        _sample_post_kernel,
        out_shape=jax.ShapeDtypeStruct(xs.shape, F32),
        compiler_params=pltpu.CompilerParams(vmem_limit_bytes=56 * MIB),
        name="sample_post",
    )(xs, mod_s, z_s, y_s, conv_s, gn_g, w_ret_out_b, ln_g, ln_b, w_conv_out_b, w_out_b,
      g_post1, g_pre2, g_post2, wg_b, wu_b, wd_b)


def _rope_tables(pos):
    inv_freq = ROPE_BASE ** (-jnp.arange(0, DK, 2, dtype=F32) / DK)
    ang = pos[:, None] * inv_freq[None, :]
    return jnp.cos(ang), jnp.sin(ang)


def _log_gamma():
    return jnp.log(1.0 - jnp.exp(jnp.linspace(jnp.log(1.0 / 32.0), jnp.log(1.0 / 512.0), N_HEADS)))


def _decay_tables(chunk):
    lg = _log_gamma()
    idx = jnp.arange(chunk, dtype=F32)
    diff = idx[:, None] - idx[None, :]
    decay = jnp.where(diff >= 0, jnp.exp(lg[:, None, None] * jnp.maximum(diff, 0.0)), 0.0)
    cross = jnp.exp(lg[None, :] * (idx[:, None] + 1.0))
    k_dec = jnp.exp(lg[None, :] * (chunk - 1.0 - idx[:, None]))
    full = jnp.broadcast_to(jnp.exp(lg * chunk)[None, :], (chunk, N_HEADS))
    rs = jnp.concatenate([cross, k_dec, full], axis=1)
    rs = jnp.pad(rs, ((0, 0), (0, LANES - rs.shape[1])))
    return decay.astype(F32), rs.astype(F32)


def kernel(x_prompt, x_sample, c_prompt, c_sample, state_ret, state_conv, w_in, w_ada, b_ada, g_pre1, g_post1, g_pre2, g_post2, conv_w, conv_b, conv_ln_g, conv_ln_b, w_conv_out, ret_gn_g, w_ret_out, w_out, w_ffn_gate, w_ffn_up, w_ffn_down):
    depth = w_in.shape[0]
    assert depth == 1, "single-layer step"
    B, L, _ = x_prompt.shape
    nb = x_sample.shape[0]

    w_in_b = w_in[0].astype(BF16)
    w_ret_out_b = w_ret_out[0].astype(BF16)
    w_conv_out_b = w_conv_out[0].astype(BF16)
    w_out_b = w_out[0].astype(BF16)
    wg_b = w_ffn_gate[0].astype(BF16)
    wu_b = w_ffn_up[0].astype(BF16)
    wd_b = w_ffn_down[0].astype(BF16)

    cos_p, sin_p = _rope_tables(jnp.arange(L, dtype=F32))
    cos_s, sin_s = _rope_tables(PAST_LEN + jnp.arange(1, dtype=F32))
    decay, rs_tab = _decay_tables(T_MIX)
    gam_tab = jnp.broadcast_to(jnp.exp(_log_gamma())[:, None], (N_HEADS, DV)).astype(F32)

    c_all = jnp.concatenate([c_prompt, c_sample], axis=0)
    mod = _modulation(c_all, w_ada[0], b_ada)
    mod_p = mod[:B].reshape(B, 1, 6 * D_MODEL)
    mod_s = mod[B:]

    xs = x_sample.reshape(nb, D_MODEL)
    z_s = _sample_in(xs, mod_s, g_pre1, w_in_b)
    q_s = z_s[:, OFF_Q:OFF_Q + RET_QK].reshape(nb, N_HEADS, DK)
    k_s = z_s[:, OFF_K:OFF_K + RET_QK].reshape(nb, N_HEADS, DK)
    v_s = z_s[:, OFF_V:OFF_V + RET_V].reshape(nb, N_HEADS, DV)

    x1_p, ret_p, conv_p = _prompt_mixer(
        x_prompt, mod_p, g_pre1, g_post1, cos_p, sin_p, decay, rs_tab, w_in_b, conv_w[0], conv_b,
        conv_ln_g, conv_ln_b, w_conv_out_b, ret_gn_g, w_ret_out_b, w_out_b)
    y_p, ret_s, yr_s = _prompt_ffn_sample_ret(
        x1_p, mod_p, g_pre2, g_post2, wg_b, wu_b, wd_b, q_s, k_s, v_s, cos_s, sin_s, gam_tab, state_ret[0])

    conv_new_s, conv_s = _sample_conv(z_s, state_conv[0], conv_w[0], conv_b)
    y_s = _sample_post(xs, mod_s, z_s, yr_s.reshape(nb, RET_V), conv_s, ret_gn_g, w_ret_out_b,
                       conv_ln_g, conv_ln_b, w_conv_out_b, w_out_b, g_post1, g_pre2, g_post2,
                       wg_b, wu_b, wd_b)

    return (y_p, y_s.reshape(x_sample.shape), ret_p[None], ret_s[None], conv_p[None], conv_new_s[None])
```

```python
import numpy as np

import jax
import jax.numpy as jnp
from jax import lax
from jax.experimental import pallas as pl
from jax.experimental.pallas import tpu as pltpu

F32 = jnp.float32
BF16 = jnp.bfloat16

D_MODEL = 1024
N_HEADS = 4
DK = 256
DV = 512
HALF = DK // 2
RET_QK = N_HEADS * DK
RET_V = N_HEADS * DV
CONV_WIDTH = 31
CONV_STATE = CONV_WIDTH - 1
EPS = 1e-6
ROPE_BASE = 10000.0
PAST_LEN = 16384

OFF_Q = 0
OFF_K = RET_QK
OFF_V = 2 * RET_QK
OFF_G = OFF_V + RET_V
OFF_U = OFF_G + RET_V
OFF_A = OFF_U + D_MODEL
OFF_GR = OFF_A + D_MODEL
OFF_GC = OFF_GR + D_MODEL
D_IN = OFF_GC + D_MODEL

LANES = 128
T_MIX = 256
HIST = 32
ROW_CHUNK = 64
PROJ_W = 256
T_FFN = 256
CONV_B = 16
PACK_TK = 256
MIB = 1024 * 1024


def _rms(x, g):
    ms = jnp.mean(x * x, axis=-1, keepdims=True)
    return x * lax.rsqrt(ms + EPS) * g


def _rotary(t, cos, sin):
    t1, t2 = t[:, :HALF], t[:, HALF:]
    return jnp.concatenate([t1 * cos - t2 * sin, t2 * cos + t1 * sin], axis=-1)


def _standardize(y):
    mu = jnp.mean(y, axis=-1, keepdims=True)
    yc = y - mu
    var = jnp.mean(yc * yc, axis=-1, keepdims=True)
    return yc * lax.rsqrt(var + EPS)


def _dot(a, b):
    return jnp.dot(a, b, preferred_element_type=F32)


def _w(packed):
    return pltpu.bitcast(packed, BF16)


def _const_spec(shape):
    return pl.BlockSpec(shape, lambda *_: (0,) * len(shape), pipeline_mode=pl.Buffered(1))


def _mod_kernel(c_ref, w_ref, b_ref, o_ref):
    s = jax.nn.silu(c_ref[...]).astype(BF16)
    o_ref[...] = _dot(s, w_ref[...].astype(BF16)) + b_ref[...]


def _modulation(c_all, w_ada, b_ada):
    rows = c_all.shape[0]
    n_out = w_ada.shape[1]
    tn = D_MODEL
    return pl.pallas_call(
        _mod_kernel,
        grid=(n_out // tn,),
        in_specs=[
            pl.BlockSpec((rows, D_MODEL), lambda j: (0, 0)),
            pl.BlockSpec((D_MODEL, tn), lambda j: (0, j)),
            pl.BlockSpec((1, tn), lambda j: (0, j)),
        ],
        out_specs=pl.BlockSpec((rows, tn), lambda j: (0, j)),
        out_shape=jax.ShapeDtypeStruct((rows, n_out), F32),
        compiler_params=pltpu.CompilerParams(dimension_semantics=("arbitrary",)),
        name="adaln_mod",
    )(c_all, w_ada, b_ada)


def _mixer_kernel(x_ref, mod_ref, gpre_ref, gpost_ref, cos_ref, sin_ref, decay_ref, rs_ref,
                  w_in_ref, convw_ref, convb_ref, lng_ref, lnb_ref, wco_ref, gng_ref, wro_ref, wo_ref,
                  x1_ref, rstate_ref, cstate_ref,
                  full_ref, conv_ref, gated_ref, hb_ref, qk_ref, v_ref, g_ref, gates_ref):
    t = pl.program_id(1)
    T = T_MIX
    n_col = D_MODEL // LANES
    mod = mod_ref[0]

    def proj(lo, width):
        return _dot(hb_ref[...], _w(w_in_ref[:, lo:lo + width]))

    sh1 = mod[:, 0:D_MODEL]
    sc1 = mod[:, D_MODEL:2 * D_MODEL]
    hb_ref[...] = (_rms(x_ref[0], gpre_ref[...]) * (1.0 + sc1) + sh1).astype(BF16)
    glu = proj(OFF_U, D_MODEL) * jax.nn.sigmoid(proj(OFF_A, D_MODEL))
    for c in range(n_col):
        full_ref[c, HIST:HIST + T, :] = glu[:, c * LANES:(c + 1) * LANES]

    @pl.when(t == 0)
    def _():
        full_ref[:, 0:HIST, :] = jnp.zeros((n_col, HIST, LANES), F32)

    @pl.when(t == pl.num_programs(1) - 1)
    def _():
        for c in range(n_col):
            cstate_ref[0, :, c * LANES:(c + 1) * LANES] = full_ref[c, HIST + T - CONV_STATE:HIST + T, :]

    def conv_block(c):
        lanes = slice(c * LANES, (c + 1) * LANES)
        for r0 in range(0, T, ROW_CHUNK):
            acc = jnp.broadcast_to(convb_ref[:, lanes], (ROW_CHUNK, LANES))
            for j in range(CONV_WIDTH):
                start = HIST - CONV_STATE + j + r0
                acc = acc + full_ref[c, start:start + ROW_CHUNK, :] * convw_ref[j:j + 1, lanes]
            conv_ref[c, r0:r0 + ROW_CHUNK, :] = acc

    for i in range(n_col):
        lo = i * PROJ_W
        for dst, off, cast in ((qk_ref, OFF_Q, F32), (v_ref, OFF_V, BF16), (g_ref, OFF_G, F32),
                               (gates_ref, OFF_GR, F32)):
            dst[:, lo:lo + PROJ_W] = proj(off + lo, PROJ_W).astype(cast)
        conv_block(i)
    full_ref[:, 0:HIST, :] = full_ref[:, T:T + HIST, :]

    @pl.when(t == 0)
    def _():
        rstate_ref[...] = jnp.zeros_like(rstate_ref)

    for h in range(N_HEADS):
        q = _rotary(qk_ref[:, h * DK:(h + 1) * DK], cos_ref[...], sin_ref[...]) * (DK ** -0.5)
        k = _rotary(qk_ref[:, RET_QK + h * DK:RET_QK + (h + 1) * DK], cos_ref[...], sin_ref[...])
        vb = v_ref[:, h * DV:(h + 1) * DV]
        qb = q.astype(BF16)
        scores = lax.dot_general(qb, k.astype(BF16), (((1,), (1,)), ((), ())),
                                 preferred_element_type=F32) * decay_ref[h]
        s_prev = rstate_ref[0, h]
        y = _dot(scores.astype(BF16), vb)
        y = y + _dot(qb, s_prev.astype(BF16)) * rs_ref[:, h:h + 1]
        k_dec = (k * rs_ref[:, N_HEADS + h:N_HEADS + h + 1]).astype(BF16)
        upd = lax.dot_general(k_dec, vb, (((0,), (0,)), ((), ())), preferred_element_type=F32)
        rstate_ref[0, h] = rs_ref[0:1, 2 * N_HEADS + h:2 * N_HEADS + h + 1] * s_prev + upd
        yn = _standardize(y) * gng_ref[:, h * DV:(h + 1) * DV]
        gated_ref[:, h * DV:(h + 1) * DV] = (jax.nn.silu(g_ref[:, h * DV:(h + 1) * DV]) * yn).astype(BF16)

    conv = jnp.concatenate([conv_ref[c] for c in range(n_col)], axis=-1)
    ln = _standardize(conv) * lng_ref[...] + lnb_ref[...]
    conv_out = _dot(jax.nn.silu(ln).astype(BF16), _w(wco_ref[...]))
    ret_out = _dot(gated_ref[...], _w(wro_ref[...]))
    merged = (jax.nn.sigmoid(gates_ref[:, 0:D_MODEL]) * ret_out
              + jax.nn.sigmoid(gates_ref[:, D_MODEL:2 * D_MODEL]) * conv_out)
    mix = _dot(merged.astype(BF16), _w(wo_ref[...]))
    gt1 = mod[:, 2 * D_MODEL:3 * D_MODEL]
    x1_ref[0] = x_ref[0] + gt1 * _rms(mix, gpost_ref[...])


def _prompt_mixer(x, mod_p, g_pre1, g_post1, cos_p, sin_p, decay, rs_tab, w_in_b, conv_w, conv_b,
                  ln_g, ln_b, w_conv_out_b, gn_g, w_ret_out_b, w_out_b):
    B, L, _ = x.shape
    T = T_MIX
    row = lambda b, t: (0, 0)
    in_specs = [
        pl.BlockSpec((1, T, D_MODEL), lambda b, t: (b, t, 0)),
        pl.BlockSpec((1, 1, 6 * D_MODEL), lambda b, t: (b, 0, 0)),
        pl.BlockSpec((1, D_MODEL), row),
        pl.BlockSpec((1, D_MODEL), row),
        pl.BlockSpec((T, HALF), lambda b, t: (t, 0)),
        pl.BlockSpec((T, HALF), lambda b, t: (t, 0)),
        _const_spec((N_HEADS, T, T)),
        _const_spec((T, LANES)),
        _const_spec((D_MODEL // 2, D_IN)),
        pl.BlockSpec((CONV_WIDTH, D_MODEL), row),
        pl.BlockSpec((1, D_MODEL), row),
        pl.BlockSpec((1, D_MODEL), row),
        pl.BlockSpec((1, D_MODEL), row),
        _const_spec((D_MODEL // 2, D_MODEL)),
        pl.BlockSpec((1, RET_V), row),
        _const_spec((RET_V // 2, D_MODEL)),
        _const_spec((D_MODEL // 2, D_MODEL)),
    ]
    out_specs = [
        pl.BlockSpec((1, T, D_MODEL), lambda b, t: (b, t, 0)),
        pl.BlockSpec((1, N_HEADS, DK, DV), lambda b, t: (b, 0, 0, 0)),
        pl.BlockSpec((1, CONV_STATE, D_MODEL), lambda b, t: (b, 0, 0)),
    ]
    out_shape = [
        jax.ShapeDtypeStruct((B, L, D_MODEL), F32),
        jax.ShapeDtypeStruct((B, N_HEADS, DK, DV), F32),
        jax.ShapeDtypeStruct((B, CONV_STATE, D_MODEL), F32),
    ]
    return pl.pallas_call(
        _mixer_kernel,
        grid=(B, L // T),
        in_specs=in_specs,
        out_specs=out_specs,
        out_shape=out_shape,
        scratch_shapes=[
            pltpu.VMEM((D_MODEL // LANES, HIST + T, LANES), F32),
            pltpu.VMEM((D_MODEL // LANES, T, LANES), F32),
            pltpu.VMEM((T, RET_V), BF16),
            pltpu.VMEM((T, D_MODEL), BF16),
            pltpu.VMEM((T, 2 * RET_QK), F32),
            pltpu.VMEM((T, RET_V), BF16),
            pltpu.VMEM((T, RET_V), F32),
            pltpu.VMEM((T, 2 * D_MODEL), F32),
        ],
        compiler_params=pltpu.CompilerParams(
            dimension_semantics=("arbitrary", "arbitrary"),
            vmem_limit_bytes=60 * MIB),
        name="prompt_mixer",
    )(x, mod_p, g_pre1, g_post1, cos_p, sin_p, decay, rs_tab, w_in_b, conv_w, conv_b,
      ln_g, ln_b, w_conv_out_b, gn_g, w_ret_out_b, w_out_b)


def _ffn_body(x, sh2, sc2, gt2, gpre, gpost, wg_ref, wu_ref, wd_ref):
    hb = (_rms(x, gpre) * (1.0 + sc2) + sh2).astype(BF16)
    act = (jax.nn.silu(_dot(hb, _w(wg_ref[...]))) * _dot(hb, _w(wu_ref[...]))).astype(BF16)
    return x + gt2 * _rms(_dot(act, _w(wd_ref[...])), gpost)


def _sample_ret_step(q, k, v, cos, sin, gam_ref, s0_ref, snew_ref, y_ref, i):
    q = _rotary(q, cos, sin) * (DK ** -0.5)
    k = _rotary(k, cos, sin)
    qk = jnp.sum(q * k, axis=-1, keepdims=True)
    cols = jnp.concatenate([k, q, jnp.zeros((LANES - 2 * N_HEADS, DK), F32)], axis=0).T
    for h in range(N_HEADS):
        s0 = s0_ref[i, h]
        v_h = v[h:h + 1, :]
        gam = gam_ref[h:h + 1, :]
        snew_ref[i, h] = gam * s0 + cols[:, h:h + 1] * v_h
        qs = jnp.sum(cols[:, N_HEADS + h:N_HEADS + h + 1] * s0, axis=0, keepdims=True)
        y_ref[i, h:h + 1, :] = qk[h:h + 1, :] * v_h + gam * qs


def _ffn_ret_kernel(x_ref, mod_ref, gpre_ref, gpost_ref, wg_ref, wu_ref, wd_ref,
                    q_ref, k_ref, v_ref, cos_ref, sin_ref, gam_ref, s0_ref,
                    o_ref, snew_ref, y_ref):
    mod = mod_ref[0]
    o_ref[0] = _ffn_body(x_ref[0], mod[:, 3 * D_MODEL:4 * D_MODEL], mod[:, 4 * D_MODEL:5 * D_MODEL],
                         mod[:, 5 * D_MODEL:6 * D_MODEL], gpre_ref[...], gpost_ref[...],
                         wg_ref, wu_ref, wd_ref)
    for i in range(q_ref.shape[0]):
        _sample_ret_step(q_ref[i], k_ref[i], v_ref[i], cos_ref[...], sin_ref[...], gam_ref,
                         s0_ref, snew_ref, y_ref, i)


def _prompt_ffn_sample_ret(x1, mod_p, g_pre2, g_post2, wg_b, wu_b, wd_b,
                           q_s, k_s, v_s, cos_s, sin_s, gam_tab, state):
    B, L, _ = x1.shape
    nb = q_s.shape[0]
    d_ff = wg_b.shape[1]
    nt = L // T_FFN
    steps = B * nt
    sb = nb // steps
    assert sb * steps == nb
    row = lambda b, t: (0, 0)
    blk = lambda b, t: (b * nt + t, 0, 0)
    return pl.pallas_call(
        _ffn_ret_kernel,
        grid=(B, nt),
        in_specs=[
            pl.BlockSpec((1, T_FFN, D_MODEL), lambda b, t: (b, t, 0)),
            pl.BlockSpec((1, 1, 6 * D_MODEL), lambda b, t: (b, 0, 0)),
            pl.BlockSpec((1, D_MODEL), row),
            pl.BlockSpec((1, D_MODEL), row),
            _const_spec((D_MODEL // 2, d_ff)),
            _const_spec((D_MODEL // 2, d_ff)),
            _const_spec((d_ff // 2, D_MODEL)),
            pl.BlockSpec((sb, N_HEADS, DK), blk),
            pl.BlockSpec((sb, N_HEADS, DK), blk),
            pl.BlockSpec((sb, N_HEADS, DV), blk),
            pl.BlockSpec((1, HALF), row),
            pl.BlockSpec((1, HALF), row),
            pl.BlockSpec((N_HEADS, DV), row),
            pl.BlockSpec((sb, N_HEADS, DK, DV), lambda b, t: (b * nt + t, 0, 0, 0)),
        ],
        out_specs=[
            pl.BlockSpec((1, T_FFN, D_MODEL), lambda b, t: (b, t, 0)),
            pl.BlockSpec((sb, N_HEADS, DK, DV), lambda b, t: (b * nt + t, 0, 0, 0)),
            pl.BlockSpec((sb, N_HEADS, DV), blk),
        ],
        out_shape=[
            jax.ShapeDtypeStruct((B, L, D_MODEL), F32),
            jax.ShapeDtypeStruct((nb, N_HEADS, DK, DV), F32),
            jax.ShapeDtypeStruct((nb, N_HEADS, DV), F32),
        ],
        compiler_params=pltpu.CompilerParams(
            dimension_semantics=("arbitrary", "arbitrary"),
            vmem_limit_bytes=56 * MIB),
        name="prompt_ffn_sample_ret",
    )(x1, mod_p, g_pre2, g_post2, wg_b, wu_b, wd_b, q_s, k_s, v_s, cos_s, sin_s, gam_tab, state)


def _sample_in_kernel(x_ref, sh_ref, sc_ref, gpre_ref, w_ref, z_ref, wp_ref):
    hb = (_rms(x_ref[...], gpre_ref[...]) * (1.0 + sc_ref[...]) + sh_ref[...]).astype(BF16)
    wb = w_ref[...].astype(BF16)
    z_ref[...] = _dot(hb, wb)
    wp_ref[...] = pltpu.bitcast(wb, jnp.uint32)


def _sample_in(xs, mod, g_pre1, w_in):
    rows = xs.shape[0]
    tn = D_MODEL
    return pl.pallas_call(
        _sample_in_kernel,
        grid=(D_IN // tn,),
        in_specs=[
            pl.BlockSpec((rows, D_MODEL), lambda j: (0, 0)),
            pl.BlockSpec((rows, D_MODEL), lambda j: (0, 0)),
            pl.BlockSpec((rows, D_MODEL), lambda j: (0, 1)),
            pl.BlockSpec((1, D_MODEL), lambda j: (0, 0)),
            pl.BlockSpec((D_MODEL, tn), lambda j: (0, j)),
        ],
        out_specs=[
            pl.BlockSpec((rows, tn), lambda j: (0, j)),
            pl.BlockSpec((D_MODEL // 2, tn), lambda j: (0, j)),
        ],
        out_shape=[
            jax.ShapeDtypeStruct((rows, D_IN), F32),
            jax.ShapeDtypeStruct((D_MODEL // 2, D_IN), jnp.uint32),
        ],
        compiler_params=pltpu.CompilerParams(dimension_semantics=("arbitrary",)),
        name="sample_in_proj",
    )(xs, mod, mod, g_pre1, w_in)


def _pack_kernel(w_ref, o_ref):
    o_ref[...] = pltpu.bitcast(w_ref[...].astype(BF16), jnp.uint32)


def _pack_weight(w):
    k, n = w.shape
    tk = PACK_TK
    assert k % tk == 0
    return pl.pallas_call(
        _pack_kernel,
        grid=(k // tk,),
        in_specs=[pl.BlockSpec((tk, n), lambda i: (i, 0))],
        out_specs=pl.BlockSpec((tk // 2, n), lambda i: (i, 0)),
        out_shape=jax.ShapeDtypeStruct((k // 2, n), jnp.uint32),
        compiler_params=pltpu.CompilerParams(dimension_semantics=("arbitrary",)),
        name="pack_weight",
    )(w)


def _sample_conv_kernel(u_ref, a_ref, st_ref, w_ref, b_ref, new_ref, conv_ref):
    glu = u_ref[...] * jax.nn.sigmoid(a_ref[...])
    st = st_ref[0]
    w = w_ref[...]
    conv = jnp.sum(st * w[None, 0:CONV_STATE, :], axis=1)
    conv_ref[...] = conv + glu * w[CONV_STATE:CONV_WIDTH, :] + b_ref[...]
    new_ref[0, :, 0:CONV_STATE - 1, :] = st[:, 1:CONV_STATE, :]
    new_ref[0, :, CONV_STATE - 1:CONV_STATE, :] = glu[:, None, :]


def _sample_conv(z_s, state_conv, conv_w, conv_b):
    nb = z_s.shape[0]
    return pl.pallas_call(
        _sample_conv_kernel,
        grid=(nb // CONV_B,),
        in_specs=[
            pl.BlockSpec((CONV_B, D_MODEL), lambda i: (i, OFF_U // D_MODEL)),
            pl.BlockSpec((CONV_B, D_MODEL), lambda i: (i, OFF_A // D_MODEL)),
            pl.BlockSpec((1, CONV_B, CONV_STATE, D_MODEL), lambda i: (0, i, 0, 0)),
            pl.BlockSpec((CONV_WIDTH, D_MODEL), lambda i: (0, 0)),
            pl.BlockSpec((1, D_MODEL), lambda i: (0, 0)),
        ],
        out_specs=[
            pl.BlockSpec((1, CONV_B, CONV_STATE, D_MODEL), lambda i: (0, i, 0, 0)),
            pl.BlockSpec((CONV_B, D_MODEL), lambda i: (i, 0)),
        ],
        out_shape=[
            jax.ShapeDtypeStruct((1, nb, CONV_STATE, D_MODEL), F32),
            jax.ShapeDtypeStruct((nb, D_MODEL), F32),
        ],
        compiler_params=pltpu.CompilerParams(dimension_semantics=("arbitrary",)),
        name="sample_conv",
    )(z_s, z_s, state_conv, conv_w, conv_b)


def _sample_post_kernel(x_ref, mod_ref, z_ref, y_ref, conv_ref, gng_ref, wro_ref, lng_ref, lnb_ref,
                        wco_ref, wo_ref, gpost1_ref, gpre2_ref, gpost2_ref, wg_ref, wu_ref, wd_ref, o_ref):
    x = x_ref[...]
    rows = x.shape[0]

    def mod(i):
        return mod_ref[0:rows, i * D_MODEL:(i + 1) * D_MODEL]

    ret_out = jnp.zeros((rows, D_MODEL), F32)
    for h in range(N_HEADS):
        sl = slice(h * DV, (h + 1) * DV)
        yn = _standardize(y_ref[:, sl]) * gng_ref[:, sl]
        gated = (jax.nn.silu(z_ref[:, OFF_G + h * DV:OFF_G + (h + 1) * DV]) * yn).astype(BF16)
        ret_out = ret_out + _dot(gated, _w(wro_ref[h * DV // 2:(h + 1) * DV // 2, :]))
    ln = _standardize(conv_ref[...]) * lng_ref[...] + lnb_ref[...]
    conv_out = _dot(jax.nn.silu(ln).astype(BF16), _w(wco_ref[...]))
    merged = (jax.nn.sigmoid(z_ref[:, OFF_GR:OFF_GR + D_MODEL]) * ret_out
              + jax.nn.sigmoid(z_ref[:, OFF_GC:OFF_GC + D_MODEL]) * conv_out)
    mix = _dot(merged.astype(BF16), _w(wo_ref[...]))
    x1 = x + mod(2) * _rms(mix, gpost1_ref[...])
    o_ref[...] = _ffn_body(x1, mod(3), mod(4), mod(5), gpre2_ref[...], gpost2_ref[...],
                           wg_ref, wu_ref, wd_ref)


def _sample_post(xs, mod_s, z_s, y_s, conv_s, gn_g, w_ret_out_b, ln_g, ln_b, w_conv_out_b, w_out_b,
                 g_post1, g_pre2, g_post2, wg_b, wu_b, wd_b):
    return pl.pallas_call(
        _sample_post_kernel,
        out_shape=jax.ShapeDtypeStruct(xs.shape, F32),
        compiler_params=pltpu.CompilerParams(vmem_limit_bytes=56 * MIB),
        name="sample_post",
    )(xs, mod_s, z_s, y_s, conv_s, gn_g, w_ret_out_b, ln_g, ln_b, w_conv_out_b, w_out_b,
      g_post1, g_pre2, g_post2, wg_b, wu_b, wd_b)


def _rope_tables(pos):
    inv_freq = ROPE_BASE ** (-np.arange(0, DK, 2, dtype=np.float64) / DK)
    ang = np.asarray(pos, np.float64)[:, None] * inv_freq[None, :]
    return jnp.asarray(np.cos(ang), F32), jnp.asarray(np.sin(ang), F32)


def _log_gamma():
    return np.log(1.0 - np.exp(np.linspace(np.log(1.0 / 32.0), np.log(1.0 / 512.0), N_HEADS)))


def _decay_tables(chunk):
    lg = _log_gamma()
    idx = np.arange(chunk, dtype=np.float64)
    diff = idx[:, None] - idx[None, :]
    decay = np.where(diff >= 0, np.exp(lg[:, None, None] * np.maximum(diff, 0.0)), 0.0)
    cross = np.exp(lg[None, :] * (idx[:, None] + 1.0))
    k_dec = np.exp(lg[None, :] * (chunk - 1.0 - idx[:, None]))
    full = np.broadcast_to(np.exp(lg * chunk)[None, :], (chunk, N_HEADS))
    rs = np.concatenate([cross, k_dec, full], axis=1)
    rs = np.pad(rs, ((0, 0), (0, LANES - rs.shape[1])))
    return jnp.asarray(decay, F32), jnp.asarray(rs, F32)


def kernel(x_prompt, x_sample, c_prompt, c_sample, state_ret, state_conv, w_in, w_ada, b_ada, g_pre1, g_post1, g_pre2, g_post2, conv_w, conv_b, conv_ln_g, conv_ln_b, w_conv_out, ret_gn_g, w_ret_out, w_out, w_ffn_gate, w_ffn_up, w_ffn_down):
    depth = w_in.shape[0]
    assert depth == 1, "single-layer step"
    B, L, _ = x_prompt.shape
    nb = x_sample.shape[0]

    w_ret_out_b = _pack_weight(w_ret_out[0])
    w_conv_out_b = _pack_weight(w_conv_out[0])
    w_out_b = _pack_weight(w_out[0])
    wg_b = _pack_weight(w_ffn_gate[0])
    wu_b = _pack_weight(w_ffn_up[0])
    wd_b = _pack_weight(w_ffn_down[0])

    cos_p, sin_p = _rope_tables(np.arange(L))
    cos_s, sin_s = _rope_tables(PAST_LEN + np.arange(x_sample.shape[1]))
    decay, rs_tab = _decay_tables(T_MIX)
    gam_tab = jnp.asarray(np.broadcast_to(np.exp(_log_gamma())[:, None], (N_HEADS, DV)), F32)

    c_all = jnp.concatenate([c_sample, c_prompt], axis=0)
    mod = _modulation(c_all, w_ada[0], b_ada)
    mod_p = mod[nb:].reshape(B, 1, 6 * D_MODEL)

    xs = x_sample.reshape(nb, D_MODEL)
    z_s, w_in_b = _sample_in(xs, mod, g_pre1, w_in[0])
    q_s = z_s[:, OFF_Q:OFF_Q + RET_QK].reshape(nb, N_HEADS, DK)
    k_s = z_s[:, OFF_K:OFF_K + RET_QK].reshape(nb, N_HEADS, DK)
    v_s = z_s[:, OFF_V:OFF_V + RET_V].reshape(nb, N_HEADS, DV)

    x1_p, ret_p, conv_p = _prompt_mixer(
        x_prompt, mod_p, g_pre1, g_post1, cos_p, sin_p, decay, rs_tab, w_in_b, conv_w[0], conv_b,
        conv_ln_g, conv_ln_b, w_conv_out_b, ret_gn_g, w_ret_out_b, w_out_b)
    y_p, ret_s, yr_s = _prompt_ffn_sample_ret(
        x1_p, mod_p, g_pre2, g_post2, wg_b, wu_b, wd_b, q_s, k_s, v_s, cos_s, sin_s, gam_tab, state_ret[0])

    conv_new_s, conv_s = _sample_conv(z_s, state_conv, conv_w[0], conv_b)
    y_s = _sample_post(xs, mod, z_s, yr_s.reshape(nb, RET_V), conv_s, ret_gn_g, w_ret_out_b,
                       conv_ln_g, conv_ln_b, w_conv_out_b, w_out_b, g_post1, g_pre2, g_post2,
                       wg_b, wu_b, wd_b)

    return (y_p, y_s.reshape(x_sample.shape), ret_p[None], ret_s[None], conv_p[None], conv_new_s)
```

```python
import numpy as np

import jax
import jax.numpy as jnp
from jax import lax
from jax.experimental import pallas as pl
from jax.experimental.pallas import tpu as pltpu

F32 = jnp.float32
BF16 = jnp.bfloat16

D_MODEL = 1024
N_HEADS = 4
DK = 256
DV = 512
HALF = DK // 2
RET_QK = N_HEADS * DK
RET_V = N_HEADS * DV
CONV_WIDTH = 31
CONV_STATE = CONV_WIDTH - 1
EPS = 1e-6
ROPE_BASE = 10000.0
PAST_LEN = 16384

OFF_Q = 0
OFF_K = RET_QK
OFF_V = 2 * RET_QK
OFF_G = OFF_V + RET_V
OFF_U = OFF_G + RET_V
OFF_A = OFF_U + D_MODEL
OFF_GR = OFF_A + D_MODEL
OFF_GC = OFF_GR + D_MODEL
D_IN = OFF_GC + D_MODEL

LANES = 128
T_MIX = 256
HIST = 32
ROW_CHUNK = 64
SECTION_W = 2048
assert SECTION_W == 2 * RET_QK == RET_V == 2 * D_MODEL
T_FFN = 256
CONV_B = 16
PACK_TK = 256
MIB = 1024 * 1024


def _rms(x, g):
    ms = jnp.mean(x * x, axis=-1, keepdims=True)
    return x * lax.rsqrt(ms + EPS) * g


def _rotary(t, cos, sin):
    t1, t2 = t[:, :HALF], t[:, HALF:]
    return jnp.concatenate([t1 * cos - t2 * sin, t2 * cos + t1 * sin], axis=-1)


def _standardize(y):
    mu = jnp.mean(y, axis=-1, keepdims=True)
    yc = y - mu
    var = jnp.mean(yc * yc, axis=-1, keepdims=True)
    return yc * lax.rsqrt(var + EPS)


def _dot(a, b):
    return jnp.dot(a, b, preferred_element_type=F32)


def _w(packed):
    return pltpu.bitcast(packed, BF16)


def _const_spec(shape):
    return pl.BlockSpec(shape, lambda *_: (0,) * len(shape), pipeline_mode=pl.Buffered(1))


def _mod_kernel(c_ref, w_ref, b_ref, o_ref):
    s = jax.nn.silu(c_ref[...]).astype(BF16)
    o_ref[...] = _dot(s, w_ref[...].astype(BF16)) + b_ref[...]


def _modulation(c_all, w_ada, b_ada):
    rows = c_all.shape[0]
    n_out = w_ada.shape[1]
    tn = D_MODEL
    return pl.pallas_call(
        _mod_kernel,
        grid=(n_out // tn,),
        in_specs=[
            pl.BlockSpec((rows, D_MODEL), lambda j: (0, 0)),
            pl.BlockSpec((D_MODEL, tn), lambda j: (0, j)),
            pl.BlockSpec((1, tn), lambda j: (0, j)),
        ],
        out_specs=pl.BlockSpec((rows, tn), lambda j: (0, j)),
        out_shape=jax.ShapeDtypeStruct((rows, n_out), F32),
        compiler_params=pltpu.CompilerParams(dimension_semantics=("arbitrary",)),
        name="adaln_mod",
    )(c_all, w_ada, b_ada)


def _mixer_kernel(x_ref, mod_ref, gpre_ref, gpost_ref, cos_ref, sin_ref, decay_ref, rs_ref,
                  w_in_ref, convw_ref, convb_ref, lng_ref, lnb_ref, wco_ref, gng_ref, wro_ref, wo_ref,
                  x1_ref, rstate_ref, cstate_ref,
                  full_ref, conv_ref, gated_ref, hb_ref, qk_ref, v_ref, g_ref, gates_ref):
    t = pl.program_id(1)
    T = T_MIX
    n_col = D_MODEL // LANES
    mod = mod_ref[0]

    def proj(lo, width):
        return _dot(hb_ref[...], _w(w_in_ref[:, lo:lo + width]))

    sh1 = mod[:, 0:D_MODEL]
    sc1 = mod[:, D_MODEL:2 * D_MODEL]
    hb_ref[...] = (_rms(x_ref[0], gpre_ref[...]) * (1.0 + sc1) + sh1).astype(BF16)
    glu = proj(OFF_U, D_MODEL) * jax.nn.sigmoid(proj(OFF_A, D_MODEL))
    for c in range(n_col):
        full_ref[c, HIST:HIST + T, :] = glu[:, c * LANES:(c + 1) * LANES]

    @pl.when(t == 0)
    def _():
        full_ref[:, 0:HIST, :] = jnp.zeros((n_col, HIST, LANES), F32)

    @pl.when(t == pl.num_programs(1) - 1)
    def _():
        for c in range(n_col):
            cstate_ref[0, :, c * LANES:(c + 1) * LANES] = full_ref[c, HIST + T - CONV_STATE:HIST + T, :]

    def conv_block(c):
        lanes = slice(c * LANES, (c + 1) * LANES)
        for r0 in range(0, T, ROW_CHUNK):
            acc = jnp.broadcast_to(convb_ref[:, lanes], (ROW_CHUNK, LANES))
            for j in range(CONV_WIDTH):
                start = HIST - CONV_STATE + j + r0
                acc = acc + full_ref[c, start:start + ROW_CHUNK, :] * convw_ref[j:j + 1, lanes]
            conv_ref[c, r0:r0 + ROW_CHUNK, :] = acc

    sections = ((qk_ref, OFF_Q, F32), (v_ref, OFF_V, BF16), (g_ref, OFF_G, F32), (gates_ref, OFF_GR, F32))
    blocks_per_section = n_col // len(sections)
    for s, (dst, off, cast) in enumerate(sections):
        for c in range(blocks_per_section * s, blocks_per_section * (s + 1)):
            conv_block(c)
        dst[...] = proj(off, SECTION_W).astype(cast)
    full_ref[:, 0:HIST, :] = full_ref[:, T:T + HIST, :]

    @pl.when(t == 0)
    def _():
        rstate_ref[...] = jnp.zeros_like(rstate_ref)

    for h in range(N_HEADS):
        q = _rotary(qk_ref[:, h * DK:(h + 1) * DK], cos_ref[...], sin_ref[...]) * (DK ** -0.5)
        k = _rotary(qk_ref[:, RET_QK + h * DK:RET_QK + (h + 1) * DK], cos_ref[...], sin_ref[...])
        vb = v_ref[:, h * DV:(h + 1) * DV]
        qb = q.astype(BF16)
        scores = lax.dot_general(qb, k.astype(BF16), (((1,), (1,)), ((), ())),
                                 preferred_element_type=F32) * decay_ref[h]
        s_prev = rstate_ref[0, h]
        y = _dot(scores.astype(BF16), vb)
        y = y + _dot(qb, s_prev.astype(BF16)) * rs_ref[:, h:h + 1]
        k_dec = (k * rs_ref[:, N_HEADS + h:N_HEADS + h + 1]).astype(BF16)
        upd = lax.dot_general(k_dec, vb, (((0,), (0,)), ((), ())), preferred_element_type=F32)
        rstate_ref[0, h] = rs_ref[0:1, 2 * N_HEADS + h:2 * N_HEADS + h + 1] * s_prev + upd
        yn = _standardize(y) * gng_ref[:, h * DV:(h + 1) * DV]
        gated_ref[:, h * DV:(h + 1) * DV] = (jax.nn.silu(g_ref[:, h * DV:(h + 1) * DV]) * yn).astype(BF16)

    conv = jnp.concatenate([conv_ref[c] for c in range(n_col)], axis=-1)
    ln = _standardize(conv) * lng_ref[...] + lnb_ref[...]
    conv_out = _dot(jax.nn.silu(ln).astype(BF16), _w(wco_ref[...]))
    ret_out = _dot(gated_ref[...], _w(wro_ref[...]))
    merged = (jax.nn.sigmoid(gates_ref[:, 0:D_MODEL]) * ret_out
              + jax.nn.sigmoid(gates_ref[:, D_MODEL:2 * D_MODEL]) * conv_out)
    mix = _dot(merged.astype(BF16), _w(wo_ref[...]))
    gt1 = mod[:, 2 * D_MODEL:3 * D_MODEL]
    x1_ref[0] = x_ref[0] + gt1 * _rms(mix, gpost_ref[...])


def _prompt_mixer(x, mod_p, g_pre1, g_post1, cos_p, sin_p, decay, rs_tab, w_in_b, conv_w, conv_b,
                  ln_g, ln_b, w_conv_out_b, gn_g, w_ret_out_b, w_out_b):
    B, L, _ = x.shape
    T = T_MIX
    row = lambda b, t: (0, 0)
    in_specs = [
        pl.BlockSpec((1, T, D_MODEL), lambda b, t: (b, t, 0)),
        pl.BlockSpec((1, 1, 6 * D_MODEL), lambda b, t: (b, 0, 0)),
        pl.BlockSpec((1, D_MODEL), row),
        pl.BlockSpec((1, D_MODEL), row),
        pl.BlockSpec((T, HALF), lambda b, t: (t, 0)),
        pl.BlockSpec((T, HALF), lambda b, t: (t, 0)),
        _const_spec((N_HEADS, T, T)),
        _const_spec((T, LANES)),
        _const_spec((D_MODEL // 2, D_IN)),
        pl.BlockSpec((CONV_WIDTH, D_MODEL), row),
        pl.BlockSpec((1, D_MODEL), row),
        pl.BlockSpec((1, D_MODEL), row),
        pl.BlockSpec((1, D_MODEL), row),
        _const_spec((D_MODEL // 2, D_MODEL)),
        pl.BlockSpec((1, RET_V), row),
        _const_spec((RET_V // 2, D_MODEL)),
        _const_spec((D_MODEL // 2, D_MODEL)),
    ]
    out_specs = [
        pl.BlockSpec((1, T, D_MODEL), lambda b, t: (b, t, 0)),
        pl.BlockSpec((1, N_HEADS, DK, DV), lambda b, t: (b, 0, 0, 0)),
        pl.BlockSpec((1, CONV_STATE, D_MODEL), lambda b, t: (b, 0, 0)),
    ]
    out_shape = [
        jax.ShapeDtypeStruct((B, L, D_MODEL), F32),
        jax.ShapeDtypeStruct((B, N_HEADS, DK, DV), F32),
        jax.ShapeDtypeStruct((B, CONV_STATE, D_MODEL), F32),
    ]
    return pl.pallas_call(
        _mixer_kernel,
        grid=(B, L // T),
        in_specs=in_specs,
        out_specs=out_specs,
        out_shape=out_shape,
        scratch_shapes=[
            pltpu.VMEM((D_MODEL // LANES, HIST + T, LANES), F32),
            pltpu.VMEM((D_MODEL // LANES, T, LANES), F32),
            pltpu.VMEM((T, RET_V), BF16),
            pltpu.VMEM((T, D_MODEL), BF16),
            pltpu.VMEM((T, 2 * RET_QK), F32),
            pltpu.VMEM((T, RET_V), BF16),
            pltpu.VMEM((T, RET_V), F32),
            pltpu.VMEM((T, 2 * D_MODEL), F32),
        ],
        compiler_params=pltpu.CompilerParams(
            dimension_semantics=("arbitrary", "arbitrary"),
            vmem_limit_bytes=60 * MIB),
        name="prompt_mixer",
    )(x, mod_p, g_pre1, g_post1, cos_p, sin_p, decay, rs_tab, w_in_b, conv_w, conv_b,
      ln_g, ln_b, w_conv_out_b, gn_g, w_ret_out_b, w_out_b)


def _ffn_body(x, sh2, sc2, gt2, gpre, gpost, wg_ref, wu_ref, wd_ref):
    hb = (_rms(x, gpre) * (1.0 + sc2) + sh2).astype(BF16)
    act = (jax.nn.silu(_dot(hb, _w(wg_ref[...]))) * _dot(hb, _w(wu_ref[...]))).astype(BF16)
    return x + gt2 * _rms(_dot(act, _w(wd_ref[...])), gpost)


def _sample_ret_step(q, k, v, cos, sin, gam_ref, s0_ref, snew_ref, y_ref, i):
    q = _rotary(q, cos, sin) * (DK ** -0.5)
    k = _rotary(k, cos, sin)
    qk = jnp.sum(q * k, axis=-1, keepdims=True)
    cols = jnp.concatenate([k, q, jnp.zeros((LANES - 2 * N_HEADS, DK), F32)], axis=0).T
    for h in range(N_HEADS):
        s0 = s0_ref[i, h]
        v_h = v[h:h + 1, :]
        gam = gam_ref[h:h + 1, :]
        snew_ref[i, h] = gam * s0 + cols[:, h:h + 1] * v_h
        qs = jnp.sum(cols[:, N_HEADS + h:N_HEADS + h + 1] * s0, axis=0, keepdims=True)
        y_ref[i, h:h + 1, :] = qk[h:h + 1, :] * v_h + gam * qs


def _ffn_ret_kernel(x_ref, mod_ref, gpre_ref, gpost_ref, wg_ref, wu_ref, wd_ref,
                    q_ref, k_ref, v_ref, cos_ref, sin_ref, gam_ref, s0_ref,
                    o_ref, snew_ref, y_ref):
    mod = mod_ref[0]
    o_ref[0] = _ffn_body(x_ref[0], mod[:, 3 * D_MODEL:4 * D_MODEL], mod[:, 4 * D_MODEL:5 * D_MODEL],
                         mod[:, 5 * D_MODEL:6 * D_MODEL], gpre_ref[...], gpost_ref[...],
                         wg_ref, wu_ref, wd_ref)
    for i in range(q_ref.shape[0]):
        _sample_ret_step(q_ref[i], k_ref[i], v_ref[i], cos_ref[...], sin_ref[...], gam_ref,
                         s0_ref, snew_ref, y_ref, i)


def _prompt_ffn_sample_ret(x1, mod_p, g_pre2, g_post2, wg_b, wu_b, wd_b,
                           q_s, k_s, v_s, cos_s, sin_s, gam_tab, state):
    B, L, _ = x1.shape
    nb = q_s.shape[0]
    d_ff = wg_b.shape[1]
    nt = L // T_FFN
    steps = B * nt
    sb = nb // steps
    assert sb * steps == nb
    row = lambda b, t: (0, 0)
    blk = lambda b, t: (b * nt + t, 0, 0)
    return pl.pallas_call(
        _ffn_ret_kernel,
        grid=(B, nt),
        in_specs=[
            pl.BlockSpec((1, T_FFN, D_MODEL), lambda b, t: (b, t, 0)),
            pl.BlockSpec((1, 1, 6 * D_MODEL), lambda b, t: (b, 0, 0)),
            pl.BlockSpec((1, D_MODEL), row),
            pl.BlockSpec((1, D_MODEL), row),
            _const_spec((D_MODEL // 2, d_ff)),
            _const_spec((D_MODEL // 2, d_ff)),
            _const_spec((d_ff // 2, D_MODEL)),
            pl.BlockSpec((sb, N_HEADS, DK), blk),
            pl.BlockSpec((sb, N_HEADS, DK), blk),
            pl.BlockSpec((sb, N_HEADS, DV), blk),
            pl.BlockSpec((1, HALF), row),
            pl.BlockSpec((1, HALF), row),
            pl.BlockSpec((N_HEADS, DV), row),
            pl.BlockSpec((sb, N_HEADS, DK, DV), lambda b, t: (b * nt + t, 0, 0, 0)),
        ],
        out_specs=[
            pl.BlockSpec((1, T_FFN, D_MODEL), lambda b, t: (b, t, 0)),
            pl.BlockSpec((sb, N_HEADS, DK, DV), lambda b, t: (b * nt + t, 0, 0, 0)),
            pl.BlockSpec((sb, N_HEADS, DV), blk),
        ],
        out_shape=[
            jax.ShapeDtypeStruct((B, L, D_MODEL), F32),
            jax.ShapeDtypeStruct((nb, N_HEADS, DK, DV), F32),
            jax.ShapeDtypeStruct((nb, N_HEADS, DV), F32),
        ],
        compiler_params=pltpu.CompilerParams(
            dimension_semantics=("arbitrary", "arbitrary"),
            vmem_limit_bytes=56 * MIB),
        name="prompt_ffn_sample_ret",
    )(x1, mod_p, g_pre2, g_post2, wg_b, wu_b, wd_b, q_s, k_s, v_s, cos_s, sin_s, gam_tab, state)


def _sample_in_kernel(x_ref, sh_ref, sc_ref, gpre_ref, w_ref, z_ref, wp_ref):
    hb = (_rms(x_ref[...], gpre_ref[...]) * (1.0 + sc_ref[...]) + sh_ref[...]).astype(BF16)
    wb = w_ref[...].astype(BF16)
    z_ref[...] = _dot(hb, wb)
    wp_ref[...] = pltpu.bitcast(wb, jnp.uint32)


def _sample_in(xs, mod, g_pre1, w_in):
    rows = xs.shape[0]
    tn = D_MODEL
    return pl.pallas_call(
        _sample_in_kernel,
        grid=(D_IN // tn,),
        in_specs=[
            pl.BlockSpec((rows, D_MODEL), lambda j: (0, 0)),
            pl.BlockSpec((rows, D_MODEL), lambda j: (0, 0)),
            pl.BlockSpec((rows, D_MODEL), lambda j: (0, 1)),
            pl.BlockSpec((1, D_MODEL), lambda j: (0, 0)),
            pl.BlockSpec((D_MODEL, tn), lambda j: (0, j)),
        ],
        out_specs=[
            pl.BlockSpec((rows, tn), lambda j: (0, j)),
            pl.BlockSpec((D_MODEL // 2, tn), lambda j: (0, j)),
        ],
        out_shape=[
            jax.ShapeDtypeStruct((rows, D_IN), F32),
            jax.ShapeDtypeStruct((D_MODEL // 2, D_IN), jnp.uint32),
        ],
        compiler_params=pltpu.CompilerParams(dimension_semantics=("arbitrary",)),
        name="sample_in_proj",
    )(xs, mod, mod, g_pre1, w_in)


def _pack_kernel(w_ref, o_ref):
    o_ref[...] = pltpu.bitcast(w_ref[...].astype(BF16), jnp.uint32)


def _pack_weight(w):
    k, n = w.shape
    tk = PACK_TK
    assert k % tk == 0
    return pl.pallas_call(
        _pack_kernel,
        grid=(k // tk,),
        in_specs=[pl.BlockSpec((tk, n), lambda i: (i, 0))],
        out_specs=pl.BlockSpec((tk // 2, n), lambda i: (i, 0)),
        out_shape=jax.ShapeDtypeStruct((k // 2, n), jnp.uint32),
        compiler_params=pltpu.CompilerParams(dimension_semantics=("arbitrary",)),
        name="pack_weight",
    )(w)


def _sample_conv_kernel(u_ref, a_ref, st_ref, w_ref, b_ref, new_ref, conv_ref):
    glu = u_ref[...] * jax.nn.sigmoid(a_ref[...])
    acc = glu * w_ref[CONV_STATE:CONV_WIDTH, :] + b_ref[...]
    for j in range(CONV_STATE):
        acc = acc + st_ref[0, j] * w_ref[j:j + 1, :]
    conv_ref[...] = acc
    for j in range(CONV_STATE - 1):
        new_ref[0, j] = st_ref[0, j + 1]
    new_ref[0, CONV_STATE - 1] = glu


def _sample_conv(z_s, state_conv_t, conv_w, conv_b):
    nb = z_s.shape[0]
    return pl.pallas_call(
        _sample_conv_kernel,
        grid=(nb // CONV_B,),
        in_specs=[
            pl.BlockSpec((CONV_B, D_MODEL), lambda i: (i, OFF_U // D_MODEL)),
            pl.BlockSpec((CONV_B, D_MODEL), lambda i: (i, OFF_A // D_MODEL)),
            pl.BlockSpec((1, CONV_STATE, CONV_B, D_MODEL), lambda i: (0, 0, i, 0)),
            pl.BlockSpec((CONV_WIDTH, D_MODEL), lambda i: (0, 0)),
            pl.BlockSpec((1, D_MODEL), lambda i: (0, 0)),
        ],
        out_specs=[
            pl.BlockSpec((1, CONV_STATE, CONV_B, D_MODEL), lambda i: (0, 0, i, 0)),
            pl.BlockSpec((CONV_B, D_MODEL), lambda i: (i, 0)),
        ],
        out_shape=[
            jax.ShapeDtypeStruct((1, CONV_STATE, nb, D_MODEL), F32),
            jax.ShapeDtypeStruct((nb, D_MODEL), F32),
        ],
        compiler_params=pltpu.CompilerParams(dimension_semantics=("arbitrary",)),
        name="sample_conv",
    )(z_s, z_s, state_conv_t, conv_w, conv_b)


def _sample_post_kernel(x_ref, mod_ref, z_ref, y_ref, conv_ref, gng_ref, wro_ref, lng_ref, lnb_ref,
                        wco_ref, wo_ref, gpost1_ref, gpre2_ref, gpost2_ref, wg_ref, wu_ref, wd_ref, o_ref):
    x = x_ref[...]
    rows = x.shape[0]

    def mod(i):
        return mod_ref[0:rows, i * D_MODEL:(i + 1) * D_MODEL]

    ret_out = jnp.zeros((rows, D_MODEL), F32)
    for h in range(N_HEADS):
        sl = slice(h * DV, (h + 1) * DV)
        yn = _standardize(y_ref[:, sl]) * gng_ref[:, sl]
        gated = (jax.nn.silu(z_ref[:, OFF_G + h * DV:OFF_G + (h + 1) * DV]) * yn).astype(BF16)
        ret_out = ret_out + _dot(gated, _w(wro_ref[h * DV // 2:(h + 1) * DV // 2, :]))
    ln = _standardize(conv_ref[...]) * lng_ref[...] + lnb_ref[...]
    conv_out = _dot(jax.nn.silu(ln).astype(BF16), _w(wco_ref[...]))
    merged = (jax.nn.sigmoid(z_ref[:, OFF_GR:OFF_GR + D_MODEL]) * ret_out
              + jax.nn.sigmoid(z_ref[:, OFF_GC:OFF_GC + D_MODEL]) * conv_out)
    mix = _dot(merged.astype(BF16), _w(wo_ref[...]))
    x1 = x + mod(2) * _rms(mix, gpost1_ref[...])
    o_ref[...] = _ffn_body(x1, mod(3), mod(4), mod(5), gpre2_ref[...], gpost2_ref[...],
                           wg_ref, wu_ref, wd_ref)


def _sample_post(xs, mod_s, z_s, y_s, conv_s, gn_g, w_ret_out_b, ln_g, ln_b, w_conv_out_b, w_out_b,
                 g_post1, g_pre2, g_post2, wg_b, wu_b, wd_b):
    return pl.pallas_call(
        _sample_post_kernel,
        out_shape=jax.ShapeDtypeStruct(xs.shape, F32),
        compiler_params=pltpu.CompilerParams(vmem_limit_bytes=56 * MIB),
        name="sample_post",
    )(xs, mod_s, z_s, y_s, conv_s, gn_g, w_ret_out_b, ln_g, ln_b, w_conv_out_b, w_out_b,
      g_post1, g_pre2, g_post2, wg_b, wu_b, wd_b)


def _rope_tables(pos):
    inv_freq = ROPE_BASE ** (-np.arange(0, DK, 2, dtype=np.float64) / DK)
    ang = np.asarray(pos, np.float64)[:, None] * inv_freq[None, :]
    return jnp.asarray(np.cos(ang), F32), jnp.asarray(np.sin(ang), F32)


def _log_gamma():
    return np.log(1.0 - np.exp(np.linspace(np.log(1.0 / 32.0), np.log(1.0 / 512.0), N_HEADS)))


def _decay_tables(chunk):
    lg = _log_gamma()
    idx = np.arange(chunk, dtype=np.float64)
    diff = idx[:, None] - idx[None, :]
    decay = np.where(diff >= 0, np.exp(lg[:, None, None] * np.maximum(diff, 0.0)), 0.0)
    cross = np.exp(lg[None, :] * (idx[:, None] + 1.0))
    k_dec = np.exp(lg[None, :] * (chunk - 1.0 - idx[:, None]))
    full = np.broadcast_to(np.exp(lg * chunk)[None, :], (chunk, N_HEADS))
    rs = np.concatenate([cross, k_dec, full], axis=1)
    rs = np.pad(rs, ((0, 0), (0, LANES - rs.shape[1])))
    return jnp.asarray(decay, F32), jnp.asarray(rs, F32)


def kernel(x_prompt, x_sample, c_prompt, c_sample, state_ret, state_conv, w_in, w_ada, b_ada, g_pre1, g_post1, g_pre2, g_post2, conv_w, conv_b, conv_ln_g, conv_ln_b, w_conv_out, ret_gn_g, w_ret_out, w_out, w_ffn_gate, w_ffn_up, w_ffn_down):
    depth = w_in.shape[0]
    assert depth == 1, "single-layer step"
    B, L, _ = x_prompt.shape
    nb = x_sample.shape[0]

    w_ret_out_b = _pack_weight(w_ret_out[0])
    w_conv_out_b = _pack_weight(w_conv_out[0])
    w_out_b = _pack_weight(w_out[0])
    wg_b = _pack_weight(w_ffn_gate[0])
    wu_b = _pack_weight(w_ffn_up[0])
    wd_b = _pack_weight(w_ffn_down[0])

    cos_p, sin_p = _rope_tables(np.arange(L))
    cos_s, sin_s = _rope_tables(PAST_LEN + np.arange(x_sample.shape[1]))
    decay, rs_tab = _decay_tables(T_MIX)
    gam_tab = jnp.asarray(np.broadcast_to(np.exp(_log_gamma())[:, None], (N_HEADS, DV)), F32)

    c_all = jnp.concatenate([c_sample, c_prompt], axis=0)
    mod = _modulation(c_all, w_ada[0], b_ada)
    mod_p = mod[nb:].reshape(B, 1, 6 * D_MODEL)

    xs = x_sample.reshape(nb, D_MODEL)
    z_s, w_in_b = _sample_in(xs, mod, g_pre1, w_in[0])
    q_s = z_s[:, OFF_Q:OFF_Q + RET_QK].reshape(nb, N_HEADS, DK)
    k_s = z_s[:, OFF_K:OFF_K + RET_QK].reshape(nb, N_HEADS, DK)
    v_s = z_s[:, OFF_V:OFF_V + RET_V].reshape(nb, N_HEADS, DV)

    x1_p, ret_p, conv_p = _prompt_mixer(
        x_prompt, mod_p, g_pre1, g_post1, cos_p, sin_p, decay, rs_tab, w_in_b, conv_w[0], conv_b,
        conv_ln_g, conv_ln_b, w_conv_out_b, ret_gn_g, w_ret_out_b, w_out_b)
    y_p, ret_s, yr_s = _prompt_ffn_sample_ret(
        x1_p, mod_p, g_pre2, g_post2, wg_b, wu_b, wd_b, q_s, k_s, v_s, cos_s, sin_s, gam_tab, state_ret[0])

    conv_new_t, conv_s = _sample_conv(z_s, state_conv.transpose(0, 2, 1, 3), conv_w[0], conv_b)
    conv_new_s = conv_new_t.transpose(0, 2, 1, 3)
    y_s = _sample_post(xs, mod, z_s, yr_s.reshape(nb, RET_V), conv_s, ret_gn_g, w_ret_out_b,
                       conv_ln_g, conv_ln_b, w_conv_out_b, w_out_b, g_post1, g_pre2, g_post2,
                       wg_b, wu_b, wd_b)

    return (y_p, y_s.reshape(x_sample.shape), ret_p[None], ret_s[None], conv_p[None], conv_new_s)
```

```python
import numpy as np

import jax
import jax.numpy as jnp
from jax import lax
from jax.experimental import pallas as pl
from jax.experimental.pallas import tpu as pltpu

F32 = jnp.float32
BF16 = jnp.bfloat16

D_MODEL = 1024
N_HEADS = 4
DK = 256
DV = 512
HALF = DK // 2
RET_QK = N_HEADS * DK
RET_V = N_HEADS * DV
CONV_WIDTH = 31
CONV_STATE = CONV_WIDTH - 1
EPS = 1e-6
ROPE_BASE = 10000.0
PAST_LEN = 16384

OFF_Q = 0
OFF_K = RET_QK
OFF_V = 2 * RET_QK
OFF_G = OFF_V + RET_V
OFF_U = OFF_G + RET_V
OFF_A = OFF_U + D_MODEL
OFF_GR = OFF_A + D_MODEL
OFF_GC = OFF_GR + D_MODEL
D_IN = OFF_GC + D_MODEL

LANES = 128
T_MIX = 256
HIST = 32
ROW_CHUNK = 64
T_FFN = 256
CONV_B = 16
PACK_TK = 256
MIB = 1024 * 1024


def _rms(x, g):
    ms = jnp.mean(x * x, axis=-1, keepdims=True)
    return x * lax.rsqrt(ms + EPS) * g


def _rotary(t, cos, sin):
    t1, t2 = t[:, :HALF], t[:, HALF:]
    return jnp.concatenate([t1 * cos - t2 * sin, t2 * cos + t1 * sin], axis=-1)


def _standardize(y):
    mu = jnp.mean(y, axis=-1, keepdims=True)
    yc = y - mu
    var = jnp.mean(yc * yc, axis=-1, keepdims=True)
    return yc * lax.rsqrt(var + EPS)


def _dot(a, b):
    return jnp.dot(a, b, preferred_element_type=F32)


def _w(packed):
    return pltpu.bitcast(packed, BF16)


def _const_spec(shape):
    return pl.BlockSpec(shape, lambda *_: (0,) * len(shape), pipeline_mode=pl.Buffered(1))


def _mod_kernel(c_ref, w_ref, b_ref, o_ref):
    s = jax.nn.silu(c_ref[...]).astype(BF16)
    o_ref[...] = _dot(s, w_ref[...].astype(BF16)) + b_ref[...]


def _modulation(c_all, w_ada, b_ada):
    rows = c_all.shape[0]
    n_out = w_ada.shape[1]
    tn = D_MODEL
    return pl.pallas_call(
        _mod_kernel,
        grid=(n_out // tn,),
        in_specs=[
            pl.BlockSpec((rows, D_MODEL), lambda j: (0, 0)),
            pl.BlockSpec((D_MODEL, tn), lambda j: (0, j)),
            pl.BlockSpec((1, tn), lambda j: (0, j)),
        ],
        out_specs=pl.BlockSpec((rows, tn), lambda j: (0, j)),
        out_shape=jax.ShapeDtypeStruct((rows, n_out), F32),
        compiler_params=pltpu.CompilerParams(dimension_semantics=("arbitrary",)),
        name="adaln_mod",
    )(c_all, w_ada, b_ada)


def _mixer_kernel(x_ref, mod_ref, gpre_ref, gpost_ref, cos_ref, sin_ref, decay_ref, rs_ref,
                  w_in_ref, convw_ref, convb_ref, lng_ref, lnb_ref, wco_ref, gng_ref, wro_ref, wo_ref,
                  x1_ref, rstate_ref, cstate_ref, full_ref, conv_ref, gated_ref):
    t = pl.program_id(1)
    T = T_MIX
    n_col = D_MODEL // LANES

    @pl.when(t == 0)
    def _():
        rstate_ref[...] = jnp.zeros_like(rstate_ref)
        full_ref[:, 0:HIST, :] = jnp.zeros((n_col, HIST, LANES), F32)

    mod = mod_ref[0]
    sh1 = mod[:, 0:D_MODEL]
    sc1 = mod[:, D_MODEL:2 * D_MODEL]
    gt1 = mod[:, 2 * D_MODEL:3 * D_MODEL]
    hb = (_rms(x_ref[0], gpre_ref[...]) * (1.0 + sc1) + sh1).astype(BF16)

    def proj(lo, width):
        return _dot(hb, _w(w_in_ref[:, lo:lo + width]))

    glu = proj(OFF_U, D_MODEL) * jax.nn.sigmoid(proj(OFF_A, D_MODEL))
    for c in range(n_col):
        full_ref[c, HIST:HIST + T, :] = glu[:, c * LANES:(c + 1) * LANES]

    @pl.when(t == pl.num_programs(1) - 1)
    def _():
        for c in range(n_col):
            cstate_ref[0, :, c * LANES:(c + 1) * LANES] = full_ref[c, HIST + T - CONV_STATE:HIST + T, :]

    def conv_block(c):
        lanes = slice(c * LANES, (c + 1) * LANES)
        for r0 in range(0, T, ROW_CHUNK):
            acc = jnp.broadcast_to(convb_ref[:, lanes], (ROW_CHUNK, LANES))
            for j in range(CONV_WIDTH):
                start = HIST - CONV_STATE + j + r0
                acc = acc + full_ref[c, start:start + ROW_CHUNK, :] * convw_ref[j:j + 1, lanes]
            conv_ref[c, r0:r0 + ROW_CHUNK, :] = acc

    blocks_per_head = n_col // N_HEADS
    for h in range(N_HEADS):
        q = _rotary(proj(OFF_Q + h * DK, DK), cos_ref[...], sin_ref[...]) * (DK ** -0.5)
        k = _rotary(proj(OFF_K + h * DK, DK), cos_ref[...], sin_ref[...])
        vb = proj(OFF_V + h * DV, DV).astype(BF16)
        qb = q.astype(BF16)
        conv_block(blocks_per_head * h)
        scores = lax.dot_general(qb, k.astype(BF16), (((1,), (1,)), ((), ())),
                                 preferred_element_type=F32) * decay_ref[h]
        s_prev = rstate_ref[0, h]
        y = _dot(scores.astype(BF16), vb)
        y = y + _dot(qb, s_prev.astype(BF16)) * rs_ref[:, h:h + 1]
        k_dec = (k * rs_ref[:, N_HEADS + h:N_HEADS + h + 1]).astype(BF16)
        upd = lax.dot_general(k_dec, vb, (((0,), (0,)), ((), ())), preferred_element_type=F32)
        rstate_ref[0, h] = rs_ref[0:1, 2 * N_HEADS + h:2 * N_HEADS + h + 1] * s_prev + upd
        for c in range(blocks_per_head * h + 1, blocks_per_head * (h + 1)):
            conv_block(c)
        yn = _standardize(y) * gng_ref[:, h * DV:(h + 1) * DV]
        gated_ref[:, h * DV:(h + 1) * DV] = (jax.nn.silu(proj(OFF_G + h * DV, DV)) * yn).astype(BF16)

    full_ref[:, 0:HIST, :] = full_ref[:, T:T + HIST, :]
    conv = jnp.concatenate([conv_ref[c] for c in range(n_col)], axis=-1)
    ln = _standardize(conv) * lng_ref[...] + lnb_ref[...]
    conv_out = _dot(jax.nn.silu(ln).astype(BF16), _w(wco_ref[...]))
    ret_out = _dot(gated_ref[...], _w(wro_ref[...]))

    merged = (jax.nn.sigmoid(proj(OFF_GR, D_MODEL)) * ret_out
              + jax.nn.sigmoid(proj(OFF_GC, D_MODEL)) * conv_out)
    mix = _dot(merged.astype(BF16), _w(wo_ref[...]))
    x1_ref[0] = x_ref[0] + gt1 * _rms(mix, gpost_ref[...])


def _prompt_mixer(x, mod_p, g_pre1, g_post1, cos_p, sin_p, decay, rs_tab, w_in_b, conv_w, conv_b,
                  ln_g, ln_b, w_conv_out_b, gn_g, w_ret_out_b, w_out_b):
    B, L, _ = x.shape
    T = T_MIX
    row = lambda b, t: (0, 0)
    in_specs = [
        pl.BlockSpec((1, T, D_MODEL), lambda b, t: (b, t, 0)),
        pl.BlockSpec((1, 1, 6 * D_MODEL), lambda b, t: (b, 0, 0)),
        pl.BlockSpec((1, D_MODEL), row),
        pl.BlockSpec((1, D_MODEL), row),
        pl.BlockSpec((T, HALF), lambda b, t: (t, 0)),
        pl.BlockSpec((T, HALF), lambda b, t: (t, 0)),
        _const_spec((N_HEADS, T, T)),
        _const_spec((T, LANES)),
        _const_spec((D_MODEL // 2, D_IN)),
        pl.BlockSpec((CONV_WIDTH, D_MODEL), row),
        pl.BlockSpec((1, D_MODEL), row),
        pl.BlockSpec((1, D_MODEL), row),
        pl.BlockSpec((1, D_MODEL), row),
        _const_spec((D_MODEL // 2, D_MODEL)),
        pl.BlockSpec((1, RET_V), row),
        _const_spec((RET_V // 2, D_MODEL)),
        _const_spec((D_MODEL // 2, D_MODEL)),
    ]
    out_specs = [
        pl.BlockSpec((1, T, D_MODEL), lambda b, t: (b, t, 0)),
        pl.BlockSpec((1, N_HEADS, DK, DV), lambda b, t: (b, 0, 0, 0)),
        pl.BlockSpec((1, CONV_STATE, D_MODEL), lambda b, t: (b, 0, 0)),
    ]
    out_shape = [
        jax.ShapeDtypeStruct((B, L, D_MODEL), F32),
        jax.ShapeDtypeStruct((B, N_HEADS, DK, DV), F32),
        jax.ShapeDtypeStruct((B, CONV_STATE, D_MODEL), F32),
    ]
    return pl.pallas_call(
        _mixer_kernel,
        grid=(B, L // T),
        in_specs=in_specs,
        out_specs=out_specs,
        out_shape=out_shape,
        scratch_shapes=[
            pltpu.VMEM((D_MODEL // LANES, HIST + T, LANES), F32),
            pltpu.VMEM((D_MODEL // LANES, T, LANES), F32),
            pltpu.VMEM((T, RET_V), BF16),
        ],
        compiler_params=pltpu.CompilerParams(
            dimension_semantics=("arbitrary", "arbitrary"),
            vmem_limit_bytes=56 * MIB),
        name="prompt_mixer",
    )(x, mod_p, g_pre1, g_post1, cos_p, sin_p, decay, rs_tab, w_in_b, conv_w, conv_b,
      ln_g, ln_b, w_conv_out_b, gn_g, w_ret_out_b, w_out_b)


def _ffn_body(x, sh2, sc2, gt2, gpre, gpost, wg_ref, wu_ref, wd_ref):
    hb = (_rms(x, gpre) * (1.0 + sc2) + sh2).astype(BF16)
    act = (jax.nn.silu(_dot(hb, _w(wg_ref[...]))) * _dot(hb, _w(wu_ref[...]))).astype(BF16)
    return x + gt2 * _rms(_dot(act, _w(wd_ref[...])), gpost)


def _sample_ret_step(q, k, v, cos, sin, gam_ref, s0_ref, snew_ref, y_ref, i):
    q = _rotary(q, cos, sin) * (DK ** -0.5)
    k = _rotary(k, cos, sin)
    qk = jnp.sum(q * k, axis=-1, keepdims=True)
    cols = jnp.concatenate([k, q, jnp.zeros((LANES - 2 * N_HEADS, DK), F32)], axis=0).T
    for h in range(N_HEADS):
        s0 = s0_ref[i, h]
        v_h = v[h:h + 1, :]
        gam = gam_ref[h:h + 1, :]
        snew_ref[i, h] = gam * s0 + cols[:, h:h + 1] * v_h
        qs = jnp.sum(cols[:, N_HEADS + h:N_HEADS + h + 1] * s0, axis=0, keepdims=True)
        y_ref[i, h:h + 1, :] = qk[h:h + 1, :] * v_h + gam * qs


def _ffn_ret_kernel(x_ref, mod_ref, gpre_ref, gpost_ref, wg_ref, wu_ref, wd_ref,
                    q_ref, k_ref, v_ref, cos_ref, sin_ref, gam_ref, s0_ref,
                    o_ref, snew_ref, y_ref):
    mod = mod_ref[0]
    o_ref[0] = _ffn_body(x_ref[0], mod[:, 3 * D_MODEL:4 * D_MODEL], mod[:, 4 * D_MODEL:5 * D_MODEL],
                         mod[:, 5 * D_MODEL:6 * D_MODEL], gpre_ref[...], gpost_ref[...],
                         wg_ref, wu_ref, wd_ref)
    for i in range(q_ref.shape[0]):
        _sample_ret_step(q_ref[i], k_ref[i], v_ref[i], cos_ref[...], sin_ref[...], gam_ref,
                         s0_ref, snew_ref, y_ref, i)


def _prompt_ffn_sample_ret(x1, mod_p, g_pre2, g_post2, wg_b, wu_b, wd_b,
                           q_s, k_s, v_s, cos_s, sin_s, gam_tab, state):
    B, L, _ = x1.shape
    nb = q_s.shape[0]
    d_ff = wg_b.shape[1]
    nt = L // T_FFN
    steps = B * nt
    sb = nb // steps
    assert sb * steps == nb
    row = lambda b, t: (0, 0)
    blk = lambda b, t: (b * nt + t, 0, 0)
    return pl.pallas_call(
        _ffn_ret_kernel,
        grid=(B, nt),
        in_specs=[
            pl.BlockSpec((1, T_FFN, D_MODEL), lambda b, t: (b, t, 0)),
            pl.BlockSpec((1, 1, 6 * D_MODEL), lambda b, t: (b, 0, 0)),
            pl.BlockSpec((1, D_MODEL), row),
            pl.BlockSpec((1, D_MODEL), row),
            _const_spec((D_MODEL // 2, d_ff)),
            _const_spec((D_MODEL // 2, d_ff)),
            _const_spec((d_ff // 2, D_MODEL)),
            pl.BlockSpec((sb, N_HEADS, DK), blk),
            pl.BlockSpec((sb, N_HEADS, DK), blk),
            pl.BlockSpec((sb, N_HEADS, DV), blk),
            pl.BlockSpec((1, HALF), row),
            pl.BlockSpec((1, HALF), row),
            pl.BlockSpec((N_HEADS, DV), row),
            pl.BlockSpec((sb, N_HEADS, DK, DV), lambda b, t: (b * nt + t, 0, 0, 0)),
        ],
        out_specs=[
            pl.BlockSpec((1, T_FFN, D_MODEL), lambda b, t: (b, t, 0)),
            pl.BlockSpec((sb, N_HEADS, DK, DV), lambda b, t: (b * nt + t, 0, 0, 0)),
            pl.BlockSpec((sb, N_HEADS, DV), blk),
        ],
        out_shape=[
            jax.ShapeDtypeStruct((B, L, D_MODEL), F32),
            jax.ShapeDtypeStruct((nb, N_HEADS, DK, DV), F32),
            jax.ShapeDtypeStruct((nb, N_HEADS, DV), F32),
        ],
        compiler_params=pltpu.CompilerParams(
            dimension_semantics=("arbitrary", "arbitrary"),
            vmem_limit_bytes=56 * MIB),
        name="prompt_ffn_sample_ret",
    )(x1, mod_p, g_pre2, g_post2, wg_b, wu_b, wd_b, q_s, k_s, v_s, cos_s, sin_s, gam_tab, state)


def _sample_in_kernel(x_ref, sh_ref, sc_ref, gpre_ref, w_ref, z_ref, wp_ref):
    hb = (_rms(x_ref[...], gpre_ref[...]) * (1.0 + sc_ref[...]) + sh_ref[...]).astype(BF16)
    wb = w_ref[...].astype(BF16)
    z_ref[...] = _dot(hb, wb)
    wp_ref[...] = pltpu.bitcast(wb, jnp.uint32)


def _sample_in(xs, mod, g_pre1, w_in):
    rows = xs.shape[0]
    tn = D_MODEL
    return pl.pallas_call(
        _sample_in_kernel,
        grid=(D_IN // tn,),
        in_specs=[
            pl.BlockSpec((rows, D_MODEL), lambda j: (0, 0)),
            pl.BlockSpec((rows, D_MODEL), lambda j: (0, 0)),
            pl.BlockSpec((rows, D_MODEL), lambda j: (0, 1)),
            pl.BlockSpec((1, D_MODEL), lambda j: (0, 0)),
            pl.BlockSpec((D_MODEL, tn), lambda j: (0, j)),
        ],
        out_specs=[
            pl.BlockSpec((rows, tn), lambda j: (0, j)),
            pl.BlockSpec((D_MODEL // 2, tn), lambda j: (0, j)),
        ],
        out_shape=[
            jax.ShapeDtypeStruct((rows, D_IN), F32),
            jax.ShapeDtypeStruct((D_MODEL // 2, D_IN), jnp.uint32),
        ],
        compiler_params=pltpu.CompilerParams(dimension_semantics=("arbitrary",)),
        name="sample_in_proj",
    )(xs, mod, mod, g_pre1, w_in)


def _pack_kernel(w_ref, o_ref):
    o_ref[...] = pltpu.bitcast(w_ref[...].astype(BF16), jnp.uint32)


def _pack_weight(w):
    k, n = w.shape
    tk = PACK_TK
    assert k % tk == 0
    return pl.pallas_call(
        _pack_kernel,
        grid=(k // tk,),
        in_specs=[pl.BlockSpec((tk, n), lambda i: (i, 0))],
        out_specs=pl.BlockSpec((tk // 2, n), lambda i: (i, 0)),
        out_shape=jax.ShapeDtypeStruct((k // 2, n), jnp.uint32),
        compiler_params=pltpu.CompilerParams(dimension_semantics=("arbitrary",)),
        name="pack_weight",
    )(w)


def _sample_conv_kernel(u_ref, a_ref, st_ref, w_ref, b_ref, new_ref, conv_ref):
    glu = u_ref[...] * jax.nn.sigmoid(a_ref[...])
    acc = glu * w_ref[CONV_STATE:CONV_WIDTH, :] + b_ref[...]
    for j in range(CONV_STATE):
        acc = acc + st_ref[0, j] * w_ref[j:j + 1, :]
    conv_ref[...] = acc
    for j in range(CONV_STATE - 1):
        new_ref[0, j] = st_ref[0, j + 1]
    new_ref[0, CONV_STATE - 1] = glu


def _sample_conv(z_s, state_conv_t, conv_w, conv_b):
    nb = z_s.shape[0]
    return pl.pallas_call(
        _sample_conv_kernel,
        grid=(nb // CONV_B,),
        in_specs=[
            pl.BlockSpec((CONV_B, D_MODEL), lambda i: (i, OFF_U // D_MODEL)),
            pl.BlockSpec((CONV_B, D_MODEL), lambda i: (i, OFF_A // D_MODEL)),
            pl.BlockSpec((1, CONV_STATE, CONV_B, D_MODEL), lambda i: (0, 0, i, 0)),
            pl.BlockSpec((CONV_WIDTH, D_MODEL), lambda i: (0, 0)),
            pl.BlockSpec((1, D_MODEL), lambda i: (0, 0)),
        ],
        out_specs=[
            pl.BlockSpec((1, CONV_STATE, CONV_B, D_MODEL), lambda i: (0, 0, i, 0)),
            pl.BlockSpec((CONV_B, D_MODEL), lambda i: (i, 0)),
        ],
        out_shape=[
            jax.ShapeDtypeStruct((1, CONV_STATE, nb, D_MODEL), F32),
            jax.ShapeDtypeStruct((nb, D_MODEL), F32),
        ],
        compiler_params=pltpu.CompilerParams(dimension_semantics=("arbitrary",)),
        name="sample_conv",
    )(z_s, z_s, state_conv_t, conv_w, conv_b)


def _sample_post_kernel(x_ref, mod_ref, z_ref, y_ref, conv_ref, gng_ref, wro_ref, lng_ref, lnb_ref,
                        wco_ref, wo_ref, gpost1_ref, gpre2_ref, gpost2_ref, wg_ref, wu_ref, wd_ref, o_ref):
    x = x_ref[...]
    rows = x.shape[0]

    def mod(i):
        return mod_ref[0:rows, i * D_MODEL:(i + 1) * D_MODEL]

    ret_out = jnp.zeros((rows, D_MODEL), F32)
    for h in range(N_HEADS):
        sl = slice(h * DV, (h + 1) * DV)
        yn = _standardize(y_ref[:, sl]) * gng_ref[:, sl]
        gated = (jax.nn.silu(z_ref[:, OFF_G + h * DV:OFF_G + (h + 1) * DV]) * yn).astype(BF16)
        ret_out = ret_out + _dot(gated, _w(wro_ref[h * DV // 2:(h + 1) * DV // 2, :]))
    ln = _standardize(conv_ref[...]) * lng_ref[...] + lnb_ref[...]
    conv_out = _dot(jax.nn.silu(ln).astype(BF16), _w(wco_ref[...]))
    merged = (jax.nn.sigmoid(z_ref[:, OFF_GR:OFF_GR + D_MODEL]) * ret_out
              + jax.nn.sigmoid(z_ref[:, OFF_GC:OFF_GC + D_MODEL]) * conv_out)
    mix = _dot(merged.astype(BF16), _w(wo_ref[...]))
    x1 = x + mod(2) * _rms(mix, gpost1_ref[...])
    o_ref[...] = _ffn_body(x1, mod(3), mod(4), mod(5), gpre2_ref[...], gpost2_ref[...],
                           wg_ref, wu_ref, wd_ref)


def _sample_post(xs, mod_s, z_s, y_s, conv_s, gn_g, w_ret_out_b, ln_g, ln_b, w_conv_out_b, w_out_b,
                 g_post1, g_pre2, g_post2, wg_b, wu_b, wd_b):
    return pl.pallas_call(
        _sample_post_kernel,
        out_shape=jax.ShapeDtypeStruct(xs.shape, F32),
        compiler_params=pltpu.CompilerParams(vmem_limit_bytes=56 * MIB),
        name="sample_post",
    )(xs, mod_s, z_s, y_s, conv_s, gn_g, w_ret_out_b, ln_g, ln_b, w_conv_out_b, w_out_b,
      g_post1, g_pre2, g_post2, wg_b, wu_b, wd_b)


def _rope_tables(pos):
    inv_freq = ROPE_BASE ** (-np.arange(0, DK, 2, dtype=np.float64) / DK)
    ang = np.asarray(pos, np.float64)[:, None] * inv_freq[None, :]
    return jnp.asarray(np.cos(ang), F32), jnp.asarray(np.sin(ang), F32)


def _log_gamma():
    return np.log(1.0 - np.exp(np.linspace(np.log(1.0 / 32.0), np.log(1.0 / 512.0), N_HEADS)))


def _decay_tables(chunk):
    lg = _log_gamma()
    idx = np.arange(chunk, dtype=np.float64)
    diff = idx[:, None] - idx[None, :]
    decay = np.where(diff >= 0, np.exp(lg[:, None, None] * np.maximum(diff, 0.0)), 0.0)
    cross = np.exp(lg[None, :] * (idx[:, None] + 1.0))
    k_dec = np.exp(lg[None, :] * (chunk - 1.0 - idx[:, None]))
    full = np.broadcast_to(np.exp(lg * chunk)[None, :], (chunk, N_HEADS))
    rs = np.concatenate([cross, k_dec, full], axis=1)
    rs = np.pad(rs, ((0, 0), (0, LANES - rs.shape[1])))
    return jnp.asarray(decay, F32), jnp.asarray(rs, F32)


def kernel(x_prompt, x_sample, c_prompt, c_sample, state_ret, state_conv, w_in, w_ada, b_ada, g_pre1, g_post1, g_pre2, g_post2, conv_w, conv_b, conv_ln_g, conv_ln_b, w_conv_out, ret_gn_g, w_ret_out, w_out, w_ffn_gate, w_ffn_up, w_ffn_down):
    depth = w_in.shape[0]
    assert depth == 1, "single-layer step"
    B, L, _ = x_prompt.shape
    nb = x_sample.shape[0]

    w_ret_out_b = _pack_weight(w_ret_out[0])
    w_conv_out_b = _pack_weight(w_conv_out[0])
    w_out_b = _pack_weight(w_out[0])
    wg_b = _pack_weight(w_ffn_gate[0])
    wu_b = _pack_weight(w_ffn_up[0])
    wd_b = _pack_weight(w_ffn_down[0])

    cos_p, sin_p = _rope_tables(np.arange(L))
    cos_s, sin_s = _rope_tables(PAST_LEN + np.arange(x_sample.shape[1]))
    decay, rs_tab = _decay_tables(T_MIX)
    gam_tab = jnp.asarray(np.broadcast_to(np.exp(_log_gamma())[:, None], (N_HEADS, DV)), F32)

    c_all = jnp.concatenate([c_sample, c_prompt], axis=0)
    mod = _modulation(c_all, w_ada[0], b_ada)
    mod_p = mod[nb:].reshape(B, 1, 6 * D_MODEL)

    xs = x_sample.reshape(nb, D_MODEL)
    z_s, w_in_b = _sample_in(xs, mod, g_pre1, w_in[0])
    q_s = z_s[:, OFF_Q:OFF_Q + RET_QK].reshape(nb, N_HEADS, DK)
    k_s = z_s[:, OFF_K:OFF_K + RET_QK].reshape(nb, N_HEADS, DK)
    v_s = z_s[:, OFF_V:OFF_V + RET_V].reshape(nb, N_HEADS, DV)

    x1_p, ret_p, conv_p = _prompt_mixer(
        x_prompt, mod_p, g_pre1, g_post1, cos_p, sin_p, decay, rs_tab, w_in_b, conv_w[0], conv_b,
        conv_ln_g, conv_ln_b, w_conv_out_b, ret_gn_g, w_ret_out_b, w_out_b)
    y_p, ret_s, yr_s = _prompt_ffn_sample_ret(
        x1_p, mod_p, g_pre2, g_post2, wg_b, wu_b, wd_b, q_s, k_s, v_s, cos_s, sin_s, gam_tab, state_ret[0])

    conv_new_t, conv_s = _sample_conv(z_s, state_conv.transpose(0, 2, 1, 3), conv_w[0], conv_b)
    conv_new_s = conv_new_t.transpose(0, 2, 1, 3)
    y_s = _sample_post(xs, mod, z_s, yr_s.reshape(nb, RET_V), conv_s, ret_gn_g, w_ret_out_b,
                       conv_ln_g, conv_ln_b, w_conv_out_b, w_out_b, g_post1, g_pre2, g_post2,
                       wg_b, wu_b, wd_b)

    return (y_p, y_s.reshape(x_sample.shape), ret_p[None], ret_s[None], conv_p[None], conv_new_s)
```

```python
import numpy as np

import jax
import jax.numpy as jnp
from jax import lax
from jax.experimental import pallas as pl
from jax.experimental.pallas import tpu as pltpu

F32 = jnp.float32
BF16 = jnp.bfloat16

D_MODEL = 1024
N_HEADS = 4
DK = 256
DV = 512
HALF = DK // 2
RET_QK = N_HEADS * DK
RET_V = N_HEADS * DV
CONV_WIDTH = 31
CONV_STATE = CONV_WIDTH - 1
EPS = 1e-6
ROPE_BASE = 10000.0
PAST_LEN = 16384

OFF_Q = 0
OFF_K = RET_QK
OFF_V = 2 * RET_QK
OFF_G = OFF_V + RET_V
OFF_U = OFF_G + RET_V
OFF_A = OFF_U + D_MODEL
OFF_GR = OFF_A + D_MODEL
OFF_GC = OFF_GR + D_MODEL
D_IN = OFF_GC + D_MODEL

LANES = 128
T_MIX = 512
CHUNK = 256
HIST = 32
ROW_CHUNK = 64
T_FFN = 256
CONV_B = 16
PACK_TK = 256
MIB = 1024 * 1024


def _rms(x, g):
    ms = jnp.mean(x * x, axis=-1, keepdims=True)
    return x * lax.rsqrt(ms + EPS) * g


def _rotary(t, cos, sin):
    t1, t2 = t[:, :HALF], t[:, HALF:]
    return jnp.concatenate([t1 * cos - t2 * sin, t2 * cos + t1 * sin], axis=-1)


def _standardize(y):
    mu = jnp.mean(y, axis=-1, keepdims=True)
    yc = y - mu
    var = jnp.mean(yc * yc, axis=-1, keepdims=True)
    return yc * lax.rsqrt(var + EPS)


def _dot(a, b):
    return jnp.dot(a, b, preferred_element_type=F32)


def _w(packed):
    return pltpu.bitcast(packed, BF16)


def _const_spec(shape):
    return pl.BlockSpec(shape, lambda *_: (0,) * len(shape), pipeline_mode=pl.Buffered(1))


def _mod_kernel(c_ref, w_ref, b_ref, o_ref):
    s = jax.nn.silu(c_ref[...]).astype(BF16)
    o_ref[...] = _dot(s, w_ref[...].astype(BF16)) + b_ref[...]


def _modulation(c_all, w_ada, b_ada):
    rows = c_all.shape[0]
    n_out = w_ada.shape[1]
    tn = D_MODEL
    return pl.pallas_call(
        _mod_kernel,
        grid=(n_out // tn,),
        in_specs=[
            pl.BlockSpec((rows, D_MODEL), lambda j: (0, 0)),
            pl.BlockSpec((D_MODEL, tn), lambda j: (0, j)),
            pl.BlockSpec((1, tn), lambda j: (0, j)),
        ],
        out_specs=pl.BlockSpec((rows, tn), lambda j: (0, j)),
        out_shape=jax.ShapeDtypeStruct((rows, n_out), F32),
        compiler_params=pltpu.CompilerParams(dimension_semantics=("arbitrary",)),
        name="adaln_mod",
    )(c_all, w_ada, b_ada)


def _mixer_kernel(x_ref, mod_ref, gpre_ref, gpost_ref, cos_ref, sin_ref, decay_ref, rs_ref,
                  w_in_ref, convw_ref, convb_ref, lng_ref, lnb_ref, wco_ref, gng_ref, wro_ref, wo_ref,
                  x1_ref, rstate_ref, cstate_ref, full_ref, conv_ref, gated_ref):
    t = pl.program_id(1)
    T = T_MIX
    n_col = D_MODEL // LANES

    @pl.when(t == 0)
    def _():
        rstate_ref[...] = jnp.zeros_like(rstate_ref)
        full_ref[:, 0:HIST, :] = jnp.zeros((n_col, HIST, LANES), F32)

    mod = mod_ref[0]
    sh1 = mod[:, 0:D_MODEL]
    sc1 = mod[:, D_MODEL:2 * D_MODEL]
    gt1 = mod[:, 2 * D_MODEL:3 * D_MODEL]
    hb = (_rms(x_ref[0], gpre_ref[...]) * (1.0 + sc1) + sh1).astype(BF16)

    def proj(lo, width):
        return _dot(hb, _w(w_in_ref[:, lo:lo + width]))

    glu = proj(OFF_U, D_MODEL) * jax.nn.sigmoid(proj(OFF_A, D_MODEL))
    for c in range(n_col):
        full_ref[c, HIST:HIST + T, :] = glu[:, c * LANES:(c + 1) * LANES]

    @pl.when(t == pl.num_programs(1) - 1)
    def _():
        for c in range(n_col):
            cstate_ref[0, :, c * LANES:(c + 1) * LANES] = full_ref[c, HIST + T - CONV_STATE:HIST + T, :]

    def conv_block(c):
        lanes = slice(c * LANES, (c + 1) * LANES)
        for r0 in range(0, T, ROW_CHUNK):
            acc = jnp.broadcast_to(convb_ref[:, lanes], (ROW_CHUNK, LANES))
            for j in range(CONV_WIDTH):
                start = HIST - CONV_STATE + j + r0
                acc = acc + full_ref[c, start:start + ROW_CHUNK, :] * convw_ref[j:j + 1, lanes]
            conv_ref[c, r0:r0 + ROW_CHUNK, :] = acc

    blocks_per_head = n_col // N_HEADS
    for h in range(N_HEADS):
        zq = proj(OFF_Q + h * DK, DK)
        zk = proj(OFF_K + h * DK, DK)
        zv = proj(OFF_V + h * DV, DV).astype(BF16)
        zg = proj(OFF_G + h * DV, DV)
        conv_block(blocks_per_head * h)
        for ci in range(T // CHUNK):
            rows = slice(ci * CHUNK, (ci + 1) * CHUNK)
            q = _rotary(zq[rows], cos_ref[rows, :], sin_ref[rows, :]) * (DK ** -0.5)
            k = _rotary(zk[rows], cos_ref[rows, :], sin_ref[rows, :])
            vb = zv[rows]
            qb = q.astype(BF16)
            scores = lax.dot_general(qb, k.astype(BF16), (((1,), (1,)), ((), ())),
                                     preferred_element_type=F32) * decay_ref[h]
            s_prev = rstate_ref[0, h]
            y = _dot(scores.astype(BF16), vb)
            y = y + _dot(qb, s_prev.astype(BF16)) * rs_ref[:, h:h + 1]
            k_dec = (k * rs_ref[:, N_HEADS + h:N_HEADS + h + 1]).astype(BF16)
            upd = lax.dot_general(k_dec, vb, (((0,), (0,)), ((), ())), preferred_element_type=F32)
            rstate_ref[0, h] = rs_ref[0:1, 2 * N_HEADS + h:2 * N_HEADS + h + 1] * s_prev + upd
            yn = _standardize(y) * gng_ref[:, h * DV:(h + 1) * DV]
            gated_ref[rows, h * DV:(h + 1) * DV] = (jax.nn.silu(zg[rows]) * yn).astype(BF16)
        for c in range(blocks_per_head * h + 1, blocks_per_head * (h + 1)):
            conv_block(c)

    full_ref[:, 0:HIST, :] = full_ref[:, T:T + HIST, :]
    conv = jnp.concatenate([conv_ref[c] for c in range(n_col)], axis=-1)
    ln = _standardize(conv) * lng_ref[...] + lnb_ref[...]
    conv_out = _dot(jax.nn.silu(ln).astype(BF16), _w(wco_ref[...]))
    ret_out = _dot(gated_ref[...], _w(wro_ref[...]))

    merged = (jax.nn.sigmoid(proj(OFF_GR, D_MODEL)) * ret_out
              + jax.nn.sigmoid(proj(OFF_GC, D_MODEL)) * conv_out)
    mix = _dot(merged.astype(BF16), _w(wo_ref[...]))
    x1_ref[0] = x_ref[0] + gt1 * _rms(mix, gpost_ref[...])


def _prompt_mixer(x, mod_p, g_pre1, g_post1, cos_p, sin_p, decay, rs_tab, w_in_b, conv_w, conv_b,
                  ln_g, ln_b, w_conv_out_b, gn_g, w_ret_out_b, w_out_b):
    B, L, _ = x.shape
    T = T_MIX
    row = lambda b, t: (0, 0)
    in_specs = [
        pl.BlockSpec((1, T, D_MODEL), lambda b, t: (b, t, 0)),
        pl.BlockSpec((1, 1, 6 * D_MODEL), lambda b, t: (b, 0, 0)),
        pl.BlockSpec((1, D_MODEL), row),
        pl.BlockSpec((1, D_MODEL), row),
        pl.BlockSpec((T, HALF), lambda b, t: (t, 0)),
        pl.BlockSpec((T, HALF), lambda b, t: (t, 0)),
        _const_spec((N_HEADS, CHUNK, CHUNK)),
        _const_spec((CHUNK, LANES)),
        _const_spec((D_MODEL // 2, D_IN)),
        pl.BlockSpec((CONV_WIDTH, D_MODEL), row),
        pl.BlockSpec((1, D_MODEL), row),
        pl.BlockSpec((1, D_MODEL), row),
        pl.BlockSpec((1, D_MODEL), row),
        _const_spec((D_MODEL // 2, D_MODEL)),
        pl.BlockSpec((1, RET_V), row),
        _const_spec((RET_V // 2, D_MODEL)),
        _const_spec((D_MODEL // 2, D_MODEL)),
    ]
    out_specs = [
        pl.BlockSpec((1, T, D_MODEL), lambda b, t: (b, t, 0)),
        pl.BlockSpec((1, N_HEADS, DK, DV), lambda b, t: (b, 0, 0, 0)),
        pl.BlockSpec((1, CONV_STATE, D_MODEL), lambda b, t: (b, 0, 0)),
    ]
    out_shape = [
        jax.ShapeDtypeStruct((B, L, D_MODEL), F32),
        jax.ShapeDtypeStruct((B, N_HEADS, DK, DV), F32),
        jax.ShapeDtypeStruct((B, CONV_STATE, D_MODEL), F32),
    ]
    return pl.pallas_call(
        _mixer_kernel,
        grid=(B, L // T),
        in_specs=in_specs,
        out_specs=out_specs,
        out_shape=out_shape,
        scratch_shapes=[
            pltpu.VMEM((D_MODEL // LANES, HIST + T, LANES), F32),
            pltpu.VMEM((D_MODEL // LANES, T, LANES), F32),
            pltpu.VMEM((T, RET_V), BF16),
        ],
        compiler_params=pltpu.CompilerParams(
            dimension_semantics=("arbitrary", "arbitrary"),
            vmem_limit_bytes=62 * MIB),
        name="prompt_mixer",
    )(x, mod_p, g_pre1, g_post1, cos_p, sin_p, decay, rs_tab, w_in_b, conv_w, conv_b,
      ln_g, ln_b, w_conv_out_b, gn_g, w_ret_out_b, w_out_b)


def _ffn_body(x, sh2, sc2, gt2, gpre, gpost, wg_ref, wu_ref, wd_ref):
    hb = (_rms(x, gpre) * (1.0 + sc2) + sh2).astype(BF16)
    act = (jax.nn.silu(_dot(hb, _w(wg_ref[...]))) * _dot(hb, _w(wu_ref[...]))).astype(BF16)
    return x + gt2 * _rms(_dot(act, _w(wd_ref[...])), gpost)


def _sample_ret_step(q, k, v, cos, sin, gam_ref, s0_ref, snew_ref, y_ref, i):
    q = _rotary(q, cos, sin) * (DK ** -0.5)
    k = _rotary(k, cos, sin)
    qk = jnp.sum(q * k, axis=-1, keepdims=True)
    cols = jnp.concatenate([k, q, jnp.zeros((LANES - 2 * N_HEADS, DK), F32)], axis=0).T
    for h in range(N_HEADS):
        s0 = s0_ref[i, h]
        v_h = v[h:h + 1, :]
        gam = gam_ref[h:h + 1, :]
        snew_ref[i, h] = gam * s0 + cols[:, h:h + 1] * v_h
        qs = jnp.sum(cols[:, N_HEADS + h:N_HEADS + h + 1] * s0, axis=0, keepdims=True)
        y_ref[i, h:h + 1, :] = qk[h:h + 1, :] * v_h + gam * qs


def _ffn_ret_kernel(x_ref, mod_ref, gpre_ref, gpost_ref, wg_ref, wu_ref, wd_ref,
                    q_ref, k_ref, v_ref, cos_ref, sin_ref, gam_ref, s0_ref,
                    o_ref, snew_ref, y_ref):
    mod = mod_ref[0]
    o_ref[0] = _ffn_body(x_ref[0], mod[:, 3 * D_MODEL:4 * D_MODEL], mod[:, 4 * D_MODEL:5 * D_MODEL],
                         mod[:, 5 * D_MODEL:6 * D_MODEL], gpre_ref[...], gpost_ref[...],
                         wg_ref, wu_ref, wd_ref)
    for i in range(q_ref.shape[0]):
        _sample_ret_step(q_ref[i], k_ref[i], v_ref[i], cos_ref[...], sin_ref[...], gam_ref,
                         s0_ref, snew_ref, y_ref, i)


def _prompt_ffn_sample_ret(x1, mod_p, g_pre2, g_post2, wg_b, wu_b, wd_b,
                           q_s, k_s, v_s, cos_s, sin_s, gam_tab, state):
    B, L, _ = x1.shape
    nb = q_s.shape[0]
    d_ff = wg_b.shape[1]
    nt = L // T_FFN
    steps = B * nt
    sb = nb // steps
    assert sb * steps == nb
    row = lambda b, t: (0, 0)
    blk = lambda b, t: (b * nt + t, 0, 0)
    return pl.pallas_call(
        _ffn_ret_kernel,
        grid=(B, nt),
        in_specs=[
            pl.BlockSpec((1, T_FFN, D_MODEL), lambda b, t: (b, t, 0)),
            pl.BlockSpec((1, 1, 6 * D_MODEL), lambda b, t: (b, 0, 0)),
            pl.BlockSpec((1, D_MODEL), row),
            pl.BlockSpec((1, D_MODEL), row),
            _const_spec((D_MODEL // 2, d_ff)),
            _const_spec((D_MODEL // 2, d_ff)),
            _const_spec((d_ff // 2, D_MODEL)),
            pl.BlockSpec((sb, N_HEADS, DK), blk),
            pl.BlockSpec((sb, N_HEADS, DK), blk),
            pl.BlockSpec((sb, N_HEADS, DV), blk),
            pl.BlockSpec((1, HALF), row),
            pl.BlockSpec((1, HALF), row),
            pl.BlockSpec((N_HEADS, DV), row),
            pl.BlockSpec((sb, N_HEADS, DK, DV), lambda b, t: (b * nt + t, 0, 0, 0)),
        ],
        out_specs=[
            pl.BlockSpec((1, T_FFN, D_MODEL), lambda b, t: (b, t, 0)),
            pl.BlockSpec((sb, N_HEADS, DK, DV), lambda b, t: (b * nt + t, 0, 0, 0)),
            pl.BlockSpec((sb, N_HEADS, DV), blk),
        ],
        out_shape=[
            jax.ShapeDtypeStruct((B, L, D_MODEL), F32),
            jax.ShapeDtypeStruct((nb, N_HEADS, DK, DV), F32),
            jax.ShapeDtypeStruct((nb, N_HEADS, DV), F32),
        ],
        compiler_params=pltpu.CompilerParams(
            dimension_semantics=("arbitrary", "arbitrary"),
            vmem_limit_bytes=56 * MIB),
        name="prompt_ffn_sample_ret",
    )(x1, mod_p, g_pre2, g_post2, wg_b, wu_b, wd_b, q_s, k_s, v_s, cos_s, sin_s, gam_tab, state)


def _sample_in_kernel(x_ref, sh_ref, sc_ref, gpre_ref, w_ref, z_ref, wp_ref):
    hb = (_rms(x_ref[...], gpre_ref[...]) * (1.0 + sc_ref[...]) + sh_ref[...]).astype(BF16)
    wb = w_ref[...].astype(BF16)
    z_ref[...] = _dot(hb, wb)
    wp_ref[...] = pltpu.bitcast(wb, jnp.uint32)


def _sample_in(xs, mod, g_pre1, w_in):
    rows = xs.shape[0]
    tn = D_MODEL
    return pl.pallas_call(
        _sample_in_kernel,
        grid=(D_IN // tn,),
        in_specs=[
            pl.BlockSpec((rows, D_MODEL), lambda j: (0, 0)),
            pl.BlockSpec((rows, D_MODEL), lambda j: (0, 0)),
            pl.BlockSpec((rows, D_MODEL), lambda j: (0, 1)),
            pl.BlockSpec((1, D_MODEL), lambda j: (0, 0)),
            pl.BlockSpec((D_MODEL, tn), lambda j: (0, j)),
        ],
        out_specs=[
            pl.BlockSpec((rows, tn), lambda j: (0, j)),
            pl.BlockSpec((D_MODEL // 2, tn), lambda j: (0, j)),
        ],
        out_shape=[
            jax.ShapeDtypeStruct((rows, D_IN), F32),
            jax.ShapeDtypeStruct((D_MODEL // 2, D_IN), jnp.uint32),
        ],
        compiler_params=pltpu.CompilerParams(dimension_semantics=("arbitrary",)),
        name="sample_in_proj",
    )(xs, mod, mod, g_pre1, w_in)


def _pack_kernel(w_ref, o_ref):
    o_ref[...] = pltpu.bitcast(w_ref[...].astype(BF16), jnp.uint32)


def _pack_weight(w):
    k, n = w.shape
    tk = PACK_TK
    assert k % tk == 0
    return pl.pallas_call(
        _pack_kernel,
        grid=(k // tk,),
        in_specs=[pl.BlockSpec((tk, n), lambda i: (i, 0))],
        out_specs=pl.BlockSpec((tk // 2, n), lambda i: (i, 0)),
        out_shape=jax.ShapeDtypeStruct((k // 2, n), jnp.uint32),
        compiler_params=pltpu.CompilerParams(dimension_semantics=("arbitrary",)),
        name="pack_weight",
    )(w)


def _sample_conv_kernel(u_ref, a_ref, st_ref, w_ref, b_ref, new_ref, conv_ref):
    glu = u_ref[...] * jax.nn.sigmoid(a_ref[...])
    acc = glu * w_ref[CONV_STATE:CONV_WIDTH, :] + b_ref[...]
    for j in range(CONV_STATE):
        acc = acc + st_ref[0, j] * w_ref[j:j + 1, :]
    conv_ref[...] = acc
    for j in range(CONV_STATE - 1):
        new_ref[0, j] = st_ref[0, j + 1]
    new_ref[0, CONV_STATE - 1] = glu


def _sample_conv(z_s, state_conv_t, conv_w, conv_b):
    nb = z_s.shape[0]
    return pl.pallas_call(
        _sample_conv_kernel,
        grid=(nb // CONV_B,),
        in_specs=[
            pl.BlockSpec((CONV_B, D_MODEL), lambda i: (i, OFF_U // D_MODEL)),
            pl.BlockSpec((CONV_B, D_MODEL), lambda i: (i, OFF_A // D_MODEL)),
            pl.BlockSpec((1, CONV_STATE, CONV_B, D_MODEL), lambda i: (0, 0, i, 0)),
            pl.BlockSpec((CONV_WIDTH, D_MODEL), lambda i: (0, 0)),
            pl.BlockSpec((1, D_MODEL), lambda i: (0, 0)),
        ],
        out_specs=[
            pl.BlockSpec((1, CONV_STATE, CONV_B, D_MODEL), lambda i: (0, 0, i, 0)),
            pl.BlockSpec((CONV_B, D_MODEL), lambda i: (i, 0)),
        ],
        out_shape=[
            jax.ShapeDtypeStruct((1, CONV_STATE, nb, D_MODEL), F32),
            jax.ShapeDtypeStruct((nb, D_MODEL), F32),
        ],
        compiler_params=pltpu.CompilerParams(dimension_semantics=("arbitrary",)),
        name="sample_conv",
    )(z_s, z_s, state_conv_t, conv_w, conv_b)


def _sample_post_kernel(x_ref, mod_ref, z_ref, y_ref, conv_ref, gng_ref, wro_ref, lng_ref, lnb_ref,
                        wco_ref, wo_ref, gpost1_ref, gpre2_ref, gpost2_ref, wg_ref, wu_ref, wd_ref, o_ref):
    x = x_ref[...]
    rows = x.shape[0]

    def mod(i):
        return mod_ref[0:rows, i * D_MODEL:(i + 1) * D_MODEL]

    ret_out = jnp.zeros((rows, D_MODEL), F32)
    for h in range(N_HEADS):
        sl = slice(h * DV, (h + 1) * DV)
        yn = _standardize(y_ref[:, sl]) * gng_ref[:, sl]
        gated = (jax.nn.silu(z_ref[:, OFF_G + h * DV:OFF_G + (h + 1) * DV]) * yn).astype(BF16)
        ret_out = ret_out + _dot(gated, _w(wro_ref[h * DV // 2:(h + 1) * DV // 2, :]))
    ln = _standardize(conv_ref[...]) * lng_ref[...] + lnb_ref[...]
    conv_out = _dot(jax.nn.silu(ln).astype(BF16), _w(wco_ref[...]))
    merged = (jax.nn.sigmoid(z_ref[:, OFF_GR:OFF_GR + D_MODEL]) * ret_out
              + jax.nn.sigmoid(z_ref[:, OFF_GC:OFF_GC + D_MODEL]) * conv_out)
    mix = _dot(merged.astype(BF16), _w(wo_ref[...]))
    x1 = x + mod(2) * _rms(mix, gpost1_ref[...])
    o_ref[...] = _ffn_body(x1, mod(3), mod(4), mod(5), gpre2_ref[...], gpost2_ref[...],
                           wg_ref, wu_ref, wd_ref)


def _sample_post(xs, mod_s, z_s, y_s, conv_s, gn_g, w_ret_out_b, ln_g, ln_b, w_conv_out_b, w_out_b,
                 g_post1, g_pre2, g_post2, wg_b, wu_b, wd_b):
    return pl.pallas_call(
        _sample_post_kernel,
        out_shape=jax.ShapeDtypeStruct(xs.shape, F32),
        compiler_params=pltpu.CompilerParams(vmem_limit_bytes=56 * MIB),
        name="sample_post",
    )(xs, mod_s, z_s, y_s, conv_s, gn_g, w_ret_out_b, ln_g, ln_b, w_conv_out_b, w_out_b,
      g_post1, g_pre2, g_post2, wg_b, wu_b, wd_b)


def _rope_tables(pos):
    inv_freq = ROPE_BASE ** (-np.arange(0, DK, 2, dtype=np.float64) / DK)
    ang = np.asarray(pos, np.float64)[:, None] * inv_freq[None, :]
    return jnp.asarray(np.cos(ang), F32), jnp.asarray(np.sin(ang), F32)


def _log_gamma():
    return np.log(1.0 - np.exp(np.linspace(np.log(1.0 / 32.0), np.log(1.0 / 512.0), N_HEADS)))


def _decay_tables(chunk):
    lg = _log_gamma()
    idx = np.arange(chunk, dtype=np.float64)
    diff = idx[:, None] - idx[None, :]
    decay = np.where(diff >= 0, np.exp(lg[:, None, None] * np.maximum(diff, 0.0)), 0.0)
    cross = np.exp(lg[None, :] * (idx[:, None] + 1.0))
    k_dec = np.exp(lg[None, :] * (chunk - 1.0 - idx[:, None]))
    full = np.broadcast_to(np.exp(lg * chunk)[None, :], (chunk, N_HEADS))
    rs = np.concatenate([cross, k_dec, full], axis=1)
    rs = np.pad(rs, ((0, 0), (0, LANES - rs.shape[1])))
    return jnp.asarray(decay, F32), jnp.asarray(rs, F32)


def kernel(x_prompt, x_sample, c_prompt, c_sample, state_ret, state_conv, w_in, w_ada, b_ada, g_pre1, g_post1, g_pre2, g_post2, conv_w, conv_b, conv_ln_g, conv_ln_b, w_conv_out, ret_gn_g, w_ret_out, w_out, w_ffn_gate, w_ffn_up, w_ffn_down):
    depth = w_in.shape[0]
    assert depth == 1, "single-layer step"
    B, L, _ = x_prompt.shape
    nb = x_sample.shape[0]

    w_ret_out_b = _pack_weight(w_ret_out[0])
    w_conv_out_b = _pack_weight(w_conv_out[0])
    w_out_b = _pack_weight(w_out[0])
    wg_b = _pack_weight(w_ffn_gate[0])
    wu_b = _pack_weight(w_ffn_up[0])
    wd_b = _pack_weight(w_ffn_down[0])

    cos_p, sin_p = _rope_tables(np.arange(L))
    cos_s, sin_s = _rope_tables(PAST_LEN + np.arange(x_sample.shape[1]))
    decay, rs_tab = _decay_tables(CHUNK)
    gam_tab = jnp.asarray(np.broadcast_to(np.exp(_log_gamma())[:, None], (N_HEADS, DV)), F32)

    c_all = jnp.concatenate([c_sample, c_prompt], axis=0)
    mod = _modulation(c_all, w_ada[0], b_ada)
    mod_p = mod[nb:].reshape(B, 1, 6 * D_MODEL)

    xs = x_sample.reshape(nb, D_MODEL)
    z_s, w_in_b = _sample_in(xs, mod, g_pre1, w_in[0])
    q_s = z_s[:, OFF_Q:OFF_Q + RET_QK].reshape(nb, N_HEADS, DK)
    k_s = z_s[:, OFF_K:OFF_K + RET_QK].reshape(nb, N_HEADS, DK)
    v_s = z_s[:, OFF_V:OFF_V + RET_V].reshape(nb, N_HEADS, DV)

    x1_p, ret_p, conv_p = _prompt_mixer(
        x_prompt, mod_p, g_pre1, g_post1, cos_p, sin_p, decay, rs_tab, w_in_b, conv_w[0], conv_b,
        conv_ln_g, conv_ln_b, w_conv_out_b, ret_gn_g, w_ret_out_b, w_out_b)
    y_p, ret_s, yr_s = _prompt_ffn_sample_ret(
        x1_p, mod_p, g_pre2, g_post2, wg_b, wu_b, wd_b, q_s, k_s, v_s, cos_s, sin_s, gam_tab, state_ret[0])

    conv_new_t, conv_s = _sample_conv(z_s, state_conv.transpose(0, 2, 1, 3), conv_w[0], conv_b)
    conv_new_s = conv_new_t.transpose(0, 2, 1, 3)
    y_s = _sample_post(xs, mod, z_s, yr_s.reshape(nb, RET_V), conv_s, ret_gn_g, w_ret_out_b,
                       conv_ln_g, conv_ln_b, w_conv_out_b, w_out_b, g_post1, g_pre2, g_post2,
                       wg_b, wu_b, wd_b)

    return (y_p, y_s.reshape(x_sample.shape), ret_p[None], ret_s[None], conv_p[None], conv_new_s)
```

```python
import numpy as np

import jax
import jax.numpy as jnp
from jax import lax
from jax.experimental import pallas as pl
from jax.experimental.pallas import tpu as pltpu

F32 = jnp.float32
BF16 = jnp.bfloat16

D_MODEL = 1024
N_HEADS = 4
DK = 256
DV = 512
HALF = DK // 2
RET_QK = N_HEADS * DK
RET_V = N_HEADS * DV
CONV_WIDTH = 31
CONV_STATE = CONV_WIDTH - 1
EPS = 1e-6
ROPE_BASE = 10000.0
PAST_LEN = 16384

OFF_Q = 0
OFF_K = RET_QK
OFF_V = 2 * RET_QK
OFF_G = OFF_V + RET_V
OFF_U = OFF_G + RET_V
OFF_A = OFF_U + D_MODEL
OFF_GR = OFF_A + D_MODEL
OFF_GC = OFF_GR + D_MODEL
D_IN = OFF_GC + D_MODEL

LANES = 128
BF16_TILE_ROWS = 16
T_MIX = 256
CHUNK = 256
HIST = 32
ROW_CHUNK = 64
T_FFN = 256
CONV_B = 16
MIB = 1024 * 1024


def _rms(x, g):
    ms = jnp.mean(x * x, axis=-1, keepdims=True)
    return x * lax.rsqrt(ms + EPS) * g


def _rotary(t, cos, sin):
    t1, t2 = t[:, :HALF], t[:, HALF:]
    return jnp.concatenate([t1 * cos - t2 * sin, t2 * cos + t1 * sin], axis=-1)


def _standardize(y):
    mu = jnp.mean(y, axis=-1, keepdims=True)
    yc = y - mu
    var = jnp.mean(yc * yc, axis=-1, keepdims=True)
    return yc * lax.rsqrt(var + EPS)


def _dot(a, b):
    return jnp.dot(a, b, preferred_element_type=F32)


def _w(packed):
    return pltpu.bitcast(packed, BF16)


def _const_spec(shape):
    return pl.BlockSpec(shape, lambda *_: (0,) * len(shape), pipeline_mode=pl.Buffered(1))


def _mod_kernel(c_ref, w_ref, b_ref, o_ref):
    s = jax.nn.silu(c_ref[...]).astype(BF16)
    o_ref[...] = _dot(s, w_ref[...].astype(BF16)) + b_ref[...]


def _modulation(c_all, w_ada, b_ada):
    rows = c_all.shape[0]
    n_out = w_ada.shape[1]
    tn = D_MODEL
    return pl.pallas_call(
        _mod_kernel,
        grid=(n_out // tn,),
        in_specs=[
            pl.BlockSpec((rows, D_MODEL), lambda j: (0, 0)),
            pl.BlockSpec((D_MODEL, tn), lambda j: (0, j)),
            pl.BlockSpec((1, tn), lambda j: (0, j)),
        ],
        out_specs=pl.BlockSpec((rows, tn), lambda j: (0, j)),
        out_shape=jax.ShapeDtypeStruct((rows, n_out), F32),
        compiler_params=pltpu.CompilerParams(dimension_semantics=("arbitrary",)),
        name="adaln_mod",
    )(c_all, w_ada, b_ada)


def _mixer_kernel(x_ref, mod_ref, gpre_ref, gpost_ref, cos_ref, sin_ref, decay_ref, rs_ref,
                  w_in_ref, convw_ref, convb_ref, lng_ref, lnb_ref, wco_ref, gng_ref, wro_ref, wo_ref,
                  wg_f_ref, wu_f_ref, wd_f_ref,
                  x1_ref, rstate_ref, cstate_ref, wg_p_ref, wu_p_ref, wd_p_ref,
                  full_ref, conv_ref, gated_ref):
    t = pl.program_id(1)
    T = T_MIX
    n_col = D_MODEL // LANES

    _pack_slabs((wg_f_ref, wu_f_ref, wd_f_ref), (wg_p_ref, wu_p_ref, wd_p_ref))

    @pl.when(t == 0)
    def _():
        rstate_ref[...] = jnp.zeros_like(rstate_ref)
        full_ref[:, 0:HIST, :] = jnp.zeros((n_col, HIST, LANES), F32)

    mod = mod_ref[0]
    sh1 = mod[:, 0:D_MODEL]
    sc1 = mod[:, D_MODEL:2 * D_MODEL]
    gt1 = mod[:, 2 * D_MODEL:3 * D_MODEL]
    hb = (_rms(x_ref[0], gpre_ref[...]) * (1.0 + sc1) + sh1).astype(BF16)

    def proj(lo, width):
        return _dot(hb, _w(w_in_ref[:, lo:lo + width]))

    glu = proj(OFF_U, D_MODEL) * jax.nn.sigmoid(proj(OFF_A, D_MODEL))
    for c in range(n_col):
        full_ref[c, HIST:HIST + T, :] = glu[:, c * LANES:(c + 1) * LANES]

    @pl.when(t == pl.num_programs(1) - 1)
    def _():
        for c in range(n_col):
            cstate_ref[0, :, c * LANES:(c + 1) * LANES] = full_ref[c, HIST + T - CONV_STATE:HIST + T, :]

    def conv_block(c):
        lanes = slice(c * LANES, (c + 1) * LANES)
        for r0 in range(0, T, ROW_CHUNK):
            acc = jnp.broadcast_to(convb_ref[:, lanes], (ROW_CHUNK, LANES))
            for j in range(CONV_WIDTH):
                start = HIST - CONV_STATE + j + r0
                acc = acc + full_ref[c, start:start + ROW_CHUNK, :] * convw_ref[j:j + 1, lanes]
            conv_ref[c, r0:r0 + ROW_CHUNK, :] = acc

    blocks_per_head = n_col // N_HEADS
    for h in range(N_HEADS):
        zq = proj(OFF_Q + h * DK, DK)
        zk = proj(OFF_K + h * DK, DK)
        zv = proj(OFF_V + h * DV, DV).astype(BF16)
        zg = proj(OFF_G + h * DV, DV)
        conv_block(blocks_per_head * h)
        for ci in range(T // CHUNK):
            rows = slice(ci * CHUNK, (ci + 1) * CHUNK)
            q = _rotary(zq[rows], cos_ref[rows, :], sin_ref[rows, :]) * (DK ** -0.5)
            k = _rotary(zk[rows], cos_ref[rows, :], sin_ref[rows, :])
            vb = zv[rows]
            qb = q.astype(BF16)
            scores = lax.dot_general(qb, k.astype(BF16), (((1,), (1,)), ((), ())),
                                     preferred_element_type=F32) * decay_ref[h]
            s_prev = rstate_ref[0, h]
            y = _dot(scores.astype(BF16), vb)
            y = y + _dot(qb, s_prev.astype(BF16)) * rs_ref[:, h:h + 1]
            k_dec = (k * rs_ref[:, N_HEADS + h:N_HEADS + h + 1]).astype(BF16)
            upd = lax.dot_general(k_dec, vb, (((0,), (0,)), ((), ())), preferred_element_type=F32)
            rstate_ref[0, h] = rs_ref[0:1, 2 * N_HEADS + h:2 * N_HEADS + h + 1] * s_prev + upd
            yn = _standardize(y) * gng_ref[:, h * DV:(h + 1) * DV]
            gated_ref[rows, h * DV:(h + 1) * DV] = (jax.nn.silu(zg[rows]) * yn).astype(BF16)
        for c in range(blocks_per_head * h + 1, blocks_per_head * (h + 1)):
            conv_block(c)

    full_ref[:, 0:HIST, :] = full_ref[:, T:T + HIST, :]
    conv = jnp.concatenate([conv_ref[c] for c in range(n_col)], axis=-1)
    ln = _standardize(conv) * lng_ref[...] + lnb_ref[...]
    conv_out = _dot(jax.nn.silu(ln).astype(BF16), _w(wco_ref[...]))
    ret_out = _dot(gated_ref[...], _w(wro_ref[...]))

    merged = (jax.nn.sigmoid(proj(OFF_GR, D_MODEL)) * ret_out
              + jax.nn.sigmoid(proj(OFF_GC, D_MODEL)) * conv_out)
    mix = _dot(merged.astype(BF16), _w(wo_ref[...]))
    x1_ref[0] = x_ref[0] + gt1 * _rms(mix, gpost_ref[...])


def _slab_spec(total_rows, cols, steps, step_of, packed):
    rows = next(r for r in range(BF16_TILE_ROWS, total_rows + 1, BF16_TILE_ROWS)
                if total_rows % r == 0 and total_rows // r <= steps)
    last = total_rows // rows - 1
    return pl.BlockSpec((rows // 2 if packed else rows, cols),
                        lambda *idx: (jnp.minimum(step_of(*idx), last), 0))


def _pack_slabs(srcs, dsts):
    for src, dst in zip(srcs, dsts):
        dst[...] = pltpu.bitcast(src[...].astype(BF16), jnp.uint32)


def _prompt_mixer(x, mod_p, g_pre1, g_post1, cos_p, sin_p, decay, rs_tab, w_in_b, conv_w, conv_b,
                  ln_g, ln_b, w_conv_out_b, gn_g, w_ret_out_b, w_out_b, w_gate, w_up, w_down):
    B, L, _ = x.shape
    T = T_MIX
    nt = L // T
    d_ff = w_gate.shape[1]
    row = lambda b, t: (0, 0)
    step_of = lambda b, t: b * nt + t
    slabs = [(D_MODEL, d_ff), (D_MODEL, d_ff), (d_ff, D_MODEL)]
    in_specs = [
        pl.BlockSpec((1, T, D_MODEL), lambda b, t: (b, t, 0)),
        pl.BlockSpec((1, 1, 6 * D_MODEL), lambda b, t: (b, 0, 0)),
        pl.BlockSpec((1, D_MODEL), row),
        pl.BlockSpec((1, D_MODEL), row),
        pl.BlockSpec((T, HALF), lambda b, t: (t, 0)),
        pl.BlockSpec((T, HALF), lambda b, t: (t, 0)),
        _const_spec((N_HEADS, CHUNK, CHUNK)),
        _const_spec((CHUNK, LANES)),
        _const_spec((D_MODEL // 2, D_IN)),
        pl.BlockSpec((CONV_WIDTH, D_MODEL), row),
        pl.BlockSpec((1, D_MODEL), row),
        pl.BlockSpec((1, D_MODEL), row),
        pl.BlockSpec((1, D_MODEL), row),
        _const_spec((D_MODEL // 2, D_MODEL)),
        pl.BlockSpec((1, RET_V), row),
        _const_spec((RET_V // 2, D_MODEL)),
        _const_spec((D_MODEL // 2, D_MODEL)),
    ] + [_slab_spec(r, c, B * nt, step_of, packed=False) for r, c in slabs]
    out_specs = [
        pl.BlockSpec((1, T, D_MODEL), lambda b, t: (b, t, 0)),
        pl.BlockSpec((1, N_HEADS, DK, DV), lambda b, t: (b, 0, 0, 0)),
        pl.BlockSpec((1, CONV_STATE, D_MODEL), lambda b, t: (b, 0, 0)),
    ] + [_slab_spec(r, c, B * nt, step_of, packed=True) for r, c in slabs]
    out_shape = [
        jax.ShapeDtypeStruct((B, L, D_MODEL), F32),
        jax.ShapeDtypeStruct((B, N_HEADS, DK, DV), F32),
        jax.ShapeDtypeStruct((B, CONV_STATE, D_MODEL), F32),
    ] + [jax.ShapeDtypeStruct((r // 2, c), jnp.uint32) for r, c in slabs]
    return pl.pallas_call(
        _mixer_kernel,
        grid=(B, L // T),
        in_specs=in_specs,
        out_specs=out_specs,
        out_shape=out_shape,
        scratch_shapes=[
            pltpu.VMEM((D_MODEL // LANES, HIST + T, LANES), F32),
            pltpu.VMEM((D_MODEL // LANES, T, LANES), F32),
            pltpu.VMEM((T, RET_V), BF16),
        ],
        compiler_params=pltpu.CompilerParams(
            dimension_semantics=("arbitrary", "arbitrary"),
            vmem_limit_bytes=56 * MIB),
        name="prompt_mixer",
    )(x, mod_p, g_pre1, g_post1, cos_p, sin_p, decay, rs_tab, w_in_b, conv_w, conv_b,
      ln_g, ln_b, w_conv_out_b, gn_g, w_ret_out_b, w_out_b, w_gate, w_up, w_down)


def _ffn_body(x, sh2, sc2, gt2, gpre, gpost, wg_ref, wu_ref, wd_ref):
    hb = (_rms(x, gpre) * (1.0 + sc2) + sh2).astype(BF16)
    act = (jax.nn.silu(_dot(hb, _w(wg_ref[...]))) * _dot(hb, _w(wu_ref[...]))).astype(BF16)
    return x + gt2 * _rms(_dot(act, _w(wd_ref[...])), gpost)


def _sample_ret_step(q, k, v, cos, sin, gam_ref, s0_ref, snew_ref, y_ref, i):
    q = _rotary(q, cos, sin) * (DK ** -0.5)
    k = _rotary(k, cos, sin)
    qk = jnp.sum(q * k, axis=-1, keepdims=True)
    cols = jnp.concatenate([k, q, jnp.zeros((LANES - 2 * N_HEADS, DK), F32)], axis=0).T
    for h in range(N_HEADS):
        s0 = s0_ref[i, h]
        v_h = v[h:h + 1, :]
        gam = gam_ref[h:h + 1, :]
        snew_ref[i, h] = gam * s0 + cols[:, h:h + 1] * v_h
        qs = jnp.sum(cols[:, N_HEADS + h:N_HEADS + h + 1] * s0, axis=0, keepdims=True)
        y_ref[i, h:h + 1, :] = qk[h:h + 1, :] * v_h + gam * qs


def _ffn_ret_kernel(x_ref, mod_ref, gpre_ref, gpost_ref, wg_ref, wu_ref, wd_ref,
                    q_ref, k_ref, v_ref, cos_ref, sin_ref, gam_ref, s0_ref,
                    o_ref, snew_ref, y_ref):
    mod = mod_ref[0]
    o_ref[0] = _ffn_body(x_ref[0], mod[:, 3 * D_MODEL:4 * D_MODEL], mod[:, 4 * D_MODEL:5 * D_MODEL],
                         mod[:, 5 * D_MODEL:6 * D_MODEL], gpre_ref[...], gpost_ref[...],
                         wg_ref, wu_ref, wd_ref)
    for i in range(q_ref.shape[0]):
        _sample_ret_step(q_ref[i], k_ref[i], v_ref[i], cos_ref[...], sin_ref[...], gam_ref,
                         s0_ref, snew_ref, y_ref, i)


def _prompt_ffn_sample_ret(x1, mod_p, g_pre2, g_post2, wg_b, wu_b, wd_b,
                           q_s, k_s, v_s, cos_s, sin_s, gam_tab, state):
    B, L, _ = x1.shape
    nb = q_s.shape[0]
    d_ff = wg_b.shape[1]
    nt = L // T_FFN
    steps = B * nt
    sb = nb // steps
    assert sb * steps == nb
    row = lambda b, t: (0, 0)
    blk = lambda b, t: (b * nt + t, 0, 0)
    return pl.pallas_call(
        _ffn_ret_kernel,
        grid=(B, nt),
        in_specs=[
            pl.BlockSpec((1, T_FFN, D_MODEL), lambda b, t: (b, t, 0)),
            pl.BlockSpec((1, 1, 6 * D_MODEL), lambda b, t: (b, 0, 0)),
            pl.BlockSpec((1, D_MODEL), row),
            pl.BlockSpec((1, D_MODEL), row),
            _const_spec((D_MODEL // 2, d_ff)),
            _const_spec((D_MODEL // 2, d_ff)),
            _const_spec((d_ff // 2, D_MODEL)),
            pl.BlockSpec((sb, N_HEADS, DK), blk),
            pl.BlockSpec((sb, N_HEADS, DK), blk),
            pl.BlockSpec((sb, N_HEADS, DV), blk),
            pl.BlockSpec((1, HALF), row),
            pl.BlockSpec((1, HALF), row),
            pl.BlockSpec((N_HEADS, DV), row),
            pl.BlockSpec((sb, N_HEADS, DK, DV), lambda b, t: (b * nt + t, 0, 0, 0)),
        ],
        out_specs=[
            pl.BlockSpec((1, T_FFN, D_MODEL), lambda b, t: (b, t, 0)),
            pl.BlockSpec((sb, N_HEADS, DK, DV), lambda b, t: (b * nt + t, 0, 0, 0)),
            pl.BlockSpec((sb, N_HEADS, DV), blk),
        ],
        out_shape=[
            jax.ShapeDtypeStruct((B, L, D_MODEL), F32),
            jax.ShapeDtypeStruct((nb, N_HEADS, DK, DV), F32),
            jax.ShapeDtypeStruct((nb, N_HEADS, DV), F32),
        ],
        compiler_params=pltpu.CompilerParams(
            dimension_semantics=("arbitrary", "arbitrary"),
            vmem_limit_bytes=56 * MIB),
        name="prompt_ffn_sample_ret",
    )(x1, mod_p, g_pre2, g_post2, wg_b, wu_b, wd_b, q_s, k_s, v_s, cos_s, sin_s, gam_tab, state)


def _sample_in_kernel(x_ref, sh_ref, sc_ref, gpre_ref, w_ref, wro_f_ref, wco_f_ref, wo_f_ref,
                      z_ref, wp_ref, wro_p_ref, wco_p_ref, wo_p_ref):
    hb = (_rms(x_ref[...], gpre_ref[...]) * (1.0 + sc_ref[...]) + sh_ref[...]).astype(BF16)
    wb = w_ref[...].astype(BF16)
    z_ref[...] = _dot(hb, wb)
    wp_ref[...] = pltpu.bitcast(wb, jnp.uint32)
    _pack_slabs((wro_f_ref, wco_f_ref, wo_f_ref), (wro_p_ref, wco_p_ref, wo_p_ref))


def _sample_in(xs, mod, g_pre1, w_in, w_ret_out, w_conv_out, w_out):
    rows = xs.shape[0]
    tn = D_MODEL
    steps = D_IN // tn
    slabs = [w_ret_out.shape, w_conv_out.shape, w_out.shape]
    step_of = lambda j: j
    return pl.pallas_call(
        _sample_in_kernel,
        grid=(steps,),
        in_specs=[
            pl.BlockSpec((rows, D_MODEL), lambda j: (0, 0)),
            pl.BlockSpec((rows, D_MODEL), lambda j: (0, 0)),
            pl.BlockSpec((rows, D_MODEL), lambda j: (0, 1)),
            pl.BlockSpec((1, D_MODEL), lambda j: (0, 0)),
            pl.BlockSpec((D_MODEL, tn), lambda j: (0, j)),
        ] + [_slab_spec(r, c, steps, step_of, packed=False) for r, c in slabs],
        out_specs=[
            pl.BlockSpec((rows, tn), lambda j: (0, j)),
            pl.BlockSpec((D_MODEL // 2, tn), lambda j: (0, j)),
        ] + [_slab_spec(r, c, steps, step_of, packed=True) for r, c in slabs],
        out_shape=[
            jax.ShapeDtypeStruct((rows, D_IN), F32),
            jax.ShapeDtypeStruct((D_MODEL // 2, D_IN), jnp.uint32),
        ] + [jax.ShapeDtypeStruct((r // 2, c), jnp.uint32) for r, c in slabs],
        compiler_params=pltpu.CompilerParams(dimension_semantics=("arbitrary",)),
        name="sample_in_proj",
    )(xs, mod, mod, g_pre1, w_in, w_ret_out, w_conv_out, w_out)


def _sample_conv_kernel(u_ref, a_ref, st_ref, w_ref, b_ref, new_ref, conv_ref):
    glu = u_ref[...] * jax.nn.sigmoid(a_ref[...])
    acc = glu * w_ref[CONV_STATE:CONV_WIDTH, :] + b_ref[...]
    for j in range(CONV_STATE):
        acc = acc + st_ref[0, j] * w_ref[j:j + 1, :]
    conv_ref[...] = acc
    for j in range(CONV_STATE - 1):
        new_ref[0, j] = st_ref[0, j + 1]
    new_ref[0, CONV_STATE - 1] = glu


def _sample_conv(z_s, state_conv_t, conv_w, conv_b):
    nb = z_s.shape[0]
    return pl.pallas_call(
        _sample_conv_kernel,
        grid=(nb // CONV_B,),
        in_specs=[
            pl.BlockSpec((CONV_B, D_MODEL), lambda i: (i, OFF_U // D_MODEL)),
            pl.BlockSpec((CONV_B, D_MODEL), lambda i: (i, OFF_A // D_MODEL)),
            pl.BlockSpec((1, CONV_STATE, CONV_B, D_MODEL), lambda i: (0, 0, i, 0)),
            pl.BlockSpec((CONV_WIDTH, D_MODEL), lambda i: (0, 0)),
            pl.BlockSpec((1, D_MODEL), lambda i: (0, 0)),
        ],
        out_specs=[
            pl.BlockSpec((1, CONV_STATE, CONV_B, D_MODEL), lambda i: (0, 0, i, 0)),
            pl.BlockSpec((CONV_B, D_MODEL), lambda i: (i, 0)),
        ],
        out_shape=[
            jax.ShapeDtypeStruct((1, CONV_STATE, nb, D_MODEL), F32),
            jax.ShapeDtypeStruct((nb, D_MODEL), F32),
        ],
        compiler_params=pltpu.CompilerParams(dimension_semantics=("arbitrary",)),
        name="sample_conv",
    )(z_s, z_s, state_conv_t, conv_w, conv_b)


def _sample_post_kernel(x_ref, mod_ref, z_ref, y_ref, conv_ref, gng_ref, wro_ref, lng_ref, lnb_ref,
                        wco_ref, wo_ref, gpost1_ref, gpre2_ref, gpost2_ref, wg_ref, wu_ref, wd_ref, o_ref):
    x = x_ref[...]
    rows = x.shape[0]

    def mod(i):
        return mod_ref[0:rows, i * D_MODEL:(i + 1) * D_MODEL]

    ret_out = jnp.zeros((rows, D_MODEL), F32)
    for h in range(N_HEADS):
        sl = slice(h * DV, (h + 1) * DV)
        yn = _standardize(y_ref[:, sl]) * gng_ref[:, sl]
        gated = (jax.nn.silu(z_ref[:, OFF_G + h * DV:OFF_G + (h + 1) * DV]) * yn).astype(BF16)
        ret_out = ret_out + _dot(gated, _w(wro_ref[h * DV // 2:(h + 1) * DV // 2, :]))
    ln = _standardize(conv_ref[...]) * lng_ref[...] + lnb_ref[...]
    conv_out = _dot(jax.nn.silu(ln).astype(BF16), _w(wco_ref[...]))
    merged = (jax.nn.sigmoid(z_ref[:, OFF_GR:OFF_GR + D_MODEL]) * ret_out
              + jax.nn.sigmoid(z_ref[:, OFF_GC:OFF_GC + D_MODEL]) * conv_out)
    mix = _dot(merged.astype(BF16), _w(wo_ref[...]))
    x1 = x + mod(2) * _rms(mix, gpost1_ref[...])
    o_ref[...] = _ffn_body(x1, mod(3), mod(4), mod(5), gpre2_ref[...], gpost2_ref[...],
                           wg_ref, wu_ref, wd_ref)


def _sample_post(xs, mod_s, z_s, y_s, conv_s, gn_g, w_ret_out_b, ln_g, ln_b, w_conv_out_b, w_out_b,
                 g_post1, g_pre2, g_post2, wg_b, wu_b, wd_b):
    return pl.pallas_call(
        _sample_post_kernel,
        out_shape=jax.ShapeDtypeStruct(xs.shape, F32),
        compiler_params=pltpu.CompilerParams(vmem_limit_bytes=56 * MIB),
        name="sample_post",
    )(xs, mod_s, z_s, y_s, conv_s, gn_g, w_ret_out_b, ln_g, ln_b, w_conv_out_b, w_out_b,
      g_post1, g_pre2, g_post2, wg_b, wu_b, wd_b)


def _rope_tables(pos):
    inv_freq = ROPE_BASE ** (-np.arange(0, DK, 2, dtype=np.float64) / DK)
    ang = np.asarray(pos, np.float64)[:, None] * inv_freq[None, :]
    return jnp.asarray(np.cos(ang), F32), jnp.asarray(np.sin(ang), F32)


def _log_gamma():
    return np.log(1.0 - np.exp(np.linspace(np.log(1.0 / 32.0), np.log(1.0 / 512.0), N_HEADS)))


def _decay_tables(chunk):
    lg = _log_gamma()
    idx = np.arange(chunk, dtype=np.float64)
    diff = idx[:, None] - idx[None, :]
    decay = np.where(diff >= 0, np.exp(lg[:, None, None] * np.maximum(diff, 0.0)), 0.0)
    cross = np.exp(lg[None, :] * (idx[:, None] + 1.0))
    k_dec = np.exp(lg[None, :] * (chunk - 1.0 - idx[:, None]))
    full = np.broadcast_to(np.exp(lg * chunk)[None, :], (chunk, N_HEADS))
    rs = np.concatenate([cross, k_dec, full], axis=1)
    rs = np.pad(rs, ((0, 0), (0, LANES - rs.shape[1])))
    return jnp.asarray(decay, F32), jnp.asarray(rs, F32)


def kernel(x_prompt, x_sample, c_prompt, c_sample, state_ret, state_conv, w_in, w_ada, b_ada, g_pre1, g_post1, g_pre2, g_post2, conv_w, conv_b, conv_ln_g, conv_ln_b, w_conv_out, ret_gn_g, w_ret_out, w_out, w_ffn_gate, w_ffn_up, w_ffn_down):
    depth = w_in.shape[0]
    assert depth == 1, "single-layer step"
    B, L, _ = x_prompt.shape
    nb = x_sample.shape[0]

    cos_p, sin_p = _rope_tables(np.arange(L))
    cos_s, sin_s = _rope_tables(PAST_LEN + np.arange(x_sample.shape[1]))
    decay, rs_tab = _decay_tables(CHUNK)
    gam_tab = jnp.asarray(np.broadcast_to(np.exp(_log_gamma())[:, None], (N_HEADS, DV)), F32)

    c_all = jnp.concatenate([c_sample, c_prompt], axis=0)
    mod = _modulation(c_all, w_ada[0], b_ada)
    mod_p = mod[nb:].reshape(B, 1, 6 * D_MODEL)

    xs = x_sample.reshape(nb, D_MODEL)
    z_s, w_in_b, w_ret_out_b, w_conv_out_b, w_out_b = _sample_in(
        xs, mod, g_pre1, w_in[0], w_ret_out[0], w_conv_out[0], w_out[0])
    q_s = z_s[:, OFF_Q:OFF_Q + RET_QK].reshape(nb, N_HEADS, DK)
    k_s = z_s[:, OFF_K:OFF_K + RET_QK].reshape(nb, N_HEADS, DK)
    v_s = z_s[:, OFF_V:OFF_V + RET_V].reshape(nb, N_HEADS, DV)

    x1_p, ret_p, conv_p, wg_b, wu_b, wd_b = _prompt_mixer(
        x_prompt, mod_p, g_pre1, g_post1, cos_p, sin_p, decay, rs_tab, w_in_b, conv_w[0], conv_b,
        conv_ln_g, conv_ln_b, w_conv_out_b, ret_gn_g, w_ret_out_b, w_out_b,
        w_ffn_gate[0], w_ffn_up[0], w_ffn_down[0])
    y_p, ret_s, yr_s = _prompt_ffn_sample_ret(
        x1_p, mod_p, g_pre2, g_post2, wg_b, wu_b, wd_b, q_s, k_s, v_s, cos_s, sin_s, gam_tab, state_ret[0])

    conv_new_t, conv_s = _sample_conv(z_s, state_conv.transpose(0, 2, 1, 3), conv_w[0], conv_b)
    conv_new_s = conv_new_t.transpose(0, 2, 1, 3)
    y_s = _sample_post(xs, mod, z_s, yr_s.reshape(nb, RET_V), conv_s, ret_gn_g, w_ret_out_b,
                       conv_ln_g, conv_ln_b, w_conv_out_b, w_out_b, g_post1, g_pre2, g_post2,
                       wg_b, wu_b, wd_b)

    return (y_p, y_s.reshape(x_sample.shape), ret_p[None], ret_s[None], conv_p[None], conv_new_s)
```

```python
import functools

import numpy as np

import jax
import jax.numpy as jnp
from jax import lax
from jax.experimental import pallas as pl
from jax.experimental.pallas import tpu as pltpu

F32 = jnp.float32
BF16 = jnp.bfloat16

D_MODEL = 1024
N_HEADS = 4
DK = 256
DV = 512
HALF = DK // 2
RET_QK = N_HEADS * DK
RET_V = N_HEADS * DV
CONV_WIDTH = 31
CONV_STATE = CONV_WIDTH - 1
EPS = 1e-6
ROPE_BASE = 10000.0
PAST_LEN = 16384

OFF_Q = 0
OFF_K = RET_QK
OFF_V = 2 * RET_QK
OFF_G = OFF_V + RET_V
OFF_U = OFF_G + RET_V
OFF_A = OFF_U + D_MODEL
OFF_GR = OFF_A + D_MODEL
OFF_GC = OFF_GR + D_MODEL
D_IN = OFF_GC + D_MODEL

LANES = 128
BF16_TILE_ROWS = 16
T_MIX = 256
CHUNK = 256
HIST = 32
ROW_CHUNK = 64
T_FFN = 256
CONV_B = 8
MIB = 1024 * 1024


def _rms(x, g):
    ms = jnp.mean(x * x, axis=-1, keepdims=True)
    return x * lax.rsqrt(ms + EPS) * g


def _rotary(t, cos, sin):
    t1, t2 = t[:, :HALF], t[:, HALF:]
    return jnp.concatenate([t1 * cos - t2 * sin, t2 * cos + t1 * sin], axis=-1)


def _standardize(y):
    mu = jnp.mean(y, axis=-1, keepdims=True)
    yc = y - mu
    var = jnp.mean(yc * yc, axis=-1, keepdims=True)
    return yc * lax.rsqrt(var + EPS)


def _dot(a, b):
    return jnp.dot(a, b, preferred_element_type=F32)


def _w(packed):
    return pltpu.bitcast(packed, BF16)


def _const_spec(shape):
    return pl.BlockSpec(shape, lambda *_: (0,) * len(shape), pipeline_mode=pl.Buffered(1))


def _mod_kernel(cs_ref, cp_ref, w_ref, b_ref, os_ref, op_ref):
    wb = w_ref[...].astype(BF16)
    for c_ref, o_ref in ((cs_ref, os_ref), (cp_ref, op_ref)):
        o_ref[...] = _dot(jax.nn.silu(c_ref[...]).astype(BF16), wb) + b_ref[...]


def _modulation(c_sample, c_prompt, w_ada, b_ada):
    n_out = w_ada.shape[1]
    tn = D_MODEL
    groups = (c_sample.shape[0], c_prompt.shape[0])
    return pl.pallas_call(
        _mod_kernel,
        grid=(n_out // tn,),
        in_specs=[pl.BlockSpec((rows, D_MODEL), lambda j: (0, 0)) for rows in groups] + [
            pl.BlockSpec((D_MODEL, tn), lambda j: (0, j)),
            pl.BlockSpec((1, tn), lambda j: (0, j)),
        ],
        out_specs=[pl.BlockSpec((rows, tn), lambda j: (0, j)) for rows in groups],
        out_shape=[jax.ShapeDtypeStruct((rows, n_out), F32) for rows in groups],
        compiler_params=pltpu.CompilerParams(dimension_semantics=("arbitrary",)),
        name="adaln_mod",
    )(c_sample, c_prompt, w_ada, b_ada)


def _mixer_kernel(x_ref, mod_ref, gpre_ref, gpost_ref, cos_ref, sin_ref, decay_ref, rs_ref,
                  w_in_ref, convw_ref, convb_ref, lng_ref, lnb_ref, wco_ref, gng_ref, wro_ref, wo_ref,
                  wg_f_ref, wu_f_ref, wd_f_ref,
                  x1_ref, rstate_ref, cstate_ref, wg_p_ref, wu_p_ref, wd_p_ref,
                  full_ref, conv_ref, gated_ref):
    t = pl.program_id(1)
    T = T_MIX
    n_col = D_MODEL // LANES

    _pack_slabs((wg_f_ref, wu_f_ref, wd_f_ref), (wg_p_ref, wu_p_ref, wd_p_ref))

    @pl.when(t == 0)
    def _():
        rstate_ref[...] = jnp.zeros_like(rstate_ref)
        full_ref[:, 0:HIST, :] = jnp.zeros((n_col, HIST, LANES), F32)

    mod = mod_ref[0]
    sh1 = mod[:, 0:D_MODEL]
    sc1 = mod[:, D_MODEL:2 * D_MODEL]
    gt1 = mod[:, 2 * D_MODEL:3 * D_MODEL]
    hb = (_rms(x_ref[0], gpre_ref[...]) * (1.0 + sc1) + sh1).astype(BF16)

    def proj(lo, width):
        return _dot(hb, _w(w_in_ref[:, lo:lo + width]))

    glu = proj(OFF_U, D_MODEL) * jax.nn.sigmoid(proj(OFF_A, D_MODEL))
    for c in range(n_col):
        full_ref[c, HIST:HIST + T, :] = glu[:, c * LANES:(c + 1) * LANES]

    @pl.when(t == pl.num_programs(1) - 1)
    def _():
        for c in range(n_col):
            cstate_ref[0, :, c * LANES:(c + 1) * LANES] = full_ref[c, HIST + T - CONV_STATE:HIST + T, :]

    def conv_block(c):
        lanes = slice(c * LANES, (c + 1) * LANES)
        for r0 in range(0, T, ROW_CHUNK):
            acc = jnp.broadcast_to(convb_ref[:, lanes], (ROW_CHUNK, LANES))
            for j in range(CONV_WIDTH):
                start = HIST - CONV_STATE + j + r0
                acc = acc + full_ref[c, start:start + ROW_CHUNK, :] * convw_ref[j:j + 1, lanes]
            conv_ref[c, r0:r0 + ROW_CHUNK, :] = acc

    blocks_per_head = n_col // N_HEADS
    for h in range(N_HEADS):
        zq = proj(OFF_Q + h * DK, DK)
        zk = proj(OFF_K + h * DK, DK)
        zv = proj(OFF_V + h * DV, DV).astype(BF16)
        zg = proj(OFF_G + h * DV, DV)
        conv_block(blocks_per_head * h)
        for ci in range(T // CHUNK):
            rows = slice(ci * CHUNK, (ci + 1) * CHUNK)
            q = _rotary(zq[rows], cos_ref[rows, :], sin_ref[rows, :]) * (DK ** -0.5)
            k = _rotary(zk[rows], cos_ref[rows, :], sin_ref[rows, :])
            vb = zv[rows]
            qb = q.astype(BF16)
            scores = lax.dot_general(qb, k.astype(BF16), (((1,), (1,)), ((), ())),
                                     preferred_element_type=F32) * decay_ref[h]
            s_prev = rstate_ref[0, h]
            y = _dot(scores.astype(BF16), vb)
            y = y + _dot(qb, s_prev.astype(BF16)) * rs_ref[:, h:h + 1]
            k_dec = (k * rs_ref[:, N_HEADS + h:N_HEADS + h + 1]).astype(BF16)
            upd = lax.dot_general(k_dec, vb, (((0,), (0,)), ((), ())), preferred_element_type=F32)
            rstate_ref[0, h] = rs_ref[0:1, 2 * N_HEADS + h:2 * N_HEADS + h + 1] * s_prev + upd
            yn = _standardize(y) * gng_ref[:, h * DV:(h + 1) * DV]
            gated_ref[rows, h * DV:(h + 1) * DV] = (jax.nn.silu(zg[rows]) * yn).astype(BF16)
        for c in range(blocks_per_head * h + 1, blocks_per_head * (h + 1)):
            conv_block(c)

    full_ref[:, 0:HIST, :] = full_ref[:, T:T + HIST, :]
    conv = jnp.concatenate([conv_ref[c] for c in range(n_col)], axis=-1)
    ln = _standardize(conv) * lng_ref[...] + lnb_ref[...]
    conv_out = _dot(jax.nn.silu(ln).astype(BF16), _w(wco_ref[...]))
    ret_out = _dot(gated_ref[...], _w(wro_ref[...]))

    merged = (jax.nn.sigmoid(proj(OFF_GR, D_MODEL)) * ret_out
              + jax.nn.sigmoid(proj(OFF_GC, D_MODEL)) * conv_out)
    mix = _dot(merged.astype(BF16), _w(wo_ref[...]))
    x1_ref[0] = x_ref[0] + gt1 * _rms(mix, gpost_ref[...])


def _slab_spec(total_rows, cols, steps, step_of, packed):
    rows = next(r for r in range(BF16_TILE_ROWS, total_rows + 1, BF16_TILE_ROWS)
                if total_rows % r == 0 and total_rows // r <= steps)
    last = total_rows // rows - 1
    return pl.BlockSpec((rows // 2 if packed else rows, cols),
                        lambda *idx: (jnp.minimum(step_of(*idx), last), 0))


def _pack_slabs(srcs, dsts):
    for src, dst in zip(srcs, dsts):
        dst[...] = pltpu.bitcast(src[...].astype(BF16), jnp.uint32)


def _prompt_mixer(x, mod_p, g_pre1, g_post1, cos_p, sin_p, decay, rs_tab, w_in_b, conv_w, conv_b,
                  ln_g, ln_b, w_conv_out_b, gn_g, w_ret_out_b, w_out_b, w_gate, w_up, w_down):
    B, L, _ = x.shape
    T = T_MIX
    nt = L // T
    d_ff = w_gate.shape[1]
    row = lambda b, t: (0, 0)
    step_of = lambda b, t: b * nt + t
    slabs = [(D_MODEL, d_ff), (D_MODEL, d_ff), (d_ff, D_MODEL)]
    in_specs = [
        pl.BlockSpec((1, T, D_MODEL), lambda b, t: (b, t, 0)),
        pl.BlockSpec((1, 1, 6 * D_MODEL), lambda b, t: (b, 0, 0)),
        pl.BlockSpec((1, D_MODEL), row),
        pl.BlockSpec((1, D_MODEL), row),
        pl.BlockSpec((T, HALF), lambda b, t: (t, 0)),
        pl.BlockSpec((T, HALF), lambda b, t: (t, 0)),
        _const_spec((N_HEADS, CHUNK, CHUNK)),
        _const_spec((CHUNK, LANES)),
        _const_spec((D_MODEL // 2, D_IN)),
        pl.BlockSpec((CONV_WIDTH, D_MODEL), row),
        pl.BlockSpec((1, D_MODEL), row),
        pl.BlockSpec((1, D_MODEL), row),
        pl.BlockSpec((1, D_MODEL), row),
        _const_spec((D_MODEL // 2, D_MODEL)),
        pl.BlockSpec((1, RET_V), row),
        _const_spec((RET_V // 2, D_MODEL)),
        _const_spec((D_MODEL // 2, D_MODEL)),
    ] + [_slab_spec(r, c, B * nt, step_of, packed=False) for r, c in slabs]
    out_specs = [
        pl.BlockSpec((1, T, D_MODEL), lambda b, t: (b, t, 0)),
        pl.BlockSpec((1, N_HEADS, DK, DV), lambda b, t: (b, 0, 0, 0)),
        pl.BlockSpec((1, CONV_STATE, D_MODEL), lambda b, t: (b, 0, 0)),
    ] + [_slab_spec(r, c, B * nt, step_of, packed=True) for r, c in slabs]
    out_shape = [
        jax.ShapeDtypeStruct((B, L, D_MODEL), F32),
        jax.ShapeDtypeStruct((B, N_HEADS, DK, DV), F32),
        jax.ShapeDtypeStruct((B, CONV_STATE, D_MODEL), F32),
    ] + [jax.ShapeDtypeStruct((r // 2, c), jnp.uint32) for r, c in slabs]
    return pl.pallas_call(
        _mixer_kernel,
        grid=(B, L // T),
        in_specs=in_specs,
        out_specs=out_specs,
        out_shape=out_shape,
        scratch_shapes=[
            pltpu.VMEM((D_MODEL // LANES, HIST + T, LANES), F32),
            pltpu.VMEM((D_MODEL // LANES, T, LANES), F32),
            pltpu.VMEM((T, RET_V), BF16),
        ],
        compiler_params=pltpu.CompilerParams(
            dimension_semantics=("arbitrary", "arbitrary"),
            vmem_limit_bytes=56 * MIB),
        name="prompt_mixer",
    )(x, mod_p, g_pre1, g_post1, cos_p, sin_p, decay, rs_tab, w_in_b, conv_w, conv_b,
      ln_g, ln_b, w_conv_out_b, gn_g, w_ret_out_b, w_out_b, w_gate, w_up, w_down)


def _ffn_body(x, sh2, sc2, gt2, gpre, gpost, wg_ref, wu_ref, wd_ref):
    hb = (_rms(x, gpre) * (1.0 + sc2) + sh2).astype(BF16)
    act = (jax.nn.silu(_dot(hb, _w(wg_ref[...]))) * _dot(hb, _w(wu_ref[...]))).astype(BF16)
    return x + gt2 * _rms(_dot(act, _w(wd_ref[...])), gpost)


def _sample_ret_step(q, k, v, cos, sin, gam_ref, s0_ref, snew_ref, y_ref, i):
    q = _rotary(q, cos, sin) * (DK ** -0.5)
    k = _rotary(k, cos, sin)
    qk = jnp.sum(q * k, axis=-1, keepdims=True)
    cols = jnp.concatenate([k, q, jnp.zeros((LANES - 2 * N_HEADS, DK), F32)], axis=0).T
    for h in range(N_HEADS):
        s0 = s0_ref[i, h]
        v_h = v[h:h + 1, :]
        gam = gam_ref[h:h + 1, :]
        snew_ref[i, h] = gam * s0 + cols[:, h:h + 1] * v_h
        qs = jnp.sum(cols[:, N_HEADS + h:N_HEADS + h + 1] * s0, axis=0, keepdims=True)
        y_ref[i, h:h + 1, :] = qk[h:h + 1, :] * v_h + gam * qs


def _ffn_ret_kernel(x_ref, mod_ref, gpre_ref, gpost_ref, wg_ref, wu_ref, wd_ref,
                    q_ref, k_ref, v_ref, cos_ref, sin_ref, gam_ref, s0_ref,
                    u_ref, a_ref, cst_ref, convw_ref, convb_ref,
                    o_ref, snew_ref, y_ref, cnew_ref, conv_ref, *, conv_steps):
    step = pl.program_id(0) * pl.num_programs(1) + pl.program_id(1)

    @pl.when(step < conv_steps)
    def _():
        _sample_conv_step(u_ref, a_ref, cst_ref, convw_ref, convb_ref, cnew_ref, conv_ref)

    mod = mod_ref[0]
    o_ref[0] = _ffn_body(x_ref[0], mod[:, 3 * D_MODEL:4 * D_MODEL], mod[:, 4 * D_MODEL:5 * D_MODEL],
                         mod[:, 5 * D_MODEL:6 * D_MODEL], gpre_ref[...], gpost_ref[...],
                         wg_ref, wu_ref, wd_ref)
    for i in range(q_ref.shape[0]):
        _sample_ret_step(q_ref[i], k_ref[i], v_ref[i], cos_ref[...], sin_ref[...], gam_ref,
                         s0_ref, snew_ref, y_ref, i)


def _prompt_ffn_sample_ret(x1, mod_p, g_pre2, g_post2, wg_b, wu_b, wd_b,
                           q_s, k_s, v_s, cos_s, sin_s, gam_tab, state,
                           z_s, state_conv_t, conv_w, conv_b):
    B, L, _ = x1.shape
    nb = q_s.shape[0]
    d_ff = wg_b.shape[1]
    nt = L // T_FFN
    steps = B * nt
    sb = nb // steps
    assert sb * steps == nb
    conv_steps = nb // CONV_B
    assert conv_steps * CONV_B == nb and conv_steps <= steps
    row = lambda b, t: (0, 0)
    blk = lambda b, t: (b * nt + t, 0, 0)
    cblk = lambda b, t: jnp.minimum(b * nt + t, conv_steps - 1)
    return pl.pallas_call(
        functools.partial(_ffn_ret_kernel, conv_steps=conv_steps),
        grid=(B, nt),
        in_specs=[
            pl.BlockSpec((1, T_FFN, D_MODEL), lambda b, t: (b, t, 0)),
            pl.BlockSpec((1, 1, 6 * D_MODEL), lambda b, t: (b, 0, 0)),
            pl.BlockSpec((1, D_MODEL), row),
            pl.BlockSpec((1, D_MODEL), row),
            _const_spec((D_MODEL // 2, d_ff)),
            _const_spec((D_MODEL // 2, d_ff)),
            _const_spec((d_ff // 2, D_MODEL)),
            pl.BlockSpec((sb, N_HEADS, DK), blk),
            pl.BlockSpec((sb, N_HEADS, DK), blk),
            pl.BlockSpec((sb, N_HEADS, DV), blk),
            pl.BlockSpec((1, HALF), row),
            pl.BlockSpec((1, HALF), row),
            pl.BlockSpec((N_HEADS, DV), row),
            pl.BlockSpec((sb, N_HEADS, DK, DV), lambda b, t: (b * nt + t, 0, 0, 0)),
            pl.BlockSpec((CONV_B, D_MODEL), lambda b, t: (cblk(b, t), OFF_U // D_MODEL)),
            pl.BlockSpec((CONV_B, D_MODEL), lambda b, t: (cblk(b, t), OFF_A // D_MODEL)),
            pl.BlockSpec((1, CONV_STATE, CONV_B, D_MODEL), lambda b, t: (0, 0, cblk(b, t), 0)),
            pl.BlockSpec((CONV_WIDTH, D_MODEL), row),
            pl.BlockSpec((1, D_MODEL), row),
        ],
        out_specs=[
            pl.BlockSpec((1, T_FFN, D_MODEL), lambda b, t: (b, t, 0)),
            pl.BlockSpec((sb, N_HEADS, DK, DV), lambda b, t: (b * nt + t, 0, 0, 0)),
            pl.BlockSpec((sb, N_HEADS, DV), blk),
            pl.BlockSpec((1, CONV_STATE, CONV_B, D_MODEL), lambda b, t: (0, 0, cblk(b, t), 0)),
            pl.BlockSpec((CONV_B, D_MODEL), lambda b, t: (cblk(b, t), 0)),
        ],
        out_shape=[
            jax.ShapeDtypeStruct((B, L, D_MODEL), F32),
            jax.ShapeDtypeStruct((nb, N_HEADS, DK, DV), F32),
            jax.ShapeDtypeStruct((nb, N_HEADS, DV), F32),
            jax.ShapeDtypeStruct((1, CONV_STATE, nb, D_MODEL), F32),
            jax.ShapeDtypeStruct((nb, D_MODEL), F32),
        ],
        compiler_params=pltpu.CompilerParams(
            dimension_semantics=("arbitrary", "arbitrary"),
            vmem_limit_bytes=56 * MIB),
        name="prompt_ffn_sample_ret",
    )(x1, mod_p, g_pre2, g_post2, wg_b, wu_b, wd_b, q_s, k_s, v_s, cos_s, sin_s, gam_tab, state,
      z_s, z_s, state_conv_t, conv_w, conv_b)


def _sample_in_kernel(x_ref, sh_ref, sc_ref, gpre_ref, w_ref, wro_f_ref, wco_f_ref, wo_f_ref,
                      z_ref, wp_ref, wro_p_ref, wco_p_ref, wo_p_ref):
    hb = (_rms(x_ref[...], gpre_ref[...]) * (1.0 + sc_ref[...]) + sh_ref[...]).astype(BF16)
    wb = w_ref[...].astype(BF16)
    z_ref[...] = _dot(hb, wb)
    wp_ref[...] = pltpu.bitcast(wb, jnp.uint32)
    _pack_slabs((wro_f_ref, wco_f_ref, wo_f_ref), (wro_p_ref, wco_p_ref, wo_p_ref))


def _sample_in(xs, mod, g_pre1, w_in, w_ret_out, w_conv_out, w_out):
    rows = xs.shape[0]
    tn = D_MODEL
    steps = D_IN // tn
    slabs = [w_ret_out.shape, w_conv_out.shape, w_out.shape]
    step_of = lambda j: j
    return pl.pallas_call(
        _sample_in_kernel,
        grid=(steps,),
        in_specs=[
            pl.BlockSpec((rows, D_MODEL), lambda j: (0, 0)),
            pl.BlockSpec((rows, D_MODEL), lambda j: (0, 0)),
            pl.BlockSpec((rows, D_MODEL), lambda j: (0, 1)),
            pl.BlockSpec((1, D_MODEL), lambda j: (0, 0)),
            pl.BlockSpec((D_MODEL, tn), lambda j: (0, j)),
        ] + [_slab_spec(r, c, steps, step_of, packed=False) for r, c in slabs],
        out_specs=[
            pl.BlockSpec((rows, tn), lambda j: (0, j)),
            pl.BlockSpec((D_MODEL // 2, tn), lambda j: (0, j)),
        ] + [_slab_spec(r, c, steps, step_of, packed=True) for r, c in slabs],
        out_shape=[
            jax.ShapeDtypeStruct((rows, D_IN), F32),
            jax.ShapeDtypeStruct((D_MODEL // 2, D_IN), jnp.uint32),
        ] + [jax.ShapeDtypeStruct((r // 2, c), jnp.uint32) for r, c in slabs],
        compiler_params=pltpu.CompilerParams(dimension_semantics=("arbitrary",)),
        name="sample_in_proj",
    )(xs, mod, mod, g_pre1, w_in, w_ret_out, w_conv_out, w_out)


def _sample_conv_step(u_ref, a_ref, st_ref, w_ref, b_ref, new_ref, conv_ref):
    glu = u_ref[...] * jax.nn.sigmoid(a_ref[...])
    acc = glu * w_ref[CONV_STATE:CONV_WIDTH, :] + b_ref[...]
    for j in range(CONV_STATE):
        acc = acc + st_ref[0, j] * w_ref[j:j + 1, :]
    conv_ref[...] = acc
    for j in range(CONV_STATE - 1):
        new_ref[0, j] = st_ref[0, j + 1]
    new_ref[0, CONV_STATE - 1] = glu


def _sample_post_kernel(x_ref, mod_ref, z_ref, y_ref, conv_ref, gng_ref, wro_ref, lng_ref, lnb_ref,
                        wco_ref, wo_ref, gpost1_ref, gpre2_ref, gpost2_ref, wg_ref, wu_ref, wd_ref, o_ref):
    x = x_ref[...]
    rows = x.shape[0]

    def mod(i):
        return mod_ref[0:rows, i * D_MODEL:(i + 1) * D_MODEL]

    ret_out = jnp.zeros((rows, D_MODEL), F32)
    for h in range(N_HEADS):
        sl = slice(h * DV, (h + 1) * DV)
        yn = _standardize(y_ref[:, sl]) * gng_ref[:, sl]
        gated = (jax.nn.silu(z_ref[:, OFF_G + h * DV:OFF_G + (h + 1) * DV]) * yn).astype(BF16)
        ret_out = ret_out + _dot(gated, _w(wro_ref[h * DV // 2:(h + 1) * DV // 2, :]))
    ln = _standardize(conv_ref[...]) * lng_ref[...] + lnb_ref[...]
    conv_out = _dot(jax.nn.silu(ln).astype(BF16), _w(wco_ref[...]))
    merged = (jax.nn.sigmoid(z_ref[:, OFF_GR:OFF_GR + D_MODEL]) * ret_out
              + jax.nn.sigmoid(z_ref[:, OFF_GC:OFF_GC + D_MODEL]) * conv_out)
    mix = _dot(merged.astype(BF16), _w(wo_ref[...]))
    x1 = x + mod(2) * _rms(mix, gpost1_ref[...])
    o_ref[...] = _ffn_body(x1, mod(3), mod(4), mod(5), gpre2_ref[...], gpost2_ref[...],
                           wg_ref, wu_ref, wd_ref)


def _sample_post(xs, mod_s, z_s, y_s, conv_s, gn_g, w_ret_out_b, ln_g, ln_b, w_conv_out_b, w_out_b,
                 g_post1, g_pre2, g_post2, wg_b, wu_b, wd_b):
    return pl.pallas_call(
        _sample_post_kernel,
        out_shape=jax.ShapeDtypeStruct(xs.shape, F32),
        compiler_params=pltpu.CompilerParams(vmem_limit_bytes=56 * MIB),
        name="sample_post",
    )(xs, mod_s, z_s, y_s, conv_s, gn_g, w_ret_out_b, ln_g, ln_b, w_conv_out_b, w_out_b,
      g_post1, g_pre2, g_post2, wg_b, wu_b, wd_b)


def _rope_tables(pos):
    inv_freq = ROPE_BASE ** (-np.arange(0, DK, 2, dtype=np.float64) / DK)
    ang = np.asarray(pos, np.float64)[:, None] * inv_freq[None, :]
    return jnp.asarray(np.cos(ang), F32), jnp.asarray(np.sin(ang), F32)


def _log_gamma():
    return np.log(1.0 - np.exp(np.linspace(np.log(1.0 / 32.0), np.log(1.0 / 512.0), N_HEADS)))


def _decay_tables(chunk):
    lg = _log_gamma()
    idx = np.arange(chunk, dtype=np.float64)
    diff = idx[:, None] - idx[None, :]
    decay = np.where(diff >= 0, np.exp(lg[:, None, None] * np.maximum(diff, 0.0)), 0.0)
    cross = np.exp(lg[None, :] * (idx[:, None] + 1.0))
    k_dec = np.exp(lg[None, :] * (chunk - 1.0 - idx[:, None]))
    full = np.broadcast_to(np.exp(lg * chunk)[None, :], (chunk, N_HEADS))
    rs = np.concatenate([cross, k_dec, full], axis=1)
    rs = np.pad(rs, ((0, 0), (0, LANES - rs.shape[1])))
    return jnp.asarray(decay, F32), jnp.asarray(rs, F32)


def kernel(x_prompt, x_sample, c_prompt, c_sample, state_ret, state_conv, w_in, w_ada, b_ada, g_pre1, g_post1, g_pre2, g_post2, conv_w, conv_b, conv_ln_g, conv_ln_b, w_conv_out, ret_gn_g, w_ret_out, w_out, w_ffn_gate, w_ffn_up, w_ffn_down):
    depth = w_in.shape[0]
    assert depth == 1, "single-layer step"
    B, L, _ = x_prompt.shape
    nb = x_sample.shape[0]

    cos_p, sin_p = _rope_tables(np.arange(L))
    cos_s, sin_s = _rope_tables(PAST_LEN + np.arange(x_sample.shape[1]))
    decay, rs_tab = _decay_tables(CHUNK)
    gam_tab = jnp.asarray(np.broadcast_to(np.exp(_log_gamma())[:, None], (N_HEADS, DV)), F32)

    mod, mod_p = _modulation(c_sample, c_prompt, w_ada[0], b_ada)
    mod_p = mod_p.reshape(B, 1, 6 * D_MODEL)

    xs = x_sample.reshape(nb, D_MODEL)
    z_s, w_in_b, w_ret_out_b, w_conv_out_b, w_out_b = _sample_in(
        xs, mod, g_pre1, w_in[0], w_ret_out[0], w_conv_out[0], w_out[0])
    q_s = z_s[:, OFF_Q:OFF_Q + RET_QK].reshape(nb, N_HEADS, DK)
    k_s = z_s[:, OFF_K:OFF_K + RET_QK].reshape(nb, N_HEADS, DK)
    v_s = z_s[:, OFF_V:OFF_V + RET_V].reshape(nb, N_HEADS, DV)

    x1_p, ret_p, conv_p, wg_b, wu_b, wd_b = _prompt_mixer(
        x_prompt, mod_p, g_pre1, g_post1, cos_p, sin_p, decay, rs_tab, w_in_b, conv_w[0], conv_b,
        conv_ln_g, conv_ln_b, w_conv_out_b, ret_gn_g, w_ret_out_b, w_out_b,
        w_ffn_gate[0], w_ffn_up[0], w_ffn_down[0])
    y_p, ret_s, yr_s, conv_new_t, conv_s = _prompt_ffn_sample_ret(
        x1_p, mod_p, g_pre2, g_post2, wg_b, wu_b, wd_b, q_s, k_s, v_s, cos_s, sin_s, gam_tab, state_ret[0],
        z_s, state_conv.transpose(0, 2, 1, 3), conv_w[0], conv_b)
    conv_new_s = conv_new_t.transpose(0, 2, 1, 3)
    y_s = _sample_post(xs, mod, z_s, yr_s.reshape(nb, RET_V), conv_s, ret_gn_g, w_ret_out_b,
                       conv_ln_g, conv_ln_b, w_conv_out_b, w_out_b, g_post1, g_pre2, g_post2,
                       wg_b, wu_b, wd_b)

    return (y_p, y_s.reshape(x_sample.shape), ret_p[None], ret_s[None], conv_p[None], conv_new_s)
```

```python
import functools

import numpy as np

import jax
import jax.numpy as jnp
from jax import lax
from jax.experimental import pallas as pl
from jax.experimental.pallas import tpu as pltpu

F32 = jnp.float32
BF16 = jnp.bfloat16

D_MODEL = 1024
N_HEADS = 4
DK = 256
DV = 512
HALF = DK // 2
RET_QK = N_HEADS * DK
RET_V = N_HEADS * DV
CONV_WIDTH = 31
CONV_STATE = CONV_WIDTH - 1
EPS = 1e-6
ROPE_BASE = 10000.0
PAST_LEN = 16384

OFF_Q = 0
OFF_K = RET_QK
OFF_V = 2 * RET_QK
OFF_G = OFF_V + RET_V
OFF_U = OFF_G + RET_V
OFF_A = OFF_U + D_MODEL
OFF_GR = OFF_A + D_MODEL
OFF_GC = OFF_GR + D_MODEL
D_IN = OFF_GC + D_MODEL

LANES = 128
BF16_TILE_ROWS = 16
T_MIX = 256
CHUNK = 256
HIST = 32
ROW_CHUNK = 64
T_FFN = 256
CONV_B = 8
MIB = 1024 * 1024


def _rms(x, g):
    ms = jnp.mean(x * x, axis=-1, keepdims=True)
    return x * lax.rsqrt(ms + EPS) * g


def _rotary(t, cos, sin):
    t1, t2 = t[:, :HALF], t[:, HALF:]
    return jnp.concatenate([t1 * cos - t2 * sin, t2 * cos + t1 * sin], axis=-1)


def _standardize(y):
    mu = jnp.mean(y, axis=-1, keepdims=True)
    yc = y - mu
    var = jnp.mean(yc * yc, axis=-1, keepdims=True)
    return yc * lax.rsqrt(var + EPS)


def _dot(a, b):
    return jnp.dot(a, b, preferred_element_type=F32)


def _w(packed):
    return pltpu.bitcast(packed, BF16)


def _const_spec(shape):
    return pl.BlockSpec(shape, lambda *_: (0,) * len(shape), pipeline_mode=pl.Buffered(1))


def _mod_kernel(cs_ref, cp_ref, w_ref, b_ref, os_ref, op_ref):
    wb = w_ref[...].astype(BF16)
    for c_ref, o_ref in ((cs_ref, os_ref), (cp_ref, op_ref)):
        o_ref[...] = _dot(jax.nn.silu(c_ref[...]).astype(BF16), wb) + b_ref[...]


def _modulation(c_sample, c_prompt, w_ada, b_ada):
    n_out = w_ada.shape[1]
    tn = D_MODEL
    groups = (c_sample.shape[0], c_prompt.shape[0])
    return pl.pallas_call(
        _mod_kernel,
        grid=(n_out // tn,),
        in_specs=[pl.BlockSpec((rows, D_MODEL), lambda j: (0, 0)) for rows in groups] + [
            pl.BlockSpec((D_MODEL, tn), lambda j: (0, j)),
            pl.BlockSpec((1, tn), lambda j: (0, j)),
        ],
        out_specs=[pl.BlockSpec((rows, tn), lambda j: (0, j)) for rows in groups],
        out_shape=[jax.ShapeDtypeStruct((rows, n_out), F32) for rows in groups],
        compiler_params=pltpu.CompilerParams(dimension_semantics=("arbitrary",)),
        name="adaln_mod",
    )(c_sample, c_prompt, w_ada, b_ada)


def _mixer_kernel(x_ref, mod_ref, gpre_ref, gpost_ref, cos_ref, sin_ref, decay_ref, rs_ref,
                  w_in_ref, convw_ref, convb_ref, lng_ref, lnb_ref, wco_ref, gng_ref, wro_ref, wo_ref,
                  wg_f_ref, wu_f_ref, wd_f_ref,
                  x1_ref, rstate_ref, cstate_ref, wg_p_ref, wu_p_ref, wd_p_ref,
                  full_ref, conv_ref, gated_ref):
    t = pl.program_id(1)
    T = T_MIX
    n_col = D_MODEL // LANES

    _pack_slabs((wg_f_ref, wu_f_ref, wd_f_ref), (wg_p_ref, wu_p_ref, wd_p_ref))

    @pl.when(t == 0)
    def _():
        rstate_ref[...] = jnp.zeros_like(rstate_ref)
        full_ref[:, 0:HIST, :] = jnp.zeros((n_col, HIST, LANES), F32)

    mod = mod_ref[0]
    sh1 = mod[:, 0:D_MODEL]
    sc1 = mod[:, D_MODEL:2 * D_MODEL]
    gt1 = mod[:, 2 * D_MODEL:3 * D_MODEL]
    hb = (_rms(x_ref[0], gpre_ref[...]) * (1.0 + sc1) + sh1).astype(BF16)

    def proj(lo, width):
        return _dot(hb, _w(w_in_ref[:, lo:lo + width]))

    glu = proj(OFF_U, D_MODEL) * jax.nn.sigmoid(proj(OFF_A, D_MODEL))
    for c in range(n_col):
        full_ref[c, HIST:HIST + T, :] = glu[:, c * LANES:(c + 1) * LANES]

    @pl.when(t == pl.num_programs(1) - 1)
    def _():
        for c in range(n_col):
            cstate_ref[0, :, c * LANES:(c + 1) * LANES] = full_ref[c, HIST + T - CONV_STATE:HIST + T, :]

    def conv_block(c):
        lanes = slice(c * LANES, (c + 1) * LANES)
        for r0 in range(0, T, ROW_CHUNK):
            acc = jnp.broadcast_to(convb_ref[:, lanes], (ROW_CHUNK, LANES))
            for j in range(CONV_WIDTH):
                start = HIST - CONV_STATE + j + r0
                acc = acc + full_ref[c, start:start + ROW_CHUNK, :] * convw_ref[j:j + 1, lanes]
            conv_ref[c, r0:r0 + ROW_CHUNK, :] = acc

    assert T == CHUNK
    zq = proj(OFF_Q, RET_QK)
    zk = proj(OFF_K, RET_QK)
    zv = proj(OFF_V, RET_V).astype(BF16)
    hq = lambda h: slice(h * DK, (h + 1) * DK)
    hv = lambda h: slice(h * DV, (h + 1) * DV)
    qb, kb, kd = [], [], []
    for h in range(N_HEADS):
        q = _rotary(zq[:, hq(h)], cos_ref[...], sin_ref[...]) * (DK ** -0.5)
        k = _rotary(zk[:, hq(h)], cos_ref[...], sin_ref[...])
        qb.append(q.astype(BF16))
        kb.append(k.astype(BF16))
        kd.append((k * rs_ref[:, N_HEADS + h:N_HEADS + h + 1]).astype(BF16))
    sc = [(lax.dot_general(qb[h], kb[h], (((1,), (1,)), ((), ())), preferred_element_type=F32)
           * decay_ref[h]).astype(BF16) for h in range(N_HEADS)]
    for c in range(0, n_col // 2):
        conv_block(c)
    ys = []
    for h in range(N_HEADS):
        s_prev = rstate_ref[0, h]
        y = _dot(sc[h], zv[:, hv(h)])
        y = y + _dot(qb[h], s_prev.astype(BF16)) * rs_ref[:, h:h + 1]
        upd = lax.dot_general(kd[h], zv[:, hv(h)], (((0,), (0,)), ((), ())), preferred_element_type=F32)
        rstate_ref[0, h] = rs_ref[0:1, 2 * N_HEADS + h:2 * N_HEADS + h + 1] * s_prev + upd
        ys.append(y)
    zg = proj(OFF_G, RET_V)
    for c in range(n_col // 2, n_col):
        conv_block(c)
    for h in range(N_HEADS):
        yn = _standardize(ys[h]) * gng_ref[:, hv(h)]
        gated_ref[:, hv(h)] = (jax.nn.silu(zg[:, hv(h)]) * yn).astype(BF16)

    full_ref[:, 0:HIST, :] = full_ref[:, T:T + HIST, :]
    conv = jnp.concatenate([conv_ref[c] for c in range(n_col)], axis=-1)
    ln = _standardize(conv) * lng_ref[...] + lnb_ref[...]
    conv_out = _dot(jax.nn.silu(ln).astype(BF16), _w(wco_ref[...]))
    ret_out = _dot(gated_ref[...], _w(wro_ref[...]))

    merged = (jax.nn.sigmoid(proj(OFF_GR, D_MODEL)) * ret_out
              + jax.nn.sigmoid(proj(OFF_GC, D_MODEL)) * conv_out)
    mix = _dot(merged.astype(BF16), _w(wo_ref[...]))
    x1_ref[0] = x_ref[0] + gt1 * _rms(mix, gpost_ref[...])


def _slab_spec(total_rows, cols, steps, step_of, packed):
    rows = next(r for r in range(BF16_TILE_ROWS, total_rows + 1, BF16_TILE_ROWS)
                if total_rows % r == 0 and total_rows // r <= steps)
    last = total_rows // rows - 1
    return pl.BlockSpec((rows // 2 if packed else rows, cols),
                        lambda *idx: (jnp.minimum(step_of(*idx), last), 0))


def _pack_slabs(srcs, dsts):
    for src, dst in zip(srcs, dsts):
        dst[...] = pltpu.bitcast(src[...].astype(BF16), jnp.uint32)


def _prompt_mixer(x, mod_p, g_pre1, g_post1, cos_p, sin_p, decay, rs_tab, w_in_b, conv_w, conv_b,
                  ln_g, ln_b, w_conv_out_b, gn_g, w_ret_out_b, w_out_b, w_gate, w_up, w_down):
    B, L, _ = x.shape
    T = T_MIX
    nt = L // T
    d_ff = w_gate.shape[1]
    row = lambda b, t: (0, 0)
    step_of = lambda b, t: b * nt + t
    slabs = [(D_MODEL, d_ff), (D_MODEL, d_ff), (d_ff, D_MODEL)]
    in_specs = [
        pl.BlockSpec((1, T, D_MODEL), lambda b, t: (b, t, 0)),
        pl.BlockSpec((1, 1, 6 * D_MODEL), lambda b, t: (b, 0, 0)),
        pl.BlockSpec((1, D_MODEL), row),
        pl.BlockSpec((1, D_MODEL), row),
        pl.BlockSpec((T, HALF), lambda b, t: (t, 0)),
        pl.BlockSpec((T, HALF), lambda b, t: (t, 0)),
        _const_spec((N_HEADS, CHUNK, CHUNK)),
        _const_spec((CHUNK, LANES)),
        _const_spec((D_MODEL // 2, D_IN)),
        pl.BlockSpec((CONV_WIDTH, D_MODEL), row),
        pl.BlockSpec((1, D_MODEL), row),
        pl.BlockSpec((1, D_MODEL), row),
        pl.BlockSpec((1, D_MODEL), row),
        _const_spec((D_MODEL // 2, D_MODEL)),
        pl.BlockSpec((1, RET_V), row),
        _const_spec((RET_V // 2, D_MODEL)),
        _const_spec((D_MODEL // 2, D_MODEL)),
    ] + [_slab_spec(r, c, B * nt, step_of, packed=False) for r, c in slabs]
    out_specs = [
        pl.BlockSpec((1, T, D_MODEL), lambda b, t: (b, t, 0)),
        pl.BlockSpec((1, N_HEADS, DK, DV), lambda b, t: (b, 0, 0, 0)),
        pl.BlockSpec((1, CONV_STATE, D_MODEL), lambda b, t: (b, 0, 0)),
    ] + [_slab_spec(r, c, B * nt, step_of, packed=True) for r, c in slabs]
    out_shape = [
        jax.ShapeDtypeStruct((B, L, D_MODEL), F32),
        jax.ShapeDtypeStruct((B, N_HEADS, DK, DV), F32),
        jax.ShapeDtypeStruct((B, CONV_STATE, D_MODEL), F32),
    ] + [jax.ShapeDtypeStruct((r // 2, c), jnp.uint32) for r, c in slabs]
    return pl.pallas_call(
        _mixer_kernel,
        grid=(B, L // T),
        in_specs=in_specs,
        out_specs=out_specs,
        out_shape=out_shape,
        scratch_shapes=[
            pltpu.VMEM((D_MODEL // LANES, HIST + T, LANES), F32),
            pltpu.VMEM((D_MODEL // LANES, T, LANES), F32),
            pltpu.VMEM((T, RET_V), BF16),
        ],
        compiler_params=pltpu.CompilerParams(
            dimension_semantics=("arbitrary", "arbitrary"),
            vmem_limit_bytes=56 * MIB),
        name="prompt_mixer",
    )(x, mod_p, g_pre1, g_post1, cos_p, sin_p, decay, rs_tab, w_in_b, conv_w, conv_b,
      ln_g, ln_b, w_conv_out_b, gn_g, w_ret_out_b, w_out_b, w_gate, w_up, w_down)


def _ffn_body(x, sh2, sc2, gt2, gpre, gpost, wg_ref, wu_ref, wd_ref):
    hb = (_rms(x, gpre) * (1.0 + sc2) + sh2).astype(BF16)
    act = (jax.nn.silu(_dot(hb, _w(wg_ref[...]))) * _dot(hb, _w(wu_ref[...]))).astype(BF16)
    return x + gt2 * _rms(_dot(act, _w(wd_ref[...])), gpost)


def _sample_ret_step(q, k, v, cos, sin, gam_ref, s0_ref, snew_ref, y_ref, i):
    q = _rotary(q, cos, sin) * (DK ** -0.5)
    k = _rotary(k, cos, sin)
    qk = jnp.sum(q * k, axis=-1, keepdims=True)
    cols = jnp.concatenate([k, q, jnp.zeros((LANES - 2 * N_HEADS, DK), F32)], axis=0).T
    for h in range(N_HEADS):
        s0 = s0_ref[i, h]
        v_h = v[h:h + 1, :]
        gam = gam_ref[h:h + 1, :]
        snew_ref[i, h] = gam * s0 + cols[:, h:h + 1] * v_h
        qs = jnp.sum(cols[:, N_HEADS + h:N_HEADS + h + 1] * s0, axis=0, keepdims=True)
        y_ref[i, h:h + 1, :] = qk[h:h + 1, :] * v_h + gam * qs


def _ffn_ret_kernel(x_ref, mod_ref, gpre_ref, gpost_ref, wg_ref, wu_ref, wd_ref,
                    q_ref, k_ref, v_ref, cos_ref, sin_ref, gam_ref, s0_ref,
                    u_ref, a_ref, cst_ref, convw_ref, convb_ref,
                    o_ref, snew_ref, y_ref, cnew_ref, conv_ref, *, conv_steps):
    step = pl.program_id(0) * pl.num_programs(1) + pl.program_id(1)

    @pl.when(step < conv_steps)
    def _():
        _sample_conv_step(u_ref, a_ref, cst_ref, convw_ref, convb_ref, cnew_ref, conv_ref)

    mod = mod_ref[0]
    o_ref[0] = _ffn_body(x_ref[0], mod[:, 3 * D_MODEL:4 * D_MODEL], mod[:, 4 * D_MODEL:5 * D_MODEL],
                         mod[:, 5 * D_MODEL:6 * D_MODEL], gpre_ref[...], gpost_ref[...],
                         wg_ref, wu_ref, wd_ref)
    for i in range(q_ref.shape[0]):
        _sample_ret_step(q_ref[i], k_ref[i], v_ref[i], cos_ref[...], sin_ref[...], gam_ref,
                         s0_ref, snew_ref, y_ref, i)


def _prompt_ffn_sample_ret(x1, mod_p, g_pre2, g_post2, wg_b, wu_b, wd_b,
                           q_s, k_s, v_s, cos_s, sin_s, gam_tab, state,
                           z_s, state_conv_t, conv_w, conv_b):
    B, L, _ = x1.shape
    nb = q_s.shape[0]
    d_ff = wg_b.shape[1]
    nt = L // T_FFN
    steps = B * nt
    sb = nb // steps
    assert sb * steps == nb
    conv_steps = nb // CONV_B
    assert conv_steps * CONV_B == nb and conv_steps <= steps
    row = lambda b, t: (0, 0)
    blk = lambda b, t: (b * nt + t, 0, 0)
    cblk = lambda b, t: jnp.minimum(b * nt + t, conv_steps - 1)
    return pl.pallas_call(
        functools.partial(_ffn_ret_kernel, conv_steps=conv_steps),
        grid=(B, nt),
        in_specs=[
            pl.BlockSpec((1, T_FFN, D_MODEL), lambda b, t: (b, t, 0)),
            pl.BlockSpec((1, 1, 6 * D_MODEL), lambda b, t: (b, 0, 0)),
            pl.BlockSpec((1, D_MODEL), row),
            pl.BlockSpec((1, D_MODEL), row),
            _const_spec((D_MODEL // 2, d_ff)),
            _const_spec((D_MODEL // 2, d_ff)),
            _const_spec((d_ff // 2, D_MODEL)),
            pl.BlockSpec((sb, N_HEADS, DK), blk),
            pl.BlockSpec((sb, N_HEADS, DK), blk),
            pl.BlockSpec((sb, N_HEADS, DV), blk),
            pl.BlockSpec((1, HALF), row),
            pl.BlockSpec((1, HALF), row),
            pl.BlockSpec((N_HEADS, DV), row),
            pl.BlockSpec((sb, N_HEADS, DK, DV), lambda b, t: (b * nt + t, 0, 0, 0)),
            pl.BlockSpec((CONV_B, D_MODEL), lambda b, t: (cblk(b, t), OFF_U // D_MODEL)),
            pl.BlockSpec((CONV_B, D_MODEL), lambda b, t: (cblk(b, t), OFF_A // D_MODEL)),
            pl.BlockSpec((1, CONV_STATE, CONV_B, D_MODEL), lambda b, t: (0, 0, cblk(b, t), 0)),
            pl.BlockSpec((CONV_WIDTH, D_MODEL), row),
            pl.BlockSpec((1, D_MODEL), row),
        ],
        out_specs=[
            pl.BlockSpec((1, T_FFN, D_MODEL), lambda b, t: (b, t, 0)),
            pl.BlockSpec((sb, N_HEADS, DK, DV), lambda b, t: (b * nt + t, 0, 0, 0)),
            pl.BlockSpec((sb, N_HEADS, DV), blk),
            pl.BlockSpec((1, CONV_STATE, CONV_B, D_MODEL), lambda b, t: (0, 0, cblk(b, t), 0)),
            pl.BlockSpec((CONV_B, D_MODEL), lambda b, t: (cblk(b, t), 0)),
        ],
        out_shape=[
            jax.ShapeDtypeStruct((B, L, D_MODEL), F32),
            jax.ShapeDtypeStruct((nb, N_HEADS, DK, DV), F32),
            jax.ShapeDtypeStruct((nb, N_HEADS, DV), F32),
            jax.ShapeDtypeStruct((1, CONV_STATE, nb, D_MODEL), F32),
            jax.ShapeDtypeStruct((nb, D_MODEL), F32),
        ],
        compiler_params=pltpu.CompilerParams(
            dimension_semantics=("arbitrary", "arbitrary"),
            vmem_limit_bytes=56 * MIB),
        name="prompt_ffn_sample_ret",
    )(x1, mod_p, g_pre2, g_post2, wg_b, wu_b, wd_b, q_s, k_s, v_s, cos_s, sin_s, gam_tab, state,
      z_s, z_s, state_conv_t, conv_w, conv_b)


def _sample_in_kernel(x_ref, sh_ref, sc_ref, gpre_ref, w_ref, wro_f_ref, wco_f_ref, wo_f_ref,
                      z_ref, wp_ref, wro_p_ref, wco_p_ref, wo_p_ref):
    hb = (_rms(x_ref[...], gpre_ref[...]) * (1.0 + sc_ref[...]) + sh_ref[...]).astype(BF16)
    wb = w_ref[...].astype(BF16)
    z_ref[...] = _dot(hb, wb)
    wp_ref[...] = pltpu.bitcast(wb, jnp.uint32)
    _pack_slabs((wro_f_ref, wco_f_ref, wo_f_ref), (wro_p_ref, wco_p_ref, wo_p_ref))


def _sample_in(xs, mod, g_pre1, w_in, w_ret_out, w_conv_out, w_out):
    rows = xs.shape[0]
    tn = D_MODEL
    steps = D_IN // tn
    slabs = [w_ret_out.shape, w_conv_out.shape, w_out.shape]
    step_of = lambda j: j
    return pl.pallas_call(
        _sample_in_kernel,
        grid=(steps,),
        in_specs=[
            pl.BlockSpec((rows, D_MODEL), lambda j: (0, 0)),
            pl.BlockSpec((rows, D_MODEL), lambda j: (0, 0)),
            pl.BlockSpec((rows, D_MODEL), lambda j: (0, 1)),
            pl.BlockSpec((1, D_MODEL), lambda j: (0, 0)),
            pl.BlockSpec((D_MODEL, tn), lambda j: (0, j)),
        ] + [_slab_spec(r, c, steps, step_of, packed=False) for r, c in slabs],
        out_specs=[
            pl.BlockSpec((rows, tn), lambda j: (0, j)),
            pl.BlockSpec((D_MODEL // 2, tn), lambda j: (0, j)),
        ] + [_slab_spec(r, c, steps, step_of, packed=True) for r, c in slabs],
        out_shape=[
            jax.ShapeDtypeStruct((rows, D_IN), F32),
            jax.ShapeDtypeStruct((D_MODEL // 2, D_IN), jnp.uint32),
        ] + [jax.ShapeDtypeStruct((r // 2, c), jnp.uint32) for r, c in slabs],
        compiler_params=pltpu.CompilerParams(dimension_semantics=("arbitrary",)),
        name="sample_in_proj",
    )(xs, mod, mod, g_pre1, w_in, w_ret_out, w_conv_out, w_out)


def _sample_conv_step(u_ref, a_ref, st_ref, w_ref, b_ref, new_ref, conv_ref):
    glu = u_ref[...] * jax.nn.sigmoid(a_ref[...])
    acc = glu * w_ref[CONV_STATE:CONV_WIDTH, :] + b_ref[...]
    for j in range(CONV_STATE):
        acc = acc + st_ref[0, j] * w_ref[j:j + 1, :]
    conv_ref[...] = acc
    for j in range(CONV_STATE - 1):
        new_ref[0, j] = st_ref[0, j + 1]
    new_ref[0, CONV_STATE - 1] = glu


def _sample_post_kernel(x_ref, mod_ref, z_ref, y_ref, conv_ref, gng_ref, wro_ref, lng_ref, lnb_ref,
                        wco_ref, wo_ref, gpost1_ref, gpre2_ref, gpost2_ref, wg_ref, wu_ref, wd_ref, o_ref):
    x = x_ref[...]
    rows = x.shape[0]

    def mod(i):
        return mod_ref[0:rows, i * D_MODEL:(i + 1) * D_MODEL]

    ret_out = jnp.zeros((rows, D_MODEL), F32)
    for h in range(N_HEADS):
        sl = slice(h * DV, (h + 1) * DV)
        yn = _standardize(y_ref[:, sl]) * gng_ref[:, sl]
        gated = (jax.nn.silu(z_ref[:, OFF_G + h * DV:OFF_G + (h + 1) * DV]) * yn).astype(BF16)
        ret_out = ret_out + _dot(gated, _w(wro_ref[h * DV // 2:(h + 1) * DV // 2, :]))
    ln = _standardize(conv_ref[...]) * lng_ref[...] + lnb_ref[...]
    conv_out = _dot(jax.nn.silu(ln).astype(BF16), _w(wco_ref[...]))
    merged = (jax.nn.sigmoid(z_ref[:, OFF_GR:OFF_GR + D_MODEL]) * ret_out
              + jax.nn.sigmoid(z_ref[:, OFF_GC:OFF_GC + D_MODEL]) * conv_out)
    mix = _dot(merged.astype(BF16), _w(wo_ref[...]))
    x1 = x + mod(2) * _rms(mix, gpost1_ref[...])
    o_ref[...] = _ffn_body(x1, mod(3), mod(4), mod(5), gpre2_ref[...], gpost2_ref[...],
                           wg_ref, wu_ref, wd_ref)


def _sample_post(xs, mod_s, z_s, y_s, conv_s, gn_g, w_ret_out_b, ln_g, ln_b, w_conv_out_b, w_out_b,
                 g_post1, g_pre2, g_post2, wg_b, wu_b, wd_b):
    return pl.pallas_call(
        _sample_post_kernel,
        out_shape=jax.ShapeDtypeStruct(xs.shape, F32),
        compiler_params=pltpu.CompilerParams(vmem_limit_bytes=56 * MIB),
        name="sample_post",
    )(xs, mod_s, z_s, y_s, conv_s, gn_g, w_ret_out_b, ln_g, ln_b, w_conv_out_b, w_out_b,
      g_post1, g_pre2, g_post2, wg_b, wu_b, wd_b)


def _rope_tables(pos):
    inv_freq = ROPE_BASE ** (-np.arange(0, DK, 2, dtype=np.float64) / DK)
    ang = np.asarray(pos, np.float64)[:, None] * inv_freq[None, :]
    return jnp.asarray(np.cos(ang), F32), jnp.asarray(np.sin(ang), F32)


def _log_gamma():
    return np.log(1.0 - np.exp(np.linspace(np.log(1.0 / 32.0), np.log(1.0 / 512.0), N_HEADS)))


def _decay_tables(chunk):
    lg = _log_gamma()
    idx = np.arange(chunk, dtype=np.float64)
    diff = idx[:, None] - idx[None, :]
    decay = np.where(diff >= 0, np.exp(lg[:, None, None] * np.maximum(diff, 0.0)), 0.0)
    cross = np.exp(lg[None, :] * (idx[:, None] + 1.0))
    k_dec = np.exp(lg[None, :] * (chunk - 1.0 - idx[:, None]))
    full = np.broadcast_to(np.exp(lg * chunk)[None, :], (chunk, N_HEADS))
    rs = np.concatenate([cross, k_dec, full], axis=1)
    rs = np.pad(rs, ((0, 0), (0, LANES - rs.shape[1])))
    return jnp.asarray(decay, F32), jnp.asarray(rs, F32)


def kernel(x_prompt, x_sample, c_prompt, c_sample, state_ret, state_conv, w_in, w_ada, b_ada, g_pre1, g_post1, g_pre2, g_post2, conv_w, conv_b, conv_ln_g, conv_ln_b, w_conv_out, ret_gn_g, w_ret_out, w_out, w_ffn_gate, w_ffn_up, w_ffn_down):
    depth = w_in.shape[0]
    assert depth == 1, "single-layer step"
    B, L, _ = x_prompt.shape
    nb = x_sample.shape[0]

    cos_p, sin_p = _rope_tables(np.arange(L))
    cos_s, sin_s = _rope_tables(PAST_LEN + np.arange(x_sample.shape[1]))
    decay, rs_tab = _decay_tables(CHUNK)
    gam_tab = jnp.asarray(np.broadcast_to(np.exp(_log_gamma())[:, None], (N_HEADS, DV)), F32)

    mod, mod_p = _modulation(c_sample, c_prompt, w_ada[0], b_ada)
    mod_p = mod_p.reshape(B, 1, 6 * D_MODEL)

    xs = x_sample.reshape(nb, D_MODEL)
    z_s, w_in_b, w_ret_out_b, w_conv_out_b, w_out_b = _sample_in(
        xs, mod, g_pre1, w_in[0], w_ret_out[0], w_conv_out[0], w_out[0])
    q_s = z_s[:, OFF_Q:OFF_Q + RET_QK].reshape(nb, N_HEADS, DK)
    k_s = z_s[:, OFF_K:OFF_K + RET_QK].reshape(nb, N_HEADS, DK)
    v_s = z_s[:, OFF_V:OFF_V + RET_V].reshape(nb, N_HEADS, DV)

    x1_p, ret_p, conv_p, wg_b, wu_b, wd_b = _prompt_mixer(
        x_prompt, mod_p, g_pre1, g_post1, cos_p, sin_p, decay, rs_tab, w_in_b, conv_w[0], conv_b,
        conv_ln_g, conv_ln_b, w_conv_out_b, ret_gn_g, w_ret_out_b, w_out_b,
        w_ffn_gate[0], w_ffn_up[0], w_ffn_down[0])
    y_p, ret_s, yr_s, conv_new_t, conv_s = _prompt_ffn_sample_ret(
        x1_p, mod_p, g_pre2, g_post2, wg_b, wu_b, wd_b, q_s, k_s, v_s, cos_s, sin_s, gam_tab, state_ret[0],
        z_s, state_conv.transpose(0, 2, 1, 3), conv_w[0], conv_b)
    conv_new_s = conv_new_t.transpose(0, 2, 1, 3)
    y_s = _sample_post(xs, mod, z_s, yr_s.reshape(nb, RET_V), conv_s, ret_gn_g, w_ret_out_b,
                       conv_ln_g, conv_ln_b, w_conv_out_b, w_out_b, g_post1, g_pre2, g_post2,
                       wg_b, wu_b, wd_b)

    return (y_p, y_s.reshape(x_sample.shape), ret_p[None], ret_s[None], conv_p[None], conv_new_s)
```

```python
import functools

import numpy as np

import jax
import jax.numpy as jnp
from jax import lax
from jax.experimental import pallas as pl
from jax.experimental.pallas import tpu as pltpu

F32 = jnp.float32
BF16 = jnp.bfloat16

D_MODEL = 1024
N_HEADS = 4
DK = 256
DV = 512
HALF = DK // 2
RET_QK = N_HEADS * DK
RET_V = N_HEADS * DV
CONV_WIDTH = 31
CONV_STATE = CONV_WIDTH - 1
EPS = 1e-6
ROPE_BASE = 10000.0
PAST_LEN = 16384

OFF_Q = 0
OFF_K = RET_QK
OFF_V = 2 * RET_QK
OFF_G = OFF_V + RET_V
OFF_U = OFF_G + RET_V
OFF_A = OFF_U + D_MODEL
OFF_GR = OFF_A + D_MODEL
OFF_GC = OFF_GR + D_MODEL
D_IN = OFF_GC + D_MODEL

LANES = 128
BF16_TILE_ROWS = 16
T_MIX = 256
CHUNK = 256
HIST = 32
ROW_CHUNK = 64
T_FFN = 256
CONV_B = 8
MIB = 1024 * 1024


def _rms(x, g):
    ms = jnp.mean(x * x, axis=-1, keepdims=True)
    return x * lax.rsqrt(ms + EPS) * g


def _rotary(t, cos, sin):
    t1, t2 = t[:, :HALF], t[:, HALF:]
    return jnp.concatenate([t1 * cos - t2 * sin, t2 * cos + t1 * sin], axis=-1)


def _standardize(y):
    mu = jnp.mean(y, axis=-1, keepdims=True)
    yc = y - mu
    var = jnp.mean(yc * yc, axis=-1, keepdims=True)
    return yc * lax.rsqrt(var + EPS)


def _dot(a, b):
    return jnp.dot(a, b, preferred_element_type=F32)


def _w(packed):
    return pltpu.bitcast(packed, BF16)


def _const_spec(shape):
    return pl.BlockSpec(shape, lambda *_: (0,) * len(shape), pipeline_mode=pl.Buffered(1))


def _mod_kernel(cs_ref, cp_ref, w_ref, b_ref, os_ref, op_ref):
    wb = w_ref[...].astype(BF16)
    for c_ref, o_ref in ((cs_ref, os_ref), (cp_ref, op_ref)):
        o_ref[...] = _dot(jax.nn.silu(c_ref[...]).astype(BF16), wb) + b_ref[...]


def _modulation(c_sample, c_prompt, w_ada, b_ada):
    n_out = w_ada.shape[1]
    tn = D_MODEL
    groups = (c_sample.shape[0], c_prompt.shape[0])
    return pl.pallas_call(
        _mod_kernel,
        grid=(n_out // tn,),
        in_specs=[pl.BlockSpec((rows, D_MODEL), lambda j: (0, 0)) for rows in groups] + [
            pl.BlockSpec((D_MODEL, tn), lambda j: (0, j)),
            pl.BlockSpec((1, tn), lambda j: (0, j)),
        ],
        out_specs=[pl.BlockSpec((rows, tn), lambda j: (0, j)) for rows in groups],
        out_shape=[jax.ShapeDtypeStruct((rows, n_out), F32) for rows in groups],
        compiler_params=pltpu.CompilerParams(dimension_semantics=("arbitrary",)),
        name="adaln_mod",
    )(c_sample, c_prompt, w_ada, b_ada)


def _mixer_kernel(x_ref, mod_ref, gpre_ref, gpost_ref, cos_ref, sin_ref, decay_ref, rs_ref,
                  w_in_ref, convw_ref, convb_ref, lng_ref, lnb_ref, wco_ref, gng_ref, wro_ref, wo_ref,
                  wg_f_ref, wu_f_ref, wd_f_ref,
                  x1_ref, rstate_ref, cstate_ref, wg_p_ref, wu_p_ref, wd_p_ref,
                  full_ref, conv_ref, gated_ref):
    t = pl.program_id(1)
    T = T_MIX
    n_col = D_MODEL // LANES

    _pack_slabs((wg_f_ref, wu_f_ref, wd_f_ref), (wg_p_ref, wu_p_ref, wd_p_ref))

    @pl.when(t == 0)
    def _():
        rstate_ref[...] = jnp.zeros_like(rstate_ref)
        full_ref[:, 0:HIST, :] = jnp.zeros((n_col, HIST, LANES), F32)

    mod = mod_ref[0]
    sh1 = mod[:, 0:D_MODEL]
    sc1 = mod[:, D_MODEL:2 * D_MODEL]
    gt1 = mod[:, 2 * D_MODEL:3 * D_MODEL]
    hb = (_rms(x_ref[0], gpre_ref[...]) * (1.0 + sc1) + sh1).astype(BF16)

    def proj(lo, width):
        return _dot(hb, _w(w_in_ref[:, lo:lo + width]))

    z_ua = proj(OFF_U, 2 * D_MODEL)
    glu = z_ua[:, 0:D_MODEL] * jax.nn.sigmoid(z_ua[:, D_MODEL:2 * D_MODEL])
    for c in range(n_col):
        full_ref[c, HIST:HIST + T, :] = glu[:, c * LANES:(c + 1) * LANES]

    @pl.when(t == pl.num_programs(1) - 1)
    def _():
        for c in range(n_col):
            cstate_ref[0, :, c * LANES:(c + 1) * LANES] = full_ref[c, HIST + T - CONV_STATE:HIST + T, :]

    def conv_block(c):
        lanes = slice(c * LANES, (c + 1) * LANES)
        for r0 in range(0, T, ROW_CHUNK):
            acc = jnp.broadcast_to(convb_ref[:, lanes], (ROW_CHUNK, LANES))
            for j in range(CONV_WIDTH):
                start = HIST - CONV_STATE + j + r0
                acc = acc + full_ref[c, start:start + ROW_CHUNK, :] * convw_ref[j:j + 1, lanes]
            conv_ref[c, r0:r0 + ROW_CHUNK, :] = acc

    assert T == CHUNK
    z_qkvg = proj(OFF_Q, OFF_U - OFF_Q)
    zq = z_qkvg[:, OFF_Q:OFF_K]
    zk = z_qkvg[:, OFF_K:OFF_V]
    zv = z_qkvg[:, OFF_V:OFF_G].astype(BF16)
    zg = z_qkvg[:, OFF_G:OFF_U]
    hq = lambda h: slice(h * DK, (h + 1) * DK)
    hv = lambda h: slice(h * DV, (h + 1) * DV)
    qb, kb, kd = [], [], []
    for h in range(N_HEADS):
        q = _rotary(zq[:, hq(h)], cos_ref[...], sin_ref[...]) * (DK ** -0.5)
        k = _rotary(zk[:, hq(h)], cos_ref[...], sin_ref[...])
        qb.append(q.astype(BF16))
        kb.append(k.astype(BF16))
        kd.append((k * rs_ref[:, N_HEADS + h:N_HEADS + h + 1]).astype(BF16))
    sc = [(lax.dot_general(qb[h], kb[h], (((1,), (1,)), ((), ())), preferred_element_type=F32)
           * decay_ref[h]).astype(BF16) for h in range(N_HEADS)]
    for c in range(0, n_col // 2):
        conv_block(c)
    ys = []
    for h in range(N_HEADS):
        s_prev = rstate_ref[0, h]
        y = _dot(sc[h], zv[:, hv(h)])
        y = y + _dot(qb[h], s_prev.astype(BF16)) * rs_ref[:, h:h + 1]
        upd = lax.dot_general(kd[h], zv[:, hv(h)], (((0,), (0,)), ((), ())), preferred_element_type=F32)
        rstate_ref[0, h] = rs_ref[0:1, 2 * N_HEADS + h:2 * N_HEADS + h + 1] * s_prev + upd
        ys.append(y)
    for c in range(n_col // 2, n_col):
        conv_block(c)
    for h in range(N_HEADS):
        yn = _standardize(ys[h]) * gng_ref[:, hv(h)]
        gated_ref[:, hv(h)] = (jax.nn.silu(zg[:, hv(h)]) * yn).astype(BF16)

    full_ref[:, 0:HIST, :] = full_ref[:, T:T + HIST, :]
    conv = jnp.concatenate([conv_ref[c] for c in range(n_col)], axis=-1)
    ln = _standardize(conv) * lng_ref[...] + lnb_ref[...]
    conv_out = _dot(jax.nn.silu(ln).astype(BF16), _w(wco_ref[...]))
    ret_out = _dot(gated_ref[...], _w(wro_ref[...]))

    z_gates = proj(OFF_GR, 2 * D_MODEL)
    merged = (jax.nn.sigmoid(z_gates[:, 0:D_MODEL]) * ret_out
              + jax.nn.sigmoid(z_gates[:, D_MODEL:2 * D_MODEL]) * conv_out)
    mix = _dot(merged.astype(BF16), _w(wo_ref[...]))
    x1_ref[0] = x_ref[0] + gt1 * _rms(mix, gpost_ref[...])


def _slab_spec(total_rows, cols, steps, step_of, packed):
    rows = next(r for r in range(BF16_TILE_ROWS, total_rows + 1, BF16_TILE_ROWS)
                if total_rows % r == 0 and total_rows // r <= steps)
    last = total_rows // rows - 1
    return pl.BlockSpec((rows // 2 if packed else rows, cols),
                        lambda *idx: (jnp.minimum(step_of(*idx), last), 0))


def _pack_slabs(srcs, dsts):
    for src, dst in zip(srcs, dsts):
        dst[...] = pltpu.bitcast(src[...].astype(BF16), jnp.uint32)


def _prompt_mixer(x, mod_p, g_pre1, g_post1, cos_p, sin_p, decay, rs_tab, w_in_b, conv_w, conv_b,
                  ln_g, ln_b, w_conv_out_b, gn_g, w_ret_out_b, w_out_b, w_gate, w_up, w_down):
    B, L, _ = x.shape
    T = T_MIX
    nt = L // T
    d_ff = w_gate.shape[1]
    row = lambda b, t: (0, 0)
    step_of = lambda b, t: b * nt + t
    slabs = [(D_MODEL, d_ff), (D_MODEL, d_ff), (d_ff, D_MODEL)]
    in_specs = [
        pl.BlockSpec((1, T, D_MODEL), lambda b, t: (b, t, 0)),
        pl.BlockSpec((1, 1, 6 * D_MODEL), lambda b, t: (b, 0, 0)),
        pl.BlockSpec((1, D_MODEL), row),
        pl.BlockSpec((1, D_MODEL), row),
        pl.BlockSpec((T, HALF), lambda b, t: (t, 0)),
        pl.BlockSpec((T, HALF), lambda b, t: (t, 0)),
        _const_spec((N_HEADS, CHUNK, CHUNK)),
        _const_spec((CHUNK, LANES)),
        _const_spec((D_MODEL // 2, D_IN)),
        pl.BlockSpec((CONV_WIDTH, D_MODEL), row),
        pl.BlockSpec((1, D_MODEL), row),
        pl.BlockSpec((1, D_MODEL), row),
        pl.BlockSpec((1, D_MODEL), row),
        _const_spec((D_MODEL // 2, D_MODEL)),
        pl.BlockSpec((1, RET_V), row),
        _const_spec((RET_V // 2, D_MODEL)),
        _const_spec((D_MODEL // 2, D_MODEL)),
    ] + [_slab_spec(r, c, B * nt, step_of, packed=False) for r, c in slabs]
    out_specs = [
        pl.BlockSpec((1, T, D_MODEL), lambda b, t: (b, t, 0)),
        pl.BlockSpec((1, N_HEADS, DK, DV), lambda b, t: (b, 0, 0, 0)),
        pl.BlockSpec((1, CONV_STATE, D_MODEL), lambda b, t: (b, 0, 0)),
    ] + [_slab_spec(r, c, B * nt, step_of, packed=True) for r, c in slabs]
    out_shape = [
        jax.ShapeDtypeStruct((B, L, D_MODEL), F32),
        jax.ShapeDtypeStruct((B, N_HEADS, DK, DV), F32),
        jax.ShapeDtypeStruct((B, CONV_STATE, D_MODEL), F32),
    ] + [jax.ShapeDtypeStruct((r // 2, c), jnp.uint32) for r, c in slabs]
    return pl.pallas_call(
        _mixer_kernel,
        grid=(B, L // T),
        in_specs=in_specs,
        out_specs=out_specs,
        out_shape=out_shape,
        scratch_shapes=[
            pltpu.VMEM((D_MODEL // LANES, HIST + T, LANES), F32),
            pltpu.VMEM((D_MODEL // LANES, T, LANES), F32),
            pltpu.VMEM((T, RET_V), BF16),
        ],
        compiler_params=pltpu.CompilerParams(
            dimension_semantics=("arbitrary", "arbitrary"),
            vmem_limit_bytes=56 * MIB),
        name="prompt_mixer",
    )(x, mod_p, g_pre1, g_post1, cos_p, sin_p, decay, rs_tab, w_in_b, conv_w, conv_b,
      ln_g, ln_b, w_conv_out_b, gn_g, w_ret_out_b, w_out_b, w_gate, w_up, w_down)


def _ffn_body(x, sh2, sc2, gt2, gpre, gpost, wg_ref, wu_ref, wd_ref):
    hb = (_rms(x, gpre) * (1.0 + sc2) + sh2).astype(BF16)
    act = (jax.nn.silu(_dot(hb, _w(wg_ref[...]))) * _dot(hb, _w(wu_ref[...]))).astype(BF16)
    return x + gt2 * _rms(_dot(act, _w(wd_ref[...])), gpost)


def _sample_ret_step(q, k, v, cos, sin, gam_ref, s0_ref, snew_ref, y_ref, i):
    q = _rotary(q, cos, sin) * (DK ** -0.5)
    k = _rotary(k, cos, sin)
    qk = jnp.sum(q * k, axis=-1, keepdims=True)
    cols = jnp.concatenate([k, q, jnp.zeros((LANES - 2 * N_HEADS, DK), F32)], axis=0).T
    for h in range(N_HEADS):
        s0 = s0_ref[i, h]
        v_h = v[h:h + 1, :]
        gam = gam_ref[h:h + 1, :]
        snew_ref[i, h] = gam * s0 + cols[:, h:h + 1] * v_h
        qs = jnp.sum(cols[:, N_HEADS + h:N_HEADS + h + 1] * s0, axis=0, keepdims=True)
        y_ref[i, h:h + 1, :] = qk[h:h + 1, :] * v_h + gam * qs


def _ffn_ret_kernel(x_ref, mod_ref, gpre_ref, gpost_ref, wg_ref, wu_ref, wd_ref,
                    q_ref, k_ref, v_ref, cos_ref, sin_ref, gam_ref, s0_ref,
                    u_ref, a_ref, cst_ref, convw_ref, convb_ref,
                    o_ref, snew_ref, y_ref, cnew_ref, conv_ref, *, conv_steps):
    step = pl.program_id(0) * pl.num_programs(1) + pl.program_id(1)

    @pl.when(step < conv_steps)
    def _():
        _sample_conv_step(u_ref, a_ref, cst_ref, convw_ref, convb_ref, cnew_ref, conv_ref)

    mod = mod_ref[0]
    o_ref[0] = _ffn_body(x_ref[0], mod[:, 3 * D_MODEL:4 * D_MODEL], mod[:, 4 * D_MODEL:5 * D_MODEL],
                         mod[:, 5 * D_MODEL:6 * D_MODEL], gpre_ref[...], gpost_ref[...],
                         wg_ref, wu_ref, wd_ref)
    for i in range(q_ref.shape[0]):
        _sample_ret_step(q_ref[i], k_ref[i], v_ref[i], cos_ref[...], sin_ref[...], gam_ref,
                         s0_ref, snew_ref, y_ref, i)


def _prompt_ffn_sample_ret(x1, mod_p, g_pre2, g_post2, wg_b, wu_b, wd_b,
                           q_s, k_s, v_s, cos_s, sin_s, gam_tab, state,
                           z_s, state_conv_t, conv_w, conv_b):
    B, L, _ = x1.shape
    nb = q_s.shape[0]
    d_ff = wg_b.shape[1]
    nt = L // T_FFN
    steps = B * nt
    sb = nb // steps
    assert sb * steps == nb
    conv_steps = nb // CONV_B
    assert conv_steps * CONV_B == nb and conv_steps <= steps
    row = lambda b, t: (0, 0)
    blk = lambda b, t: (b * nt + t, 0, 0)
    cblk = lambda b, t: jnp.minimum(b * nt + t, conv_steps - 1)
    return pl.pallas_call(
        functools.partial(_ffn_ret_kernel, conv_steps=conv_steps),
        grid=(B, nt),
        in_specs=[
            pl.BlockSpec((1, T_FFN, D_MODEL), lambda b, t: (b, t, 0)),
            pl.BlockSpec((1, 1, 6 * D_MODEL), lambda b, t: (b, 0, 0)),
            pl.BlockSpec((1, D_MODEL), row),
            pl.BlockSpec((1, D_MODEL), row),
            _const_spec((D_MODEL // 2, d_ff)),
            _const_spec((D_MODEL // 2, d_ff)),
            _const_spec((d_ff // 2, D_MODEL)),
            pl.BlockSpec((sb, N_HEADS, DK), blk),
            pl.BlockSpec((sb, N_HEADS, DK), blk),
            pl.BlockSpec((sb, N_HEADS, DV), blk),
            pl.BlockSpec((1, HALF), row),
            pl.BlockSpec((1, HALF), row),
            pl.BlockSpec((N_HEADS, DV), row),
            pl.BlockSpec((sb, N_HEADS, DK, DV), lambda b, t: (b * nt + t, 0, 0, 0)),
            pl.BlockSpec((CONV_B, D_MODEL), lambda b, t: (cblk(b, t), OFF_U // D_MODEL)),
            pl.BlockSpec((CONV_B, D_MODEL), lambda b, t: (cblk(b, t), OFF_A // D_MODEL)),
            pl.BlockSpec((1, CONV_STATE, CONV_B, D_MODEL), lambda b, t: (0, 0, cblk(b, t), 0)),
            pl.BlockSpec((CONV_WIDTH, D_MODEL), row),
            pl.BlockSpec((1, D_MODEL), row),
        ],
        out_specs=[
            pl.BlockSpec((1, T_FFN, D_MODEL), lambda b, t: (b, t, 0)),
            pl.BlockSpec((sb, N_HEADS, DK, DV), lambda b, t: (b * nt + t, 0, 0, 0)),
            pl.BlockSpec((sb, N_HEADS, DV), blk),
            pl.BlockSpec((1, CONV_STATE, CONV_B, D_MODEL), lambda b, t: (0, 0, cblk(b, t), 0)),
            pl.BlockSpec((CONV_B, D_MODEL), lambda b, t: (cblk(b, t), 0)),
        ],
        out_shape=[
            jax.ShapeDtypeStruct((B, L, D_MODEL), F32),
            jax.ShapeDtypeStruct((nb, N_HEADS, DK, DV), F32),
            jax.ShapeDtypeStruct((nb, N_HEADS, DV), F32),
            jax.ShapeDtypeStruct((1, CONV_STATE, nb, D_MODEL), F32),
            jax.ShapeDtypeStruct((nb, D_MODEL), F32),
        ],
        compiler_params=pltpu.CompilerParams(
            dimension_semantics=("arbitrary", "arbitrary"),
            vmem_limit_bytes=56 * MIB),
        name="prompt_ffn_sample_ret",
    )(x1, mod_p, g_pre2, g_post2, wg_b, wu_b, wd_b, q_s, k_s, v_s, cos_s, sin_s, gam_tab, state,
      z_s, z_s, state_conv_t, conv_w, conv_b)


def _sample_in_kernel(x_ref, sh_ref, sc_ref, gpre_ref, w_ref, wro_f_ref, wco_f_ref, wo_f_ref,
                      z_ref, wp_ref, wro_p_ref, wco_p_ref, wo_p_ref):
    hb = (_rms(x_ref[...], gpre_ref[...]) * (1.0 + sc_ref[...]) + sh_ref[...]).astype(BF16)
    wb = w_ref[...].astype(BF16)
    z_ref[...] = _dot(hb, wb)
    wp_ref[...] = pltpu.bitcast(wb, jnp.uint32)
    _pack_slabs((wro_f_ref, wco_f_ref, wo_f_ref), (wro_p_ref, wco_p_ref, wo_p_ref))


def _sample_in(xs, mod, g_pre1, w_in, w_ret_out, w_conv_out, w_out):
    rows = xs.shape[0]
    tn = D_MODEL
    steps = D_IN // tn
    slabs = [w_ret_out.shape, w_conv_out.shape, w_out.shape]
    step_of = lambda j: j
    return pl.pallas_call(
        _sample_in_kernel,
        grid=(steps,),
        in_specs=[
            pl.BlockSpec((rows, D_MODEL), lambda j: (0, 0)),
            pl.BlockSpec((rows, D_MODEL), lambda j: (0, 0)),
            pl.BlockSpec((rows, D_MODEL), lambda j: (0, 1)),
            pl.BlockSpec((1, D_MODEL), lambda j: (0, 0)),
            pl.BlockSpec((D_MODEL, tn), lambda j: (0, j)),
        ] + [_slab_spec(r, c, steps, step_of, packed=False) for r, c in slabs],
        out_specs=[
            pl.BlockSpec((rows, tn), lambda j: (0, j)),
            pl.BlockSpec((D_MODEL // 2, tn), lambda j: (0, j)),
        ] + [_slab_spec(r, c, steps, step_of, packed=True) for r, c in slabs],
        out_shape=[
            jax.ShapeDtypeStruct((rows, D_IN), F32),
            jax.ShapeDtypeStruct((D_MODEL // 2, D_IN), jnp.uint32),
        ] + [jax.ShapeDtypeStruct((r // 2, c), jnp.uint32) for r, c in slabs],
        compiler_params=pltpu.CompilerParams(dimension_semantics=("arbitrary",)),
        name="sample_in_proj",
    )(xs, mod, mod, g_pre1, w_in, w_ret_out, w_conv_out, w_out)


def _sample_conv_step(u_ref, a_ref, st_ref, w_ref, b_ref, new_ref, conv_ref):
    glu = u_ref[...] * jax.nn.sigmoid(a_ref[...])
    acc = glu * w_ref[CONV_STATE:CONV_WIDTH, :] + b_ref[...]
    for j in range(CONV_STATE):
        acc = acc + st_ref[0, j] * w_ref[j:j + 1, :]
    conv_ref[...] = acc
    for j in range(CONV_STATE - 1):
        new_ref[0, j] = st_ref[0, j + 1]
    new_ref[0, CONV_STATE - 1] = glu


def _sample_post_kernel(x_ref, mod_ref, z_ref, y_ref, conv_ref, gng_ref, wro_ref, lng_ref, lnb_ref,
                        wco_ref, wo_ref, gpost1_ref, gpre2_ref, gpost2_ref, wg_ref, wu_ref, wd_ref, o_ref):
    x = x_ref[...]
    rows = x.shape[0]

    def mod(i):
        return mod_ref[0:rows, i * D_MODEL:(i + 1) * D_MODEL]

    ret_out = jnp.zeros((rows, D_MODEL), F32)
    for h in range(N_HEADS):
        sl = slice(h * DV, (h + 1) * DV)
        yn = _standardize(y_ref[:, sl]) * gng_ref[:, sl]
        gated = (jax.nn.silu(z_ref[:, OFF_G + h * DV:OFF_G + (h + 1) * DV]) * yn).astype(BF16)
        ret_out = ret_out + _dot(gated, _w(wro_ref[h * DV // 2:(h + 1) * DV // 2, :]))
    ln = _standardize(conv_ref[...]) * lng_ref[...] + lnb_ref[...]
    conv_out = _dot(jax.nn.silu(ln).astype(BF16), _w(wco_ref[...]))
    merged = (jax.nn.sigmoid(z_ref[:, OFF_GR:OFF_GR + D_MODEL]) * ret_out
              + jax.nn.sigmoid(z_ref[:, OFF_GC:OFF_GC + D_MODEL]) * conv_out)
    mix = _dot(merged.astype(BF16), _w(wo_ref[...]))
    x1 = x + mod(2) * _rms(mix, gpost1_ref[...])
    o_ref[...] = _ffn_body(x1, mod(3), mod(4), mod(5), gpre2_ref[...], gpost2_ref[...],
                           wg_ref, wu_ref, wd_ref)


def _sample_post(xs, mod_s, z_s, y_s, conv_s, gn_g, w_ret_out_b, ln_g, ln_b, w_conv_out_b, w_out_b,
                 g_post1, g_pre2, g_post2, wg_b, wu_b, wd_b):
    return pl.pallas_call(
        _sample_post_kernel,
        out_shape=jax.ShapeDtypeStruct(xs.shape, F32),
        compiler_params=pltpu.CompilerParams(vmem_limit_bytes=56 * MIB),
        name="sample_post",
    )(xs, mod_s, z_s, y_s, conv_s, gn_g, w_ret_out_b, ln_g, ln_b, w_conv_out_b, w_out_b,
      g_post1, g_pre2, g_post2, wg_b, wu_b, wd_b)


def _rope_tables(pos):
    inv_freq = ROPE_BASE ** (-np.arange(0, DK, 2, dtype=np.float64) / DK)
    ang = np.asarray(pos, np.float64)[:, None] * inv_freq[None, :]
    return jnp.asarray(np.cos(ang), F32), jnp.asarray(np.sin(ang), F32)


def _log_gamma():
    return np.log(1.0 - np.exp(np.linspace(np.log(1.0 / 32.0), np.log(1.0 / 512.0), N_HEADS)))


def _decay_tables(chunk):
    lg = _log_gamma()
    idx = np.arange(chunk, dtype=np.float64)
    diff = idx[:, None] - idx[None, :]
    decay = np.where(diff >= 0, np.exp(lg[:, None, None] * np.maximum(diff, 0.0)), 0.0)
    cross = np.exp(lg[None, :] * (idx[:, None] + 1.0))
    k_dec = np.exp(lg[None, :] * (chunk - 1.0 - idx[:, None]))
    full = np.broadcast_to(np.exp(lg * chunk)[None, :], (chunk, N_HEADS))
    rs = np.concatenate([cross, k_dec, full], axis=1)
    rs = np.pad(rs, ((0, 0), (0, LANES - rs.shape[1])))
    return jnp.asarray(decay, F32), jnp.asarray(rs, F32)


def kernel(x_prompt, x_sample, c_prompt, c_sample, state_ret, state_conv, w_in, w_ada, b_ada, g_pre1, g_post1, g_pre2, g_post2, conv_w, conv_b, conv_ln_g, conv_ln_b, w_conv_out, ret_gn_g, w_ret_out, w_out, w_ffn_gate, w_ffn_up, w_ffn_down):
    depth = w_in.shape[0]
    assert depth == 1, "single-layer step"
    B, L, _ = x_prompt.shape
    nb = x_sample.shape[0]

    cos_p, sin_p = _rope_tables(np.arange(L))
    cos_s, sin_s = _rope_tables(PAST_LEN + np.arange(x_sample.shape[1]))
    decay, rs_tab = _decay_tables(CHUNK)
    gam_tab = jnp.asarray(np.broadcast_to(np.exp(_log_gamma())[:, None], (N_HEADS, DV)), F32)

    mod, mod_p = _modulation(c_sample, c_prompt, w_ada[0], b_ada)
    mod_p = mod_p.reshape(B, 1, 6 * D_MODEL)

    xs = x_sample.reshape(nb, D_MODEL)
    z_s, w_in_b, w_ret_out_b, w_conv_out_b, w_out_b = _sample_in(
        xs, mod, g_pre1, w_in[0], w_ret_out[0], w_conv_out[0], w_out[0])
    q_s = z_s[:, OFF_Q:OFF_Q + RET_QK].reshape(nb, N_HEADS, DK)
    k_s = z_s[:, OFF_K:OFF_K + RET_QK].reshape(nb, N_HEADS, DK)
    v_s = z_s[:, OFF_V:OFF_V + RET_V].reshape(nb, N_HEADS, DV)

    x1_p, ret_p, conv_p, wg_b, wu_b, wd_b = _prompt_mixer(
        x_prompt, mod_p, g_pre1, g_post1, cos_p, sin_p, decay, rs_tab, w_in_b, conv_w[0], conv_b,
        conv_ln_g, conv_ln_b, w_conv_out_b, ret_gn_g, w_ret_out_b, w_out_b,
        w_ffn_gate[0], w_ffn_up[0], w_ffn_down[0])
    y_p, ret_s, yr_s, conv_new_t, conv_s = _prompt_ffn_sample_ret(
        x1_p, mod_p, g_pre2, g_post2, wg_b, wu_b, wd_b, q_s, k_s, v_s, cos_s, sin_s, gam_tab, state_ret[0],
        z_s, state_conv.transpose(0, 2, 1, 3), conv_w[0], conv_b)
    conv_new_s = conv_new_t.transpose(0, 2, 1, 3)
    y_s = _sample_post(xs, mod, z_s, yr_s.reshape(nb, RET_V), conv_s, ret_gn_g, w_ret_out_b,
                       conv_ln_g, conv_ln_b, w_conv_out_b, w_out_b, g_post1, g_pre2, g_post2,
                       wg_b, wu_b, wd_b)

    return (y_p, y_s.reshape(x_sample.shape), ret_p[None], ret_s[None], conv_p[None], conv_new_s)
```

```python
import functools

import numpy as np

import jax
import jax.numpy as jnp
from jax import lax
from jax.experimental import pallas as pl
from jax.experimental.pallas import tpu as pltpu

F32 = jnp.float32
BF16 = jnp.bfloat16

D_MODEL = 1024
N_HEADS = 4
DK = 256
DV = 512
HALF = DK // 2
RET_QK = N_HEADS * DK
RET_V = N_HEADS * DV
CONV_WIDTH = 31
CONV_STATE = CONV_WIDTH - 1
EPS = 1e-6
ROPE_BASE = 10000.0
PAST_LEN = 16384

OFF_Q = 0
OFF_K = RET_QK
OFF_V = 2 * RET_QK
OFF_G = OFF_V + RET_V
OFF_U = OFF_G + RET_V
OFF_A = OFF_U + D_MODEL
OFF_GR = OFF_A + D_MODEL
OFF_GC = OFF_GR + D_MODEL
D_IN = OFF_GC + D_MODEL

LANES = 128
BF16_TILE_ROWS = 16
T_MIX = 256
CHUNK = 256
HIST = 32
ROW_CHUNK = 64
T_FFN = 256
CONV_B = 8
MIB = 1024 * 1024


def _rms(x, g):
    ms = jnp.mean(x * x, axis=-1, keepdims=True)
    return x * lax.rsqrt(ms + EPS) * g


def _rotary(t, cos, sin):
    t1, t2 = t[:, :HALF], t[:, HALF:]
    return jnp.concatenate([t1 * cos - t2 * sin, t2 * cos + t1 * sin], axis=-1)


def _standardize(y):
    mu = jnp.mean(y, axis=-1, keepdims=True)
    yc = y - mu
    var = jnp.mean(yc * yc, axis=-1, keepdims=True)
    return yc * lax.rsqrt(var + EPS)


def _dot(a, b):
    return jnp.dot(a, b, preferred_element_type=F32)


def _w(packed):
    return pltpu.bitcast(packed, BF16)


def _const_spec(shape):
    return pl.BlockSpec(shape, lambda *_: (0,) * len(shape), pipeline_mode=pl.Buffered(1))


def _mod_kernel(cs_ref, cp_ref, w_ref, b_ref, os_ref, op_ref):
    wb = w_ref[...].astype(BF16)
    for c_ref, o_ref in ((cs_ref, os_ref), (cp_ref, op_ref)):
        o_ref[...] = _dot(jax.nn.silu(c_ref[...]).astype(BF16), wb) + b_ref[...]


def _modulation(c_sample, c_prompt, w_ada, b_ada):
    n_out = w_ada.shape[1]
    tn = D_MODEL
    groups = (c_sample.shape[0], c_prompt.shape[0])
    return pl.pallas_call(
        _mod_kernel,
        grid=(n_out // tn,),
        in_specs=[pl.BlockSpec((rows, D_MODEL), lambda j: (0, 0)) for rows in groups] + [
            pl.BlockSpec((D_MODEL, tn), lambda j: (0, j)),
            pl.BlockSpec((1, tn), lambda j: (0, j)),
        ],
        out_specs=[pl.BlockSpec((rows, tn), lambda j: (0, j)) for rows in groups],
        out_shape=[jax.ShapeDtypeStruct((rows, n_out), F32) for rows in groups],
        compiler_params=pltpu.CompilerParams(dimension_semantics=("arbitrary",)),
        name="adaln_mod",
    )(c_sample, c_prompt, w_ada, b_ada)


def _mixer_kernel(x_ref, mod_ref, gpre_ref, gpost_ref, cos_ref, sin_ref, decay_ref, rs_ref,
                  w_in_ref, convw_ref, convb_ref, lng_ref, lnb_ref, wco_ref, gng_ref, wro_ref, wo_ref,
                  wg_f_ref, wu_f_ref, wd_f_ref,
                  x1_ref, rstate_ref, cstate_ref, wg_p_ref, wu_p_ref, wd_p_ref,
                  full_ref, conv_ref, gated_ref):
    t = pl.program_id(1)
    T = T_MIX
    n_col = D_MODEL // LANES

    _pack_slabs((wg_f_ref, wu_f_ref, wd_f_ref), (wg_p_ref, wu_p_ref, wd_p_ref))

    @pl.when(t == 0)
    def _():
        rstate_ref[...] = jnp.zeros_like(rstate_ref)
        full_ref[:, 0:HIST, :] = jnp.zeros((n_col, HIST, LANES), F32)

    mod = mod_ref[0]
    sh1 = mod[:, 0:D_MODEL]
    sc1 = mod[:, D_MODEL:2 * D_MODEL]
    gt1 = mod[:, 2 * D_MODEL:3 * D_MODEL]
    hb = (_rms(x_ref[0], gpre_ref[...]) * (1.0 + sc1) + sh1).astype(BF16)

    def proj(lo, width):
        return _dot(hb, _w(w_in_ref[:, lo:lo + width]))

    glu = proj(OFF_U, D_MODEL) * jax.nn.sigmoid(proj(OFF_A, D_MODEL))
    for c in range(n_col):
        full_ref[c, HIST:HIST + T, :] = glu[:, c * LANES:(c + 1) * LANES]

    @pl.when(t == pl.num_programs(1) - 1)
    def _():
        for c in range(n_col):
            cstate_ref[0, :, c * LANES:(c + 1) * LANES] = full_ref[c, HIST + T - CONV_STATE:HIST + T, :]

    def conv_block(c):
        lanes = slice(c * LANES, (c + 1) * LANES)
        for r0 in range(0, T, ROW_CHUNK):
            acc = jnp.broadcast_to(convb_ref[:, lanes], (ROW_CHUNK, LANES))
            for j in range(CONV_WIDTH):
                start = HIST - CONV_STATE + j + r0
                acc = acc + full_ref[c, start:start + ROW_CHUNK, :] * convw_ref[j:j + 1, lanes]
            conv_ref[c, r0:r0 + ROW_CHUNK, :] = acc

    assert T == CHUNK
    zq = proj(OFF_Q, RET_QK)
    zk = proj(OFF_K, RET_QK)
    zv = proj(OFF_V, RET_V).astype(BF16)
    hq = lambda h: slice(h * DK, (h + 1) * DK)
    hv = lambda h: slice(h * DV, (h + 1) * DV)
    qb, kb, kd = [], [], []
    for h in range(N_HEADS):
        q = _rotary(zq[:, hq(h)], cos_ref[...], sin_ref[...]) * (DK ** -0.5)
        k = _rotary(zk[:, hq(h)], cos_ref[...], sin_ref[...])
        qb.append(q.astype(BF16))
        kb.append(k.astype(BF16))
        kd.append((k * rs_ref[:, N_HEADS + h:N_HEADS + h + 1]).astype(BF16))
    sc = [(lax.dot_general(qb[h], kb[h], (((1,), (1,)), ((), ())), preferred_element_type=F32)
           * decay_ref[h]).astype(BF16) for h in range(N_HEADS)]
    for c in range(0, n_col // 2):
        conv_block(c)
    ys = []
    for h in range(N_HEADS):
        s_prev = rstate_ref[0, h]
        y = _dot(sc[h], zv[:, hv(h)])
        y = y + _dot(qb[h], s_prev.astype(BF16)) * rs_ref[:, h:h + 1]
        upd = lax.dot_general(kd[h], zv[:, hv(h)], (((0,), (0,)), ((), ())), preferred_element_type=F32)
        rstate_ref[0, h] = rs_ref[0:1, 2 * N_HEADS + h:2 * N_HEADS + h + 1] * s_prev + upd
        ys.append(y)
    zg = proj(OFF_G, RET_V)
    for c in range(n_col // 2, n_col):
        conv_block(c)
    full_ref[:, 0:HIST, :] = full_ref[:, T:T + HIST, :]
    conv = jnp.concatenate([conv_ref[c] for c in range(n_col)], axis=-1)
    ln = _standardize(conv) * lng_ref[...] + lnb_ref[...]
    conv_out = _dot(jax.nn.silu(ln).astype(BF16), _w(wco_ref[...]))
    gate_r = jax.nn.sigmoid(proj(OFF_GR, D_MODEL))
    gate_c = jax.nn.sigmoid(proj(OFF_GC, D_MODEL))
    for h in range(N_HEADS):
        yn = _standardize(ys[h]) * gng_ref[:, hv(h)]
        gated_ref[:, hv(h)] = (jax.nn.silu(zg[:, hv(h)]) * yn).astype(BF16)
    ret_out = _dot(gated_ref[...], _w(wro_ref[...]))

    merged = gate_r * ret_out + gate_c * conv_out
    mix = _dot(merged.astype(BF16), _w(wo_ref[...]))
    x1_ref[0] = x_ref[0] + gt1 * _rms(mix, gpost_ref[...])


def _slab_spec(total_rows, cols, steps, step_of, packed):
    rows = next(r for r in range(BF16_TILE_ROWS, total_rows + 1, BF16_TILE_ROWS)
                if total_rows % r == 0 and total_rows // r <= steps)
    last = total_rows // rows - 1
    return pl.BlockSpec((rows // 2 if packed else rows, cols),
                        lambda *idx: (jnp.minimum(step_of(*idx), last), 0))


def _pack_slabs(srcs, dsts):
    for src, dst in zip(srcs, dsts):
        dst[...] = pltpu.bitcast(src[...].astype(BF16), jnp.uint32)


def _prompt_mixer(x, mod_p, g_pre1, g_post1, cos_p, sin_p, decay, rs_tab, w_in_b, conv_w, conv_b,
                  ln_g, ln_b, w_conv_out_b, gn_g, w_ret_out_b, w_out_b, w_gate, w_up, w_down):
    B, L, _ = x.shape
    T = T_MIX
    nt = L // T
    d_ff = w_gate.shape[1]
    row = lambda b, t: (0, 0)
    step_of = lambda b, t: b * nt + t
    slabs = [(D_MODEL, d_ff), (D_MODEL, d_ff), (d_ff, D_MODEL)]
    in_specs = [
        pl.BlockSpec((1, T, D_MODEL), lambda b, t: (b, t, 0)),
        pl.BlockSpec((1, 1, 6 * D_MODEL), lambda b, t: (b, 0, 0)),
        pl.BlockSpec((1, D_MODEL), row),
        pl.BlockSpec((1, D_MODEL), row),
        pl.BlockSpec((T, HALF), lambda b, t: (t, 0)),
        pl.BlockSpec((T, HALF), lambda b, t: (t, 0)),
        _const_spec((N_HEADS, CHUNK, CHUNK)),
        _const_spec((CHUNK, LANES)),
        _const_spec((D_MODEL // 2, D_IN)),
        pl.BlockSpec((CONV_WIDTH, D_MODEL), row),
        pl.BlockSpec((1, D_MODEL), row),
        pl.BlockSpec((1, D_MODEL), row),
        pl.BlockSpec((1, D_MODEL), row),
        _const_spec((D_MODEL // 2, D_MODEL)),
        pl.BlockSpec((1, RET_V), row),
        _const_spec((RET_V // 2, D_MODEL)),
        _const_spec((D_MODEL // 2, D_MODEL)),
    ] + [_slab_spec(r, c, B * nt, step_of, packed=False) for r, c in slabs]
    out_specs = [
        pl.BlockSpec((1, T, D_MODEL), lambda b, t: (b, t, 0)),
        pl.BlockSpec((1, N_HEADS, DK, DV), lambda b, t: (b, 0, 0, 0)),
        pl.BlockSpec((1, CONV_STATE, D_MODEL), lambda b, t: (b, 0, 0)),
    ] + [_slab_spec(r, c, B * nt, step_of, packed=True) for r, c in slabs]
    out_shape = [
        jax.ShapeDtypeStruct((B, L, D_MODEL), F32),
        jax.ShapeDtypeStruct((B, N_HEADS, DK, DV), F32),
        jax.ShapeDtypeStruct((B, CONV_STATE, D_MODEL), F32),
    ] + [jax.ShapeDtypeStruct((r // 2, c), jnp.uint32) for r, c in slabs]
    return pl.pallas_call(
        _mixer_kernel,
        grid=(B, L // T),
        in_specs=in_specs,
        out_specs=out_specs,
        out_shape=out_shape,
        scratch_shapes=[
            pltpu.VMEM((D_MODEL // LANES, HIST + T, LANES), F32),
            pltpu.VMEM((D_MODEL // LANES, T, LANES), F32),
            pltpu.VMEM((T, RET_V), BF16),
        ],
        compiler_params=pltpu.CompilerParams(
            dimension_semantics=("arbitrary", "arbitrary"),
            vmem_limit_bytes=56 * MIB),
        name="prompt_mixer",
    )(x, mod_p, g_pre1, g_post1, cos_p, sin_p, decay, rs_tab, w_in_b, conv_w, conv_b,
      ln_g, ln_b, w_conv_out_b, gn_g, w_ret_out_b, w_out_b, w_gate, w_up, w_down)


def _ffn_body(x, sh2, sc2, gt2, gpre, gpost, wg_ref, wu_ref, wd_ref):
    hb = (_rms(x, gpre) * (1.0 + sc2) + sh2).astype(BF16)
    act = (jax.nn.silu(_dot(hb, _w(wg_ref[...]))) * _dot(hb, _w(wu_ref[...]))).astype(BF16)
    return x + gt2 * _rms(_dot(act, _w(wd_ref[...])), gpost)


def _sample_ret_step(q, k, v, cos, sin, gam_ref, s0_ref, snew_ref, y_ref, i):
    q = _rotary(q, cos, sin) * (DK ** -0.5)
    k = _rotary(k, cos, sin)
    qk = jnp.sum(q * k, axis=-1, keepdims=True)
    cols = jnp.concatenate([k, q, jnp.zeros((LANES - 2 * N_HEADS, DK), F32)], axis=0).T
    for h in range(N_HEADS):
        s0 = s0_ref[i, h]
        v_h = v[h:h + 1, :]
        gam = gam_ref[h:h + 1, :]
        snew_ref[i, h] = gam * s0 + cols[:, h:h + 1] * v_h
        qs = jnp.sum(cols[:, N_HEADS + h:N_HEADS + h + 1] * s0, axis=0, keepdims=True)
        y_ref[i, h:h + 1, :] = qk[h:h + 1, :] * v_h + gam * qs


def _ffn_ret_kernel(x_ref, mod_ref, gpre_ref, gpost_ref, wg_ref, wu_ref, wd_ref,
                    q_ref, k_ref, v_ref, cos_ref, sin_ref, gam_ref, s0_ref,
                    u_ref, a_ref, cst_ref, convw_ref, convb_ref,
                    o_ref, snew_ref, y_ref, cnew_ref, conv_ref, *, conv_steps):
    step = pl.program_id(0) * pl.num_programs(1) + pl.program_id(1)

    @pl.when(step < conv_steps)
    def _():
        _sample_conv_step(u_ref, a_ref, cst_ref, convw_ref, convb_ref, cnew_ref, conv_ref)

    mod = mod_ref[0]
    o_ref[0] = _ffn_body(x_ref[0], mod[:, 3 * D_MODEL:4 * D_MODEL], mod[:, 4 * D_MODEL:5 * D_MODEL],
                         mod[:, 5 * D_MODEL:6 * D_MODEL], gpre_ref[...], gpost_ref[...],
                         wg_ref, wu_ref, wd_ref)
    for i in range(q_ref.shape[0]):
        _sample_ret_step(q_ref[i], k_ref[i], v_ref[i], cos_ref[...], sin_ref[...], gam_ref,
                         s0_ref, snew_ref, y_ref, i)


def _prompt_ffn_sample_ret(x1, mod_p, g_pre2, g_post2, wg_b, wu_b, wd_b,
                           q_s, k_s, v_s, cos_s, sin_s, gam_tab, state,
                           z_s, state_conv_t, conv_w, conv_b):
    B, L, _ = x1.shape
    nb = q_s.shape[0]
    d_ff = wg_b.shape[1]
    nt = L // T_FFN
    steps = B * nt
    sb = nb // steps
    assert sb * steps == nb
    conv_steps = nb // CONV_B
    assert conv_steps * CONV_B == nb and conv_steps <= steps
    row = lambda b, t: (0, 0)
    blk = lambda b, t: (b * nt + t, 0, 0)
    cblk = lambda b, t: jnp.minimum(b * nt + t, conv_steps - 1)
    return pl.pallas_call(
        functools.partial(_ffn_ret_kernel, conv_steps=conv_steps),
        grid=(B, nt),
        in_specs=[
            pl.BlockSpec((1, T_FFN, D_MODEL), lambda b, t: (b, t, 0)),
            pl.BlockSpec((1, 1, 6 * D_MODEL), lambda b, t: (b, 0, 0)),
            pl.BlockSpec((1, D_MODEL), row),
            pl.BlockSpec((1, D_MODEL), row),
            _const_spec((D_MODEL // 2, d_ff)),
            _const_spec((D_MODEL // 2, d_ff)),
            _const_spec((d_ff // 2, D_MODEL)),
            pl.BlockSpec((sb, N_HEADS, DK), blk),
            pl.BlockSpec((sb, N_HEADS, DK), blk),
            pl.BlockSpec((sb, N_HEADS, DV), blk),
            pl.BlockSpec((1, HALF), row),
            pl.BlockSpec((1, HALF), row),
            pl.BlockSpec((N_HEADS, DV), row),
            pl.BlockSpec((sb, N_HEADS, DK, DV), lambda b, t: (b * nt + t, 0, 0, 0)),
            pl.BlockSpec((CONV_B, D_MODEL), lambda b, t: (cblk(b, t), OFF_U // D_MODEL)),
            pl.BlockSpec((CONV_B, D_MODEL), lambda b, t: (cblk(b, t), OFF_A // D_MODEL)),
            pl.BlockSpec((1, CONV_STATE, CONV_B, D_MODEL), lambda b, t: (0, 0, cblk(b, t), 0)),
            pl.BlockSpec((CONV_WIDTH, D_MODEL), row),
            pl.BlockSpec((1, D_MODEL), row),
        ],
        out_specs=[
            pl.BlockSpec((1, T_FFN, D_MODEL), lambda b, t: (b, t, 0)),
            pl.BlockSpec((sb, N_HEADS, DK, DV), lambda b, t: (b * nt + t, 0, 0, 0)),
            pl.BlockSpec((sb, N_HEADS, DV), blk),
            pl.BlockSpec((1, CONV_STATE, CONV_B, D_MODEL), lambda b, t: (0, 0, cblk(b, t), 0)),
            pl.BlockSpec((CONV_B, D_MODEL), lambda b, t: (cblk(b, t), 0)),
        ],
        out_shape=[
            jax.ShapeDtypeStruct((B, L, D_MODEL), F32),
            jax.ShapeDtypeStruct((nb, N_HEADS, DK, DV), F32),
            jax.ShapeDtypeStruct((nb, N_HEADS, DV), F32),
            jax.ShapeDtypeStruct((1, CONV_STATE, nb, D_MODEL), F32),
            jax.ShapeDtypeStruct((nb, D_MODEL), F32),
        ],
        compiler_params=pltpu.CompilerParams(
            dimension_semantics=("arbitrary", "arbitrary"),
            vmem_limit_bytes=56 * MIB),
        name="prompt_ffn_sample_ret",
    )(x1, mod_p, g_pre2, g_post2, wg_b, wu_b, wd_b, q_s, k_s, v_s, cos_s, sin_s, gam_tab, state,
      z_s, z_s, state_conv_t, conv_w, conv_b)


def _sample_in_kernel(x_ref, sh_ref, sc_ref, gpre_ref, w_ref, wro_f_ref, wco_f_ref, wo_f_ref,
                      z_ref, wp_ref, wro_p_ref, wco_p_ref, wo_p_ref):
    hb = (_rms(x_ref[...], gpre_ref[...]) * (1.0 + sc_ref[...]) + sh_ref[...]).astype(BF16)
    wb = w_ref[...].astype(BF16)
    z_ref[...] = _dot(hb, wb)
    wp_ref[...] = pltpu.bitcast(wb, jnp.uint32)
    _pack_slabs((wro_f_ref, wco_f_ref, wo_f_ref), (wro_p_ref, wco_p_ref, wo_p_ref))


def _sample_in(xs, mod, g_pre1, w_in, w_ret_out, w_conv_out, w_out):
    rows = xs.shape[0]
    tn = D_MODEL
    steps = D_IN // tn
    slabs = [w_ret_out.shape, w_conv_out.shape, w_out.shape]
    step_of = lambda j: j
    return pl.pallas_call(
        _sample_in_kernel,
        grid=(steps,),
        in_specs=[
            pl.BlockSpec((rows, D_MODEL), lambda j: (0, 0)),
            pl.BlockSpec((rows, D_MODEL), lambda j: (0, 0)),
            pl.BlockSpec((rows, D_MODEL), lambda j: (0, 1)),
            pl.BlockSpec((1, D_MODEL), lambda j: (0, 0)),
            pl.BlockSpec((D_MODEL, tn), lambda j: (0, j)),
        ] + [_slab_spec(r, c, steps, step_of, packed=False) for r, c in slabs],
        out_specs=[
            pl.BlockSpec((rows, tn), lambda j: (0, j)),
            pl.BlockSpec((D_MODEL // 2, tn), lambda j: (0, j)),
        ] + [_slab_spec(r, c, steps, step_of, packed=True) for r, c in slabs],
        out_shape=[
            jax.ShapeDtypeStruct((rows, D_IN), F32),
            jax.ShapeDtypeStruct((D_MODEL // 2, D_IN), jnp.uint32),
        ] + [jax.ShapeDtypeStruct((r // 2, c), jnp.uint32) for r, c in slabs],
        compiler_params=pltpu.CompilerParams(dimension_semantics=("arbitrary",)),
        name="sample_in_proj",
    )(xs, mod, mod, g_pre1, w_in, w_ret_out, w_conv_out, w_out)


def _sample_conv_step(u_ref, a_ref, st_ref, w_ref, b_ref, new_ref, conv_ref):
    glu = u_ref[...] * jax.nn.sigmoid(a_ref[...])
    acc = glu * w_ref[CONV_STATE:CONV_WIDTH, :] + b_ref[...]
    for j in range(CONV_STATE):
        acc = acc + st_ref[0, j] * w_ref[j:j + 1, :]
    conv_ref[...] = acc
    for j in range(CONV_STATE - 1):
        new_ref[0, j] = st_ref[0, j + 1]
    new_ref[0, CONV_STATE - 1] = glu


def _sample_post_kernel(x_ref, mod_ref, z_ref, y_ref, conv_ref, gng_ref, wro_ref, lng_ref, lnb_ref,
                        wco_ref, wo_ref, gpost1_ref, gpre2_ref, gpost2_ref, wg_ref, wu_ref, wd_ref, o_ref):
    x = x_ref[...]
    rows = x.shape[0]

    def mod(i):
        return mod_ref[0:rows, i * D_MODEL:(i + 1) * D_MODEL]

    ret_out = jnp.zeros((rows, D_MODEL), F32)
    for h in range(N_HEADS):
        sl = slice(h * DV, (h + 1) * DV)
        yn = _standardize(y_ref[:, sl]) * gng_ref[:, sl]
        gated = (jax.nn.silu(z_ref[:, OFF_G + h * DV:OFF_G + (h + 1) * DV]) * yn).astype(BF16)
        ret_out = ret_out + _dot(gated, _w(wro_ref[h * DV // 2:(h + 1) * DV // 2, :]))
    ln = _standardize(conv_ref[...]) * lng_ref[...] + lnb_ref[...]
    conv_out = _dot(jax.nn.silu(ln).astype(BF16), _w(wco_ref[...]))
    merged = (jax.nn.sigmoid(z_ref[:, OFF_GR:OFF_GR + D_MODEL]) * ret_out
              + jax.nn.sigmoid(z_ref[:, OFF_GC:OFF_GC + D_MODEL]) * conv_out)
    mix = _dot(merged.astype(BF16), _w(wo_ref[...]))
    x1 = x + mod(2) * _rms(mix, gpost1_ref[...])
    o_ref[...] = _ffn_body(x1, mod(3), mod(4), mod(5), gpre2_ref[...], gpost2_ref[...],
                           wg_ref, wu_ref, wd_ref)


def _sample_post(xs, mod_s, z_s, y_s, conv_s, gn_g, w_ret_out_b, ln_g, ln_b, w_conv_out_b, w_out_b,
                 g_post1, g_pre2, g_post2, wg_b, wu_b, wd_b):
    return pl.pallas_call(
        _sample_post_kernel,
        out_shape=jax.ShapeDtypeStruct(xs.shape, F32),
        compiler_params=pltpu.CompilerParams(vmem_limit_bytes=56 * MIB),
        name="sample_post",
    )(xs, mod_s, z_s, y_s, conv_s, gn_g, w_ret_out_b, ln_g, ln_b, w_conv_out_b, w_out_b,
      g_post1, g_pre2, g_post2, wg_b, wu_b, wd_b)


def _rope_tables(pos):
    inv_freq = ROPE_BASE ** (-np.arange(0, DK, 2, dtype=np.float64) / DK)
    ang = np.asarray(pos, np.float64)[:, None] * inv_freq[None, :]
    return jnp.asarray(np.cos(ang), F32), jnp.asarray(np.sin(ang), F32)


def _log_gamma():
    return np.log(1.0 - np.exp(np.linspace(np.log(1.0 / 32.0), np.log(1.0 / 512.0), N_HEADS)))


def _decay_tables(chunk):
    lg = _log_gamma()
    idx = np.arange(chunk, dtype=np.float64)
    diff = idx[:, None] - idx[None, :]
    decay = np.where(diff >= 0, np.exp(lg[:, None, None] * np.maximum(diff, 0.0)), 0.0)
    cross = np.exp(lg[None, :] * (idx[:, None] + 1.0))
    k_dec = np.exp(lg[None, :] * (chunk - 1.0 - idx[:, None]))
    full = np.broadcast_to(np.exp(lg * chunk)[None, :], (chunk, N_HEADS))
    rs = np.concatenate([cross, k_dec, full], axis=1)
    rs = np.pad(rs, ((0, 0), (0, LANES - rs.shape[1])))
    return jnp.asarray(decay, F32), jnp.asarray(rs, F32)


def kernel(x_prompt, x_sample, c_prompt, c_sample, state_ret, state_conv, w_in, w_ada, b_ada, g_pre1, g_post1, g_pre2, g_post2, conv_w, conv_b, conv_ln_g, conv_ln_b, w_conv_out, ret_gn_g, w_ret_out, w_out, w_ffn_gate, w_ffn_up, w_ffn_down):
    depth = w_in.shape[0]
    assert depth == 1, "single-layer step"
    B, L, _ = x_prompt.shape
    nb = x_sample.shape[0]

    cos_p, sin_p = _rope_tables(np.arange(L))
    cos_s, sin_s = _rope_tables(PAST_LEN + np.arange(x_sample.shape[1]))
    decay, rs_tab = _decay_tables(CHUNK)
    gam_tab = jnp.asarray(np.broadcast_to(np.exp(_log_gamma())[:, None], (N_HEADS, DV)), F32)

    mod, mod_p = _modulation(c_sample, c_prompt, w_ada[0], b_ada)
    mod_p = mod_p.reshape(B, 1, 6 * D_MODEL)

    xs = x_sample.reshape(nb, D_MODEL)
    z_s, w_in_b, w_ret_out_b, w_conv_out_b, w_out_b = _sample_in(
        xs, mod, g_pre1, w_in[0], w_ret_out[0], w_conv_out[0], w_out[0])
    q_s = z_s[:, OFF_Q:OFF_Q + RET_QK].reshape(nb, N_HEADS, DK)
    k_s = z_s[:, OFF_K:OFF_K + RET_QK].reshape(nb, N_HEADS, DK)
    v_s = z_s[:, OFF_V:OFF_V + RET_V].reshape(nb, N_HEADS, DV)

    x1_p, ret_p, conv_p, wg_b, wu_b, wd_b = _prompt_mixer(
        x_prompt, mod_p, g_pre1, g_post1, cos_p, sin_p, decay, rs_tab, w_in_b, conv_w[0], conv_b,
        conv_ln_g, conv_ln_b, w_conv_out_b, ret_gn_g, w_ret_out_b, w_out_b,
        w_ffn_gate[0], w_ffn_up[0], w_ffn_down[0])
    y_p, ret_s, yr_s, conv_new_t, conv_s = _prompt_ffn_sample_ret(
        x1_p, mod_p, g_pre2, g_post2, wg_b, wu_b, wd_b, q_s, k_s, v_s, cos_s, sin_s, gam_tab, state_ret[0],
        z_s, state_conv.transpose(0, 2, 1, 3), conv_w[0], conv_b)
    conv_new_s = conv_new_t.transpose(0, 2, 1, 3)
    y_s = _sample_post(xs, mod, z_s, yr_s.reshape(nb, RET_V), conv_s, ret_gn_g, w_ret_out_b,
                       conv_ln_g, conv_ln_b, w_conv_out_b, w_out_b, g_post1, g_pre2, g_post2,
                       wg_b, wu_b, wd_b)

    return (y_p, y_s.reshape(x_sample.shape), ret_p[None], ret_s[None], conv_p[None], conv_new_s)
```

```python
import functools

import numpy as np

import jax
import jax.numpy as jnp
from jax import lax
from jax.experimental import pallas as pl
from jax.experimental.pallas import tpu as pltpu

F32 = jnp.float32
BF16 = jnp.bfloat16

D_MODEL = 1024
N_HEADS = 4
DK = 256
DV = 512
HALF = DK // 2
RET_QK = N_HEADS * DK
RET_V = N_HEADS * DV
CONV_WIDTH = 31
CONV_STATE = CONV_WIDTH - 1
EPS = 1e-6
ROPE_BASE = 10000.0
PAST_LEN = 16384

OFF_Q = 0
OFF_K = RET_QK
OFF_V = 2 * RET_QK
OFF_G = OFF_V + RET_V
OFF_U = OFF_G + RET_V
OFF_A = OFF_U + D_MODEL
OFF_GR = OFF_A + D_MODEL
OFF_GC = OFF_GR + D_MODEL
D_IN = OFF_GC + D_MODEL

LANES = 128
BF16_TILE_ROWS = 16
T_MIX = 256
CHUNK = 256
HIST = 32
ROW_CHUNK = 64
T_FFN = 256
CONV_B = 8
MIB = 1024 * 1024
V7X_VMEM_BYTES = 64 * MIB
VMEM_LIMIT = V7X_VMEM_BYTES - 8 * MIB


def _rms(x, g):
    ms = jnp.mean(x * x, axis=-1, keepdims=True)
    return x * lax.rsqrt(ms + EPS) * g


def _rotary(t, cos, sin):
    t1, t2 = t[:, :HALF], t[:, HALF:]
    return jnp.concatenate([t1 * cos - t2 * sin, t2 * cos + t1 * sin], axis=-1)


def _standardize(y):
    mu = jnp.mean(y, axis=-1, keepdims=True)
    yc = y - mu
    var = jnp.mean(yc * yc, axis=-1, keepdims=True)
    return yc * lax.rsqrt(var + EPS)


def _dot(a, b):
    return jnp.dot(a, b, preferred_element_type=F32)


def _w(packed):
    return pltpu.bitcast(packed, BF16)


def _const_spec(shape):
    return pl.BlockSpec(shape, lambda *_: (0,) * len(shape), pipeline_mode=pl.Buffered(1))


def _mod_kernel(cs_ref, cp_ref, w_ref, b_ref, os_ref, op_ref):
    wb = w_ref[...].astype(BF16)
    for c_ref, o_ref in ((cs_ref, os_ref), (cp_ref, op_ref)):
        o_ref[...] = _dot(jax.nn.silu(c_ref[...]).astype(BF16), wb) + b_ref[...]


def _modulation(c_sample, c_prompt, w_ada, b_ada):
    n_out = w_ada.shape[1]
    tn = D_MODEL
    groups = (c_sample.shape[0], c_prompt.shape[0])
    return pl.pallas_call(
        _mod_kernel,
        grid=(n_out // tn,),
        in_specs=[pl.BlockSpec((rows, D_MODEL), lambda j: (0, 0)) for rows in groups] + [
            pl.BlockSpec((D_MODEL, tn), lambda j: (0, j)),
            pl.BlockSpec((1, tn), lambda j: (0, j)),
        ],
        out_specs=[pl.BlockSpec((rows, tn), lambda j: (0, j)) for rows in groups],
        out_shape=[jax.ShapeDtypeStruct((rows, n_out), F32) for rows in groups],
        compiler_params=pltpu.CompilerParams(dimension_semantics=("arbitrary",)),
        name="adaln_mod",
    )(c_sample, c_prompt, w_ada, b_ada)


def _mixer_kernel(x_ref, mod_ref, gpre_ref, gpost_ref, cos_ref, sin_ref, decay_ref, rs_ref,
                  w_in_ref, convw_ref, convb_ref, lng_ref, lnb_ref, wco_ref, gng_ref, wro_ref, wo_ref,
                  wg_f_ref, wu_f_ref, wd_f_ref,
                  x1_ref, rstate_ref, cstate_ref, wg_p_ref, wu_p_ref, wd_p_ref,
                  full_ref, conv_ref, gated_ref):
    t = pl.program_id(1)
    T = T_MIX
    n_col = D_MODEL // LANES

    _pack_slabs((wg_f_ref, wu_f_ref, wd_f_ref), (wg_p_ref, wu_p_ref, wd_p_ref))

    @pl.when(t == 0)
    def _():
        rstate_ref[...] = jnp.zeros_like(rstate_ref)
        full_ref[:, 0:HIST, :] = jnp.zeros((n_col, HIST, LANES), F32)

    mod = mod_ref[0]
    sh1 = mod[:, 0:D_MODEL]
    sc1 = mod[:, D_MODEL:2 * D_MODEL]
    gt1 = mod[:, 2 * D_MODEL:3 * D_MODEL]
    hb = (_rms(x_ref[0], gpre_ref[...]) * (1.0 + sc1) + sh1).astype(BF16)

    def proj(lo, width):
        return _dot(hb, _w(w_in_ref[:, lo:lo + width]))

    glu = proj(OFF_U, D_MODEL) * jax.nn.sigmoid(proj(OFF_A, D_MODEL))
    for c in range(n_col):
        full_ref[c, HIST:HIST + T, :] = glu[:, c * LANES:(c + 1) * LANES]

    @pl.when(t == pl.num_programs(1) - 1)
    def _():
        for c in range(n_col):
            cstate_ref[0, :, c * LANES:(c + 1) * LANES] = full_ref[c, HIST + T - CONV_STATE:HIST + T, :]

    def conv_block(c):
        lanes = slice(c * LANES, (c + 1) * LANES)
        for r0 in range(0, T, ROW_CHUNK):
            acc = jnp.broadcast_to(convb_ref[:, lanes], (ROW_CHUNK, LANES))
            for j in range(CONV_WIDTH):
                start = HIST - CONV_STATE + j + r0
                acc = acc + full_ref[c, start:start + ROW_CHUNK, :] * convw_ref[j:j + 1, lanes]
            conv_ref[c, r0:r0 + ROW_CHUNK, :] = acc

    assert T == CHUNK
    zq = proj(OFF_Q, RET_QK)
    zk = proj(OFF_K, RET_QK)
    zv = proj(OFF_V, RET_V).astype(BF16)
    hq = lambda h: slice(h * DK, (h + 1) * DK)
    hv = lambda h: slice(h * DV, (h + 1) * DV)
    qb, kb, kd = [], [], []
    for h in range(N_HEADS):
        q = _rotary(zq[:, hq(h)], cos_ref[...], sin_ref[...]) * (DK ** -0.5)
        k = _rotary(zk[:, hq(h)], cos_ref[...], sin_ref[...])
        qb.append(q.astype(BF16))
        kb.append(k.astype(BF16))
        kd.append((k * rs_ref[:, N_HEADS + h:N_HEADS + h + 1]).astype(BF16))
    sc = [(lax.dot_general(qb[h], kb[h], (((1,), (1,)), ((), ())), preferred_element_type=F32)
           * decay_ref[h]).astype(BF16) for h in range(N_HEADS)]
    for c in range(0, n_col // 2):
        conv_block(c)
    ys = []
    for h in range(N_HEADS):
        s_prev = rstate_ref[0, h]
        y = _dot(sc[h], zv[:, hv(h)])
        y = y + _dot(qb[h], s_prev.astype(BF16)) * rs_ref[:, h:h + 1]
        upd = lax.dot_general(kd[h], zv[:, hv(h)], (((0,), (0,)), ((), ())), preferred_element_type=F32)
        rstate_ref[0, h] = rs_ref[0:1, 2 * N_HEADS + h:2 * N_HEADS + h + 1] * s_prev + upd
        ys.append(y)
    zg = proj(OFF_G, RET_V)
    for c in range(n_col // 2, n_col):
        conv_block(c)
    for h in range(N_HEADS):
        yn = _standardize(ys[h]) * gng_ref[:, hv(h)]
        gated_ref[:, hv(h)] = (jax.nn.silu(zg[:, hv(h)]) * yn).astype(BF16)

    full_ref[:, 0:HIST, :] = full_ref[:, T:T + HIST, :]
    conv = jnp.concatenate([conv_ref[c] for c in range(n_col)], axis=-1)
    ln = _standardize(conv) * lng_ref[...] + lnb_ref[...]
    conv_out = _dot(jax.nn.silu(ln).astype(BF16), _w(wco_ref[...]))
    ret_out = _dot(gated_ref[...], _w(wro_ref[...]))

    merged = (jax.nn.sigmoid(proj(OFF_GR, D_MODEL)) * ret_out
              + jax.nn.sigmoid(proj(OFF_GC, D_MODEL)) * conv_out)
    mix = _dot(merged.astype(BF16), _w(wo_ref[...]))
    x1_ref[0] = x_ref[0] + gt1 * _rms(mix, gpost_ref[...])


def _slab_spec(total_rows, cols, steps, step_of, packed):
    rows = next(r for r in range(BF16_TILE_ROWS, total_rows + 1, BF16_TILE_ROWS)
                if total_rows % r == 0 and total_rows // r <= steps)
    last = total_rows // rows - 1
    return pl.BlockSpec((rows // 2 if packed else rows, cols),
                        lambda *idx: (jnp.minimum(step_of(*idx), last), 0))


def _pack_slabs(srcs, dsts):
    for src, dst in zip(srcs, dsts):
        dst[...] = pltpu.bitcast(src[...].astype(BF16), jnp.uint32)


def _prompt_mixer(x, mod_p, g_pre1, g_post1, cos_p, sin_p, decay, rs_tab, w_in_b, conv_w, conv_b,
                  ln_g, ln_b, w_conv_out_b, gn_g, w_ret_out_b, w_out_b, w_gate, w_up, w_down):
    B, L, _ = x.shape
    T = T_MIX
    nt = L // T
    d_ff = w_gate.shape[1]
    row = lambda b, t: (0, 0)
    step_of = lambda b, t: b * nt + t
    slabs = [(D_MODEL, d_ff), (D_MODEL, d_ff), (d_ff, D_MODEL)]
    in_specs = [
        pl.BlockSpec((1, T, D_MODEL), lambda b, t: (b, t, 0)),
        pl.BlockSpec((1, 1, 6 * D_MODEL), lambda b, t: (b, 0, 0)),
        pl.BlockSpec((1, D_MODEL), row),
        pl.BlockSpec((1, D_MODEL), row),
        pl.BlockSpec((T, HALF), lambda b, t: (t, 0)),
        pl.BlockSpec((T, HALF), lambda b, t: (t, 0)),
        _const_spec((N_HEADS, CHUNK, CHUNK)),
        _const_spec((CHUNK, LANES)),
        _const_spec((D_MODEL // 2, D_IN)),
        pl.BlockSpec((CONV_WIDTH, D_MODEL), row),
        pl.BlockSpec((1, D_MODEL), row),
        pl.BlockSpec((1, D_MODEL), row),
        pl.BlockSpec((1, D_MODEL), row),
        _const_spec((D_MODEL // 2, D_MODEL)),
        pl.BlockSpec((1, RET_V), row),
        _const_spec((RET_V // 2, D_MODEL)),
        _const_spec((D_MODEL // 2, D_MODEL)),
    ] + [_slab_spec(r, c, B * nt, step_of, packed=False) for r, c in slabs]
    out_specs = [
        pl.BlockSpec((1, T, D_MODEL), lambda b, t: (b, t, 0)),
        pl.BlockSpec((1, N_HEADS, DK, DV), lambda b, t: (b, 0, 0, 0)),
        pl.BlockSpec((1, CONV_STATE, D_MODEL), lambda b, t: (b, 0, 0)),
    ] + [_slab_spec(r, c, B * nt, step_of, packed=True) for r, c in slabs]
    out_shape = [
        jax.ShapeDtypeStruct((B, L, D_MODEL), F32),
        jax.ShapeDtypeStruct((B, N_HEADS, DK, DV), F32),
        jax.ShapeDtypeStruct((B, CONV_STATE, D_MODEL), F32),
    ] + [jax.ShapeDtypeStruct((r // 2, c), jnp.uint32) for r, c in slabs]
    return pl.pallas_call(
        _mixer_kernel,
        grid=(B, L // T),
        in_specs=in_specs,
        out_specs=out_specs,
        out_shape=out_shape,
        scratch_shapes=[
            pltpu.VMEM((D_MODEL // LANES, HIST + T, LANES), F32),
            pltpu.VMEM((D_MODEL // LANES, T, LANES), F32),
            pltpu.VMEM((T, RET_V), BF16),
        ],
        compiler_params=pltpu.CompilerParams(
            dimension_semantics=("arbitrary", "arbitrary"),
            vmem_limit_bytes=VMEM_LIMIT),
        name="prompt_mixer",
    )(x, mod_p, g_pre1, g_post1, cos_p, sin_p, decay, rs_tab, w_in_b, conv_w, conv_b,
      ln_g, ln_b, w_conv_out_b, gn_g, w_ret_out_b, w_out_b, w_gate, w_up, w_down)


def _ffn_body(x, sh2, sc2, gt2, gpre, gpost, wg_ref, wu_ref, wd_ref):
    hb = (_rms(x, gpre) * (1.0 + sc2) + sh2).astype(BF16)
    act = (jax.nn.silu(_dot(hb, _w(wg_ref[...]))) * _dot(hb, _w(wu_ref[...]))).astype(BF16)
    return x + gt2 * _rms(_dot(act, _w(wd_ref[...])), gpost)


def _sample_ret_step(q, k, v, cos, sin, gam_ref, s0_ref, snew_ref, y_ref, i):
    q = _rotary(q, cos, sin) * (DK ** -0.5)
    k = _rotary(k, cos, sin)
    qk = jnp.sum(q * k, axis=-1, keepdims=True)
    cols = jnp.concatenate([k, q, jnp.zeros((LANES - 2 * N_HEADS, DK), F32)], axis=0).T
    for h in range(N_HEADS):
        s0 = s0_ref[i, h]
        v_h = v[h:h + 1, :]
        gam = gam_ref[h:h + 1, :]
        snew_ref[i, h] = gam * s0 + cols[:, h:h + 1] * v_h
        qs = jnp.sum(cols[:, N_HEADS + h:N_HEADS + h + 1] * s0, axis=0, keepdims=True)
        y_ref[i, h:h + 1, :] = qk[h:h + 1, :] * v_h + gam * qs


def _ffn_ret_kernel(x_ref, mod_ref, gpre_ref, gpost_ref, wg_ref, wu_ref, wd_ref,
                    q_ref, k_ref, v_ref, cos_ref, sin_ref, gam_ref, s0_ref,
                    u_ref, a_ref, cst_ref, convw_ref, convb_ref,
                    o_ref, snew_ref, y_ref, cnew_ref, conv_ref, *, conv_steps):
    step = pl.program_id(0) * pl.num_programs(1) + pl.program_id(1)

    @pl.when(step < conv_steps)
    def _():
        _sample_conv_step(u_ref, a_ref, cst_ref, convw_ref, convb_ref, cnew_ref, conv_ref)

    mod = mod_ref[0]
    o_ref[0] = _ffn_body(x_ref[0], mod[:, 3 * D_MODEL:4 * D_MODEL], mod[:, 4 * D_MODEL:5 * D_MODEL],
                         mod[:, 5 * D_MODEL:6 * D_MODEL], gpre_ref[...], gpost_ref[...],
                         wg_ref, wu_ref, wd_ref)
    for i in range(q_ref.shape[0]):
        _sample_ret_step(q_ref[i], k_ref[i], v_ref[i], cos_ref[...], sin_ref[...], gam_ref,
                         s0_ref, snew_ref, y_ref, i)


def _prompt_ffn_sample_ret(x1, mod_p, g_pre2, g_post2, wg_b, wu_b, wd_b,
                           q_s, k_s, v_s, cos_s, sin_s, gam_tab, state,
                           z_s, state_conv_t, conv_w, conv_b):
    B, L, _ = x1.shape
    nb = q_s.shape[0]
    d_ff = wg_b.shape[1]
    nt = L // T_FFN
    steps = B * nt
    sb = nb // steps
    assert sb * steps == nb
    conv_steps = nb // CONV_B
    assert conv_steps * CONV_B == nb and conv_steps <= steps
    row = lambda b, t: (0, 0)
    blk = lambda b, t: (b * nt + t, 0, 0)
    cblk = lambda b, t: jnp.minimum(b * nt + t, conv_steps - 1)
    return pl.pallas_call(
        functools.partial(_ffn_ret_kernel, conv_steps=conv_steps),
        grid=(B, nt),
        in_specs=[
            pl.BlockSpec((1, T_FFN, D_MODEL), lambda b, t: (b, t, 0)),
            pl.BlockSpec((1, 1, 6 * D_MODEL), lambda b, t: (b, 0, 0)),
            pl.BlockSpec((1, D_MODEL), row),
            pl.BlockSpec((1, D_MODEL), row),
            _const_spec((D_MODEL // 2, d_ff)),
            _const_spec((D_MODEL // 2, d_ff)),
            _const_spec((d_ff // 2, D_MODEL)),
            pl.BlockSpec((sb, N_HEADS, DK), blk),
            pl.BlockSpec((sb, N_HEADS, DK), blk),
            pl.BlockSpec((sb, N_HEADS, DV), blk),
            pl.BlockSpec((1, HALF), row),
            pl.BlockSpec((1, HALF), row),
            pl.BlockSpec((N_HEADS, DV), row),
            pl.BlockSpec((sb, N_HEADS, DK, DV), lambda b, t: (b * nt + t, 0, 0, 0)),
            pl.BlockSpec((CONV_B, D_MODEL), lambda b, t: (cblk(b, t), OFF_U // D_MODEL)),
            pl.BlockSpec((CONV_B, D_MODEL), lambda b, t: (cblk(b, t), OFF_A // D_MODEL)),
            pl.BlockSpec((1, CONV_STATE, CONV_B, D_MODEL), lambda b, t: (0, 0, cblk(b, t), 0)),
            pl.BlockSpec((CONV_WIDTH, D_MODEL), row),
            pl.BlockSpec((1, D_MODEL), row),
        ],
        out_specs=[
            pl.BlockSpec((1, T_FFN, D_MODEL), lambda b, t: (b, t, 0)),
            pl.BlockSpec((sb, N_HEADS, DK, DV), lambda b, t: (b * nt + t, 0, 0, 0)),
            pl.BlockSpec((sb, N_HEADS, DV), blk),
            pl.BlockSpec((1, CONV_STATE, CONV_B, D_MODEL), lambda b, t: (0, 0, cblk(b, t), 0)),
            pl.BlockSpec((CONV_B, D_MODEL), lambda b, t: (cblk(b, t), 0)),
        ],
        out_shape=[
            jax.ShapeDtypeStruct((B, L, D_MODEL), F32),
            jax.ShapeDtypeStruct((nb, N_HEADS, DK, DV), F32),
            jax.ShapeDtypeStruct((nb, N_HEADS, DV), F32),
            jax.ShapeDtypeStruct((1, CONV_STATE, nb, D_MODEL), F32),
            jax.ShapeDtypeStruct((nb, D_MODEL), F32),
        ],
        compiler_params=pltpu.CompilerParams(
            dimension_semantics=("arbitrary", "arbitrary"),
            vmem_limit_bytes=VMEM_LIMIT),
        name="prompt_ffn_sample_ret",
    )(x1, mod_p, g_pre2, g_post2, wg_b, wu_b, wd_b, q_s, k_s, v_s, cos_s, sin_s, gam_tab, state,
      z_s, z_s, state_conv_t, conv_w, conv_b)


def _sample_in_kernel(x_ref, sh_ref, sc_ref, gpre_ref, w_ref, wro_f_ref, wco_f_ref, wo_f_ref,
                      z_ref, wp_ref, wro_p_ref, wco_p_ref, wo_p_ref):
    hb = (_rms(x_ref[...], gpre_ref[...]) * (1.0 + sc_ref[...]) + sh_ref[...]).astype(BF16)
    wb = w_ref[...].astype(BF16)
    z_ref[...] = _dot(hb, wb)
    wp_ref[...] = pltpu.bitcast(wb, jnp.uint32)
    _pack_slabs((wro_f_ref, wco_f_ref, wo_f_ref), (wro_p_ref, wco_p_ref, wo_p_ref))


def _sample_in(xs, mod, g_pre1, w_in, w_ret_out, w_conv_out, w_out):
    rows = xs.shape[0]
    tn = D_MODEL
    steps = D_IN // tn
    slabs = [w_ret_out.shape, w_conv_out.shape, w_out.shape]
    step_of = lambda j: j
    return pl.pallas_call(
        _sample_in_kernel,
        grid=(steps,),
        in_specs=[
            pl.BlockSpec((rows, D_MODEL), lambda j: (0, 0)),
            pl.BlockSpec((rows, D_MODEL), lambda j: (0, 0)),
            pl.BlockSpec((rows, D_MODEL), lambda j: (0, 1)),
            pl.BlockSpec((1, D_MODEL), lambda j: (0, 0)),
            pl.BlockSpec((D_MODEL, tn), lambda j: (0, j)),
        ] + [_slab_spec(r, c, steps, step_of, packed=False) for r, c in slabs],
        out_specs=[
            pl.BlockSpec((rows, tn), lambda j: (0, j)),
            pl.BlockSpec((D_MODEL // 2, tn), lambda j: (0, j)),
        ] + [_slab_spec(r, c, steps, step_of, packed=True) for r, c in slabs],
        out_shape=[
            jax.ShapeDtypeStruct((rows, D_IN), F32),
            jax.ShapeDtypeStruct((D_MODEL // 2, D_IN), jnp.uint32),
        ] + [jax.ShapeDtypeStruct((r // 2, c), jnp.uint32) for r, c in slabs],
        compiler_params=pltpu.CompilerParams(dimension_semantics=("arbitrary",)),
        name="sample_in_proj",
    )(xs, mod, mod, g_pre1, w_in, w_ret_out, w_conv_out, w_out)


def _sample_conv_step(u_ref, a_ref, st_ref, w_ref, b_ref, new_ref, conv_ref):
    glu = u_ref[...] * jax.nn.sigmoid(a_ref[...])
    acc = glu * w_ref[CONV_STATE:CONV_WIDTH, :] + b_ref[...]
    for j in range(CONV_STATE):
        acc = acc + st_ref[0, j] * w_ref[j:j + 1, :]
    conv_ref[...] = acc
    for j in range(CONV_STATE - 1):
        new_ref[0, j] = st_ref[0, j + 1]
    new_ref[0, CONV_STATE - 1] = glu


def _sample_post_kernel(x_ref, mod_ref, z_ref, y_ref, conv_ref, gng_ref, wro_ref, lng_ref, lnb_ref,
                        wco_ref, wo_ref, gpost1_ref, gpre2_ref, gpost2_ref, wg_ref, wu_ref, wd_ref, o_ref):
    x = x_ref[...]
    rows = x.shape[0]

    def mod(i):
        return mod_ref[:, i * D_MODEL:(i + 1) * D_MODEL]

    ret_out = jnp.zeros((rows, D_MODEL), F32)
    for h in range(N_HEADS):
        sl = slice(h * DV, (h + 1) * DV)
        yn = _standardize(y_ref[:, sl]) * gng_ref[:, sl]
        gated = (jax.nn.silu(z_ref[:, OFF_G + h * DV:OFF_G + (h + 1) * DV]) * yn).astype(BF16)
        ret_out = ret_out + _dot(gated, _w(wro_ref[h * DV // 2:(h + 1) * DV // 2, :]))
    ln = _standardize(conv_ref[...]) * lng_ref[...] + lnb_ref[...]
    conv_out = _dot(jax.nn.silu(ln).astype(BF16), _w(wco_ref[...]))
    merged = (jax.nn.sigmoid(z_ref[:, OFF_GR:OFF_GR + D_MODEL]) * ret_out
              + jax.nn.sigmoid(z_ref[:, OFF_GC:OFF_GC + D_MODEL]) * conv_out)
    mix = _dot(merged.astype(BF16), _w(wo_ref[...]))
    x1 = x + mod(2) * _rms(mix, gpost1_ref[...])
    o_ref[...] = _ffn_body(x1, mod(3), mod(4), mod(5), gpre2_ref[...], gpost2_ref[...],
                           wg_ref, wu_ref, wd_ref)


def _sample_post(xs, mod_s, z_s, y_s, conv_s, gn_g, w_ret_out_b, ln_g, ln_b, w_conv_out_b, w_out_b,
                 g_post1, g_pre2, g_post2, wg_b, wu_b, wd_b):
    return pl.pallas_call(
        _sample_post_kernel,
        out_shape=jax.ShapeDtypeStruct(xs.shape, F32),
        compiler_params=pltpu.CompilerParams(vmem_limit_bytes=VMEM_LIMIT),
        name="sample_post",
    )(xs, mod_s, z_s, y_s, conv_s, gn_g, w_ret_out_b, ln_g, ln_b, w_conv_out_b, w_out_b,
      g_post1, g_pre2, g_post2, wg_b, wu_b, wd_b)


def _rope_tables(pos):
    inv_freq = ROPE_BASE ** (-np.arange(0, DK, 2, dtype=np.float64) / DK)
    ang = np.asarray(pos, np.float64)[:, None] * inv_freq[None, :]
    return jnp.asarray(np.cos(ang), F32), jnp.asarray(np.sin(ang), F32)


def _log_gamma():
    return np.log(1.0 - np.exp(np.linspace(np.log(1.0 / 32.0), np.log(1.0 / 512.0), N_HEADS)))


def _decay_tables(chunk):
    lg = _log_gamma()
    idx = np.arange(chunk, dtype=np.float64)
    diff = idx[:, None] - idx[None, :]
    decay = np.where(diff >= 0, np.exp(lg[:, None, None] * np.maximum(diff, 0.0)), 0.0)
    cross = np.exp(lg[None, :] * (idx[:, None] + 1.0))
    k_dec = np.exp(lg[None, :] * (chunk - 1.0 - idx[:, None]))
    full = np.broadcast_to(np.exp(lg * chunk)[None, :], (chunk, N_HEADS))
    rs = np.concatenate([cross, k_dec, full], axis=1)
    rs = np.pad(rs, ((0, 0), (0, LANES - rs.shape[1])))
    return jnp.asarray(decay, F32), jnp.asarray(rs, F32)


def kernel(x_prompt, x_sample, c_prompt, c_sample, state_ret, state_conv, w_in, w_ada, b_ada, g_pre1, g_post1, g_pre2, g_post2, conv_w, conv_b, conv_ln_g, conv_ln_b, w_conv_out, ret_gn_g, w_ret_out, w_out, w_ffn_gate, w_ffn_up, w_ffn_down):
    depth = w_in.shape[0]
    assert depth == 1, "single-layer step"
    B, L, _ = x_prompt.shape
    nb = x_sample.shape[0]

    cos_p, sin_p = _rope_tables(np.arange(L))
    cos_s, sin_s = _rope_tables(PAST_LEN + np.arange(x_sample.shape[1]))
    decay, rs_tab = _decay_tables(CHUNK)
    gam_tab = jnp.asarray(np.broadcast_to(np.exp(_log_gamma())[:, None], (N_HEADS, DV)), F32)

    mod, mod_p = _modulation(c_sample, c_prompt, w_ada[0], b_ada)
    mod_p = mod_p.reshape(B, 1, 6 * D_MODEL)

    xs = x_sample.reshape(nb, D_MODEL)
    z_s, w_in_b, w_ret_out_b, w_conv_out_b, w_out_b = _sample_in(
        xs, mod, g_pre1, w_in[0], w_ret_out[0], w_conv_out[0], w_out[0])
    q_s = z_s[:, OFF_Q:OFF_Q + RET_QK].reshape(nb, N_HEADS, DK)
    k_s = z_s[:, OFF_K:OFF_K + RET_QK].reshape(nb, N_HEADS, DK)
    v_s = z_s[:, OFF_V:OFF_V + RET_V].reshape(nb, N_HEADS, DV)

    x1_p, ret_p, conv_p, wg_b, wu_b, wd_b = _prompt_mixer(
        x_prompt, mod_p, g_pre1, g_post1, cos_p, sin_p, decay, rs_tab, w_in_b, conv_w[0], conv_b,
        conv_ln_g, conv_ln_b, w_conv_out_b, ret_gn_g, w_ret_out_b, w_out_b,
        w_ffn_gate[0], w_ffn_up[0], w_ffn_down[0])
    y_p, ret_s, yr_s, conv_new_t, conv_s = _prompt_ffn_sample_ret(
        x1_p, mod_p, g_pre2, g_post2, wg_b, wu_b, wd_b, q_s, k_s, v_s, cos_s, sin_s, gam_tab, state_ret[0],
        z_s, state_conv.transpose(0, 2, 1, 3), conv_w[0], conv_b)
    conv_new_s = conv_new_t.transpose(0, 2, 1, 3)
    y_s = _sample_post(xs, mod, z_s, yr_s.reshape(nb, RET_V), conv_s, ret_gn_g, w_ret_out_b,
                       conv_ln_g, conv_ln_b, w_conv_out_b, w_out_b, g_post1, g_pre2, g_post2,
                       wg_b, wu_b, wd_b)

    return (y_p, y_s.reshape(x_sample.shape), ret_p[None], ret_s[None], conv_p[None], conv_new_s)
```

```python
import functools

import numpy as np

import jax
import jax.numpy as jnp
from jax import lax
from jax.experimental import pallas as pl
from jax.experimental.pallas import tpu as pltpu

F32 = jnp.float32
BF16 = jnp.bfloat16

D_MODEL = 1024
N_HEADS = 4
DK = 256
DV = 512
HALF = DK // 2
RET_QK = N_HEADS * DK
RET_V = N_HEADS * DV
CONV_WIDTH = 31
CONV_STATE = CONV_WIDTH - 1
EPS = 1e-6
ROPE_BASE = 10000.0
PAST_LEN = 16384

OFF_Q = 0
OFF_K = RET_QK
OFF_V = 2 * RET_QK
OFF_G = OFF_V + RET_V
OFF_U = OFF_G + RET_V
OFF_A = OFF_U + D_MODEL
OFF_GR = OFF_A + D_MODEL
OFF_GC = OFF_GR + D_MODEL
D_IN = OFF_GC + D_MODEL

LANES = 128
BF16_TILE_ROWS = 16
T_MIX = 256
CHUNK = 256
HIST = 32
ROW_CHUNK = 64
T_FFN = 256
CONV_B = 8
MIB = 1024 * 1024
V7X_VMEM_BYTES = 64 * MIB
VMEM_LIMIT = V7X_VMEM_BYTES - 8 * MIB


def _rms(x, g):
    ms = jnp.mean(x * x, axis=-1, keepdims=True)
    return x * lax.rsqrt(ms + EPS) * g


def _rotary(t, cos, sin):
    t1, t2 = t[:, :HALF], t[:, HALF:]
    return jnp.concatenate([t1 * cos - t2 * sin, t2 * cos + t1 * sin], axis=-1)


def _standardize(y):
    mu = jnp.mean(y, axis=-1, keepdims=True)
    yc = y - mu
    var = jnp.mean(yc * yc, axis=-1, keepdims=True)
    return yc * lax.rsqrt(var + EPS)


def _dot(a, b):
    return jnp.dot(a, b, preferred_element_type=F32)


def _w(packed):
    return pltpu.bitcast(packed, BF16)


def _const_spec(shape):
    return pl.BlockSpec(shape, lambda *_: (0,) * len(shape), pipeline_mode=pl.Buffered(1))


def _mod_kernel(cs_ref, cp_ref, w_ref, b_ref, os_ref, op_ref):
    wb = w_ref[...].astype(BF16)
    for c_ref, o_ref in ((cs_ref, os_ref), (cp_ref, op_ref)):
        o_ref[...] = _dot(jax.nn.silu(c_ref[...]).astype(BF16), wb) + b_ref[...]


def _modulation(c_sample, c_prompt, w_ada, b_ada):
    n_out = w_ada.shape[1]
    tn = D_MODEL
    groups = (c_sample.shape[0], c_prompt.shape[0])
    return pl.pallas_call(
        _mod_kernel,
        grid=(n_out // tn,),
        in_specs=[pl.BlockSpec((rows, D_MODEL), lambda j: (0, 0)) for rows in groups] + [
            pl.BlockSpec((D_MODEL, tn), lambda j: (0, j)),
            pl.BlockSpec((1, tn), lambda j: (0, j)),
        ],
        out_specs=[pl.BlockSpec((rows, tn), lambda j: (0, j)) for rows in groups],
        out_shape=[jax.ShapeDtypeStruct((rows, n_out), F32) for rows in groups],
        compiler_params=pltpu.CompilerParams(dimension_semantics=("arbitrary",)),
        name="adaln_mod",
    )(c_sample, c_prompt, w_ada, b_ada)


def _mixer_kernel(x_ref, mod_ref, gpre_ref, gpost_ref, cos_ref, sin_ref, decay_ref, rs_ref,
                  w_in_ref, convw_ref, convb_ref, lng_ref, lnb_ref, wco_ref, gng_ref, wro_ref, wo_ref,
                  wg_f_ref, wu_f_ref, wd_f_ref,
                  x1_ref, rstate_ref, cstate_ref, wg_p_ref, wu_p_ref, wd_p_ref,
                  full_ref, conv_ref, gated_ref):
    t = pl.program_id(1)
    T = T_MIX
    n_col = D_MODEL // LANES

    _pack_slabs((wg_f_ref, wu_f_ref, wd_f_ref), (wg_p_ref, wu_p_ref, wd_p_ref))

    @pl.when(t == 0)
    def _():
        rstate_ref[...] = jnp.zeros_like(rstate_ref)
        full_ref[:, 0:HIST, :] = jnp.zeros((n_col, HIST, LANES), F32)

    mod = mod_ref[0]
    sh1 = mod[:, 0:D_MODEL]
    sc1 = mod[:, D_MODEL:2 * D_MODEL]
    gt1 = mod[:, 2 * D_MODEL:3 * D_MODEL]
    hb = (_rms(x_ref[0], gpre_ref[...]) * (1.0 + sc1) + sh1).astype(BF16)

    def proj(lo, width):
        return _dot(hb, _w(w_in_ref[:, lo:lo + width]))

    glu = proj(OFF_U, D_MODEL) * jax.nn.sigmoid(proj(OFF_A, D_MODEL))
    for c in range(n_col):
        full_ref[c, HIST:HIST + T, :] = glu[:, c * LANES:(c + 1) * LANES]

    @pl.when(t == pl.num_programs(1) - 1)
    def _():
        for c in range(n_col):
            cstate_ref[0, :, c * LANES:(c + 1) * LANES] = full_ref[c, HIST + T - CONV_STATE:HIST + T, :]

    def conv_block(c):
        lanes = slice(c * LANES, (c + 1) * LANES)
        for r0 in range(0, T, ROW_CHUNK):
            acc = jnp.broadcast_to(convb_ref[:, lanes], (ROW_CHUNK, LANES))
            for j in range(CONV_WIDTH):
                start = HIST - CONV_STATE + j + r0
                acc = acc + full_ref[c, start:start + ROW_CHUNK, :] * convw_ref[j:j + 1, lanes]
            conv_ref[c, r0:r0 + ROW_CHUNK, :] = acc

    assert T == CHUNK
    zq = proj(OFF_Q, RET_QK)
    zk = proj(OFF_K, RET_QK)
    zv = proj(OFF_V, RET_V).astype(BF16)
    hq = lambda h: slice(h * DK, (h + 1) * DK)
    hv = lambda h: slice(h * DV, (h + 1) * DV)
    qb, kb, kd = [], [], []
    for h in range(N_HEADS):
        q = _rotary(zq[:, hq(h)], cos_ref[...], sin_ref[...]) * (DK ** -0.5)
        k = _rotary(zk[:, hq(h)], cos_ref[...], sin_ref[...])
        qb.append(q.astype(BF16))
        kb.append(k.astype(BF16))
        kd.append((k * rs_ref[:, N_HEADS + h:N_HEADS + h + 1]).astype(BF16))
    sc = [(lax.dot_general(qb[h], kb[h], (((1,), (1,)), ((), ())), preferred_element_type=F32)
           * decay_ref[h]).astype(BF16) for h in range(N_HEADS)]
    for c in range(0, n_col // 2):
        conv_block(c)
    intra = [_dot(sc[h], zv[:, hv(h)]) for h in range(N_HEADS)]
    cross = [_dot(qb[h], rstate_ref[0, h].astype(BF16)) for h in range(N_HEADS)]
    ys = [intra[h] + cross[h] * rs_ref[:, h:h + 1] for h in range(N_HEADS)]
    for h in range(N_HEADS):
        upd = lax.dot_general(kd[h], zv[:, hv(h)], (((0,), (0,)), ((), ())), preferred_element_type=F32)
        rstate_ref[0, h] = rs_ref[0:1, 2 * N_HEADS + h:2 * N_HEADS + h + 1] * rstate_ref[0, h] + upd
    zg = proj(OFF_G, RET_V)
    for c in range(n_col // 2, n_col):
        conv_block(c)
    for h in range(N_HEADS):
        yn = _standardize(ys[h]) * gng_ref[:, hv(h)]
        gated_ref[:, hv(h)] = (jax.nn.silu(zg[:, hv(h)]) * yn).astype(BF16)

    full_ref[:, 0:HIST, :] = full_ref[:, T:T + HIST, :]
    conv = jnp.concatenate([conv_ref[c] for c in range(n_col)], axis=-1)
    ln = _standardize(conv) * lng_ref[...] + lnb_ref[...]
    conv_out = _dot(jax.nn.silu(ln).astype(BF16), _w(wco_ref[...]))
    ret_out = _dot(gated_ref[...], _w(wro_ref[...]))

    merged = (jax.nn.sigmoid(proj(OFF_GR, D_MODEL)) * ret_out
              + jax.nn.sigmoid(proj(OFF_GC, D_MODEL)) * conv_out)
    mix = _dot(merged.astype(BF16), _w(wo_ref[...]))
    x1_ref[0] = x_ref[0] + gt1 * _rms(mix, gpost_ref[...])


def _slab_spec(total_rows, cols, steps, step_of, packed):
    rows = next(r for r in range(BF16_TILE_ROWS, total_rows + 1, BF16_TILE_ROWS)
                if total_rows % r == 0 and total_rows // r <= steps)
    last = total_rows // rows - 1
    return pl.BlockSpec((rows // 2 if packed else rows, cols),
                        lambda *idx: (jnp.minimum(step_of(*idx), last), 0))


def _pack_slabs(srcs, dsts):
    for src, dst in zip(srcs, dsts):
        dst[...] = pltpu.bitcast(src[...].astype(BF16), jnp.uint32)


def _prompt_mixer(x, mod_p, g_pre1, g_post1, cos_p, sin_p, decay, rs_tab, w_in_b, conv_w, conv_b,
                  ln_g, ln_b, w_conv_out_b, gn_g, w_ret_out_b, w_out_b, w_gate, w_up, w_down):
    B, L, _ = x.shape
    T = T_MIX
    nt = L // T
    d_ff = w_gate.shape[1]
    row = lambda b, t: (0, 0)
    step_of = lambda b, t: b * nt + t
    slabs = [(D_MODEL, d_ff), (D_MODEL, d_ff), (d_ff, D_MODEL)]
    in_specs = [
        pl.BlockSpec((1, T, D_MODEL), lambda b, t: (b, t, 0)),
        pl.BlockSpec((1, 1, 6 * D_MODEL), lambda b, t: (b, 0, 0)),
        pl.BlockSpec((1, D_MODEL), row),
        pl.BlockSpec((1, D_MODEL), row),
        pl.BlockSpec((T, HALF), lambda b, t: (t, 0)),
        pl.BlockSpec((T, HALF), lambda b, t: (t, 0)),
        _const_spec((N_HEADS, CHUNK, CHUNK)),
        _const_spec((CHUNK, LANES)),
        _const_spec((D_MODEL // 2, D_IN)),
        pl.BlockSpec((CONV_WIDTH, D_MODEL), row),
        pl.BlockSpec((1, D_MODEL), row),
        pl.BlockSpec((1, D_MODEL), row),
        pl.BlockSpec((1, D_MODEL), row),
        _const_spec((D_MODEL // 2, D_MODEL)),
        pl.BlockSpec((1, RET_V), row),
        _const_spec((RET_V // 2, D_MODEL)),
        _const_spec((D_MODEL // 2, D_MODEL)),
    ] + [_slab_spec(r, c, B * nt, step_of, packed=False) for r, c in slabs]
    out_specs = [
        pl.BlockSpec((1, T, D_MODEL), lambda b, t: (b, t, 0)),
        pl.BlockSpec((1, N_HEADS, DK, DV), lambda b, t: (b, 0, 0, 0)),
        pl.BlockSpec((1, CONV_STATE, D_MODEL), lambda b, t: (b, 0, 0)),
    ] + [_slab_spec(r, c, B * nt, step_of, packed=True) for r, c in slabs]
    out_shape = [
        jax.ShapeDtypeStruct((B, L, D_MODEL), F32),
        jax.ShapeDtypeStruct((B, N_HEADS, DK, DV), F32),
        jax.ShapeDtypeStruct((B, CONV_STATE, D_MODEL), F32),
    ] + [jax.ShapeDtypeStruct((r // 2, c), jnp.uint32) for r, c in slabs]
    return pl.pallas_call(
        _mixer_kernel,
        grid=(B, L // T),
        in_specs=in_specs,
        out_specs=out_specs,
        out_shape=out_shape,
        scratch_shapes=[
            pltpu.VMEM((D_MODEL // LANES, HIST + T, LANES), F32),
            pltpu.VMEM((D_MODEL // LANES, T, LANES), F32),
            pltpu.VMEM((T, RET_V), BF16),
        ],
        compiler_params=pltpu.CompilerParams(
            dimension_semantics=("arbitrary", "arbitrary"),
            vmem_limit_bytes=VMEM_LIMIT),
        name="prompt_mixer",
    )(x, mod_p, g_pre1, g_post1, cos_p, sin_p, decay, rs_tab, w_in_b, conv_w, conv_b,
      ln_g, ln_b, w_conv_out_b, gn_g, w_ret_out_b, w_out_b, w_gate, w_up, w_down)


def _ffn_body(x, sh2, sc2, gt2, gpre, gpost, wg_ref, wu_ref, wd_ref):
    hb = (_rms(x, gpre) * (1.0 + sc2) + sh2).astype(BF16)
    act = (jax.nn.silu(_dot(hb, _w(wg_ref[...]))) * _dot(hb, _w(wu_ref[...]))).astype(BF16)
    return x + gt2 * _rms(_dot(act, _w(wd_ref[...])), gpost)


def _sample_ret_step(q, k, v, cos, sin, gam_ref, s0_ref, snew_ref, y_ref, i):
    q = _rotary(q, cos, sin) * (DK ** -0.5)
    k = _rotary(k, cos, sin)
    qk = jnp.sum(q * k, axis=-1, keepdims=True)
    cols = jnp.concatenate([k, q, jnp.zeros((LANES - 2 * N_HEADS, DK), F32)], axis=0).T
    for h in range(N_HEADS):
        s0 = s0_ref[i, h]
        v_h = v[h:h + 1, :]
        gam = gam_ref[h:h + 1, :]
        snew_ref[i, h] = gam * s0 + cols[:, h:h + 1] * v_h
        qs = jnp.sum(cols[:, N_HEADS + h:N_HEADS + h + 1] * s0, axis=0, keepdims=True)
        y_ref[i, h:h + 1, :] = qk[h:h + 1, :] * v_h + gam * qs


def _ffn_ret_kernel(x_ref, mod_ref, gpre_ref, gpost_ref, wg_ref, wu_ref, wd_ref,
                    q_ref, k_ref, v_ref, cos_ref, sin_ref, gam_ref, s0_ref,
                    u_ref, a_ref, cst_ref, convw_ref, convb_ref,
                    o_ref, snew_ref, y_ref, cnew_ref, conv_ref, *, conv_steps):
    step = pl.program_id(0) * pl.num_programs(1) + pl.program_id(1)

    @pl.when(step < conv_steps)
    def _():
        _sample_conv_step(u_ref, a_ref, cst_ref, convw_ref, convb_ref, cnew_ref, conv_ref)

    mod = mod_ref[0]
    o_ref[0] = _ffn_body(x_ref[0], mod[:, 3 * D_MODEL:4 * D_MODEL], mod[:, 4 * D_MODEL:5 * D_MODEL],
                         mod[:, 5 * D_MODEL:6 * D_MODEL], gpre_ref[...], gpost_ref[...],
                         wg_ref, wu_ref, wd_ref)
    for i in range(q_ref.shape[0]):
        _sample_ret_step(q_ref[i], k_ref[i], v_ref[i], cos_ref[...], sin_ref[...], gam_ref,
                         s0_ref, snew_ref, y_ref, i)


def _prompt_ffn_sample_ret(x1, mod_p, g_pre2, g_post2, wg_b, wu_b, wd_b,
                           q_s, k_s, v_s, cos_s, sin_s, gam_tab, state,
                           z_s, state_conv_t, conv_w, conv_b):
    B, L, _ = x1.shape
    nb = q_s.shape[0]
    d_ff = wg_b.shape[1]
    nt = L // T_FFN
    steps = B * nt
    sb = nb // steps
    assert sb * steps == nb
    conv_steps = nb // CONV_B
    assert conv_steps * CONV_B == nb and conv_steps <= steps
    row = lambda b, t: (0, 0)
    blk = lambda b, t: (b * nt + t, 0, 0)
    cblk = lambda b, t: jnp.minimum(b * nt + t, conv_steps - 1)
    return pl.pallas_call(
        functools.partial(_ffn_ret_kernel, conv_steps=conv_steps),
        grid=(B, nt),
        in_specs=[
            pl.BlockSpec((1, T_FFN, D_MODEL), lambda b, t: (b, t, 0)),
            pl.BlockSpec((1, 1, 6 * D_MODEL), lambda b, t: (b, 0, 0)),
            pl.BlockSpec((1, D_MODEL), row),
            pl.BlockSpec((1, D_MODEL), row),
            _const_spec((D_MODEL // 2, d_ff)),
            _const_spec((D_MODEL // 2, d_ff)),
            _const_spec((d_ff // 2, D_MODEL)),
            pl.BlockSpec((sb, N_HEADS, DK), blk),
            pl.BlockSpec((sb, N_HEADS, DK), blk),
            pl.BlockSpec((sb, N_HEADS, DV), blk),
            pl.BlockSpec((1, HALF), row),
            pl.BlockSpec((1, HALF), row),
            pl.BlockSpec((N_HEADS, DV), row),
            pl.BlockSpec((sb, N_HEADS, DK, DV), lambda b, t: (b * nt + t, 0, 0, 0)),
            pl.BlockSpec((CONV_B, D_MODEL), lambda b, t: (cblk(b, t), OFF_U // D_MODEL)),
            pl.BlockSpec((CONV_B, D_MODEL), lambda b, t: (cblk(b, t), OFF_A // D_MODEL)),
            pl.BlockSpec((1, CONV_STATE, CONV_B, D_MODEL), lambda b, t: (0, 0, cblk(b, t), 0)),
            pl.BlockSpec((CONV_WIDTH, D_MODEL), row),
            pl.BlockSpec((1, D_MODEL), row),
        ],
        out_specs=[
            pl.BlockSpec((1, T_FFN, D_MODEL), lambda b, t: (b, t, 0)),
            pl.BlockSpec((sb, N_HEADS, DK, DV), lambda b, t: (b * nt + t, 0, 0, 0)),
            pl.BlockSpec((sb, N_HEADS, DV), blk),
            pl.BlockSpec((1, CONV_STATE, CONV_B, D_MODEL), lambda b, t: (0, 0, cblk(b, t), 0)),
            pl.BlockSpec((CONV_B, D_MODEL), lambda b, t: (cblk(b, t), 0)),
        ],
        out_shape=[
            jax.ShapeDtypeStruct((B, L, D_MODEL), F32),
            jax.ShapeDtypeStruct((nb, N_HEADS, DK, DV), F32),
            jax.ShapeDtypeStruct((nb, N_HEADS, DV), F32),
            jax.ShapeDtypeStruct((1, CONV_STATE, nb, D_MODEL), F32),
            jax.ShapeDtypeStruct((nb, D_MODEL), F32),
        ],
        compiler_params=pltpu.CompilerParams(
            dimension_semantics=("arbitrary", "arbitrary"),
            vmem_limit_bytes=VMEM_LIMIT),
        name="prompt_ffn_sample_ret",
    )(x1, mod_p, g_pre2, g_post2, wg_b, wu_b, wd_b, q_s, k_s, v_s, cos_s, sin_s, gam_tab, state,
      z_s, z_s, state_conv_t, conv_w, conv_b)


def _sample_in_kernel(x_ref, sh_ref, sc_ref, gpre_ref, w_ref, wro_f_ref, wco_f_ref, wo_f_ref,
                      z_ref, wp_ref, wro_p_ref, wco_p_ref, wo_p_ref):
    hb = (_rms(x_ref[...], gpre_ref[...]) * (1.0 + sc_ref[...]) + sh_ref[...]).astype(BF16)
    wb = w_ref[...].astype(BF16)
    z_ref[...] = _dot(hb, wb)
    wp_ref[...] = pltpu.bitcast(wb, jnp.uint32)
    _pack_slabs((wro_f_ref, wco_f_ref, wo_f_ref), (wro_p_ref, wco_p_ref, wo_p_ref))


def _sample_in(xs, mod, g_pre1, w_in, w_ret_out, w_conv_out, w_out):
    rows = xs.shape[0]
    tn = D_MODEL
    steps = D_IN // tn
    slabs = [w_ret_out.shape, w_conv_out.shape, w_out.shape]
    step_of = lambda j: j
    return pl.pallas_call(
        _sample_in_kernel,
        grid=(steps,),
        in_specs=[
            pl.BlockSpec((rows, D_MODEL), lambda j: (0, 0)),
            pl.BlockSpec((rows, D_MODEL), lambda j: (0, 0)),
            pl.BlockSpec((rows, D_MODEL), lambda j: (0, 1)),
            pl.BlockSpec((1, D_MODEL), lambda j: (0, 0)),
            pl.BlockSpec((D_MODEL, tn), lambda j: (0, j)),
        ] + [_slab_spec(r, c, steps, step_of, packed=False) for r, c in slabs],
        out_specs=[
            pl.BlockSpec((rows, tn), lambda j: (0, j)),
            pl.BlockSpec((D_MODEL // 2, tn), lambda j: (0, j)),
        ] + [_slab_spec(r, c, steps, step_of, packed=True) for r, c in slabs],
        out_shape=[
            jax.ShapeDtypeStruct((rows, D_IN), F32),
            jax.ShapeDtypeStruct((D_MODEL // 2, D_IN), jnp.uint32),
        ] + [jax.ShapeDtypeStruct((r // 2, c), jnp.uint32) for r, c in slabs],
        compiler_params=pltpu.CompilerParams(dimension_semantics=("arbitrary",)),
        name="sample_in_proj",
    )(xs, mod, mod, g_pre1, w_in, w_ret_out, w_conv_out, w_out)


def _sample_conv_step(u_ref, a_ref, st_ref, w_ref, b_ref, new_ref, conv_ref):
    glu = u_ref[...] * jax.nn.sigmoid(a_ref[...])
    acc = glu * w_ref[CONV_STATE:CONV_WIDTH, :] + b_ref[...]
    for j in range(CONV_STATE):
        acc = acc + st_ref[0, j] * w_ref[j:j + 1, :]
    conv_ref[...] = acc
    for j in range(CONV_STATE - 1):
        new_ref[0, j] = st_ref[0, j + 1]
    new_ref[0, CONV_STATE - 1] = glu


def _sample_post_kernel(x_ref, mod_ref, z_ref, y_ref, conv_ref, gng_ref, wro_ref, lng_ref, lnb_ref,
                        wco_ref, wo_ref, gpost1_ref, gpre2_ref, gpost2_ref, wg_ref, wu_ref, wd_ref, o_ref):
    x = x_ref[...]
    rows = x.shape[0]

    def mod(i):
        return mod_ref[:, i * D_MODEL:(i + 1) * D_MODEL]

    ret_out = jnp.zeros((rows, D_MODEL), F32)
    for h in range(N_HEADS):
        sl = slice(h * DV, (h + 1) * DV)
        yn = _standardize(y_ref[:, sl]) * gng_ref[:, sl]
        gated = (jax.nn.silu(z_ref[:, OFF_G + h * DV:OFF_G + (h + 1) * DV]) * yn).astype(BF16)
        ret_out = ret_out + _dot(gated, _w(wro_ref[h * DV // 2:(h + 1) * DV // 2, :]))
    ln = _standardize(conv_ref[...]) * lng_ref[...] + lnb_ref[...]
    conv_out = _dot(jax.nn.silu(ln).astype(BF16), _w(wco_ref[...]))
    merged = (jax.nn.sigmoid(z_ref[:, OFF_GR:OFF_GR + D_MODEL]) * ret_out
              + jax.nn.sigmoid(z_ref[:, OFF_GC:OFF_GC + D_MODEL]) * conv_out)
    mix = _dot(merged.astype(BF16), _w(wo_ref[...]))
    x1 = x + mod(2) * _rms(mix, gpost1_ref[...])
    o_ref[...] = _ffn_body(x1, mod(3), mod(4), mod(5), gpre2_ref[...], gpost2_ref[...],
                           wg_ref, wu_ref, wd_ref)


def _sample_post(xs, mod_s, z_s, y_s, conv_s, gn_g, w_ret_out_b, ln_g, ln_b, w_conv_out_b, w_out_b,
                 g_post1, g_pre2, g_post2, wg_b, wu_b, wd_b):
    return pl.pallas_call(
        _sample_post_kernel,
        out_shape=jax.ShapeDtypeStruct(xs.shape, F32),
        compiler_params=pltpu.CompilerParams(vmem_limit_bytes=VMEM_LIMIT),
        name="sample_post",
    )(xs, mod_s, z_s, y_s, conv_s, gn_g, w_ret_out_b, ln_g, ln_b, w_conv_out_b, w_out_b,
      g_post1, g_pre2, g_post2, wg_b, wu_b, wd_b)


def _rope_tables(pos):
    inv_freq = ROPE_BASE ** (-np.arange(0, DK, 2, dtype=np.float64) / DK)
    ang = np.asarray(pos, np.float64)[:, None] * inv_freq[None, :]
    return jnp.asarray(np.cos(ang), F32), jnp.asarray(np.sin(ang), F32)


def _log_gamma():
    return np.log(1.0 - np.exp(np.linspace(np.log(1.0 / 32.0), np.log(1.0 / 512.0), N_HEADS)))


def _decay_tables(chunk):
    lg = _log_gamma()
    idx = np.arange(chunk, dtype=np.float64)
    diff = idx[:, None] - idx[None, :]
    decay = np.where(diff >= 0, np.exp(lg[:, None, None] * np.maximum(diff, 0.0)), 0.0)
    cross = np.exp(lg[None, :] * (idx[:, None] + 1.0))
    k_dec = np.exp(lg[None, :] * (chunk - 1.0 - idx[:, None]))
    full = np.broadcast_to(np.exp(lg * chunk)[None, :], (chunk, N_HEADS))
    rs = np.concatenate([cross, k_dec, full], axis=1)
    rs = np.pad(rs, ((0, 0), (0, LANES - rs.shape[1])))
    return jnp.asarray(decay, F32), jnp.asarray(rs, F32)


def kernel(x_prompt, x_sample, c_prompt, c_sample, state_ret, state_conv, w_in, w_ada, b_ada, g_pre1, g_post1, g_pre2, g_post2, conv_w, conv_b, conv_ln_g, conv_ln_b, w_conv_out, ret_gn_g, w_ret_out, w_out, w_ffn_gate, w_ffn_up, w_ffn_down):
    depth = w_in.shape[0]
    assert depth == 1, "single-layer step"
    B, L, _ = x_prompt.shape
    nb = x_sample.shape[0]

    cos_p, sin_p = _rope_tables(np.arange(L))
    cos_s, sin_s = _rope_tables(PAST_LEN + np.arange(x_sample.shape[1]))
    decay, rs_tab = _decay_tables(CHUNK)
    gam_tab = jnp.asarray(np.broadcast_to(np.exp(_log_gamma())[:, None], (N_HEADS, DV)), F32)

    mod, mod_p = _modulation(c_sample, c_prompt, w_ada[0], b_ada)
    mod_p = mod_p.reshape(B, 1, 6 * D_MODEL)

    xs = x_sample.reshape(nb, D_MODEL)
    z_s, w_in_b, w_ret_out_b, w_conv_out_b, w_out_b = _sample_in(
        xs, mod, g_pre1, w_in[0], w_ret_out[0], w_conv_out[0], w_out[0])
    q_s = z_s[:, OFF_Q:OFF_Q + RET_QK].reshape(nb, N_HEADS, DK)
    k_s = z_s[:, OFF_K:OFF_K + RET_QK].reshape(nb, N_HEADS, DK)
    v_s = z_s[:, OFF_V:OFF_V + RET_V].reshape(nb, N_HEADS, DV)

    x1_p, ret_p, conv_p, wg_b, wu_b, wd_b = _prompt_mixer(
        x_prompt, mod_p, g_pre1, g_post1, cos_p, sin_p, decay, rs_tab, w_in_b, conv_w[0], conv_b,
        conv_ln_g, conv_ln_b, w_conv_out_b, ret_gn_g, w_ret_out_b, w_out_b,
        w_ffn_gate[0], w_ffn_up[0], w_ffn_down[0])
    y_p, ret_s, yr_s, conv_new_t, conv_s = _prompt_ffn_sample_ret(
        x1_p, mod_p, g_pre2, g_post2, wg_b, wu_b, wd_b, q_s, k_s, v_s, cos_s, sin_s, gam_tab, state_ret[0],
        z_s, state_conv.transpose(0, 2, 1, 3), conv_w[0], conv_b)
    conv_new_s = conv_new_t.transpose(0, 2, 1, 3)
    y_s = _sample_post(xs, mod, z_s, yr_s.reshape(nb, RET_V), conv_s, ret_gn_g, w_ret_out_b,
                       conv_ln_g, conv_ln_b, w_conv_out_b, w_out_b, g_post1, g_pre2, g_post2,
                       wg_b, wu_b, wd_b)

    return (y_p, y_s.reshape(x_sample.shape), ret_p[None], ret_s[None], conv_p[None], conv_new_s)
```

```python
import functools

import numpy as np

import jax
import jax.numpy as jnp
from jax import lax
from jax.experimental import pallas as pl
from jax.experimental.pallas import tpu as pltpu

F32 = jnp.float32
BF16 = jnp.bfloat16

D_MODEL = 1024
N_HEADS = 4
DK = 256
DV = 512
HALF = DK // 2
RET_QK = N_HEADS * DK
RET_V = N_HEADS * DV
CONV_WIDTH = 31
CONV_STATE = CONV_WIDTH - 1
EPS = 1e-6
ROPE_BASE = 10000.0
PAST_LEN = 16384

OFF_Q = 0
OFF_K = RET_QK
OFF_V = 2 * RET_QK
OFF_G = OFF_V + RET_V
OFF_U = OFF_G + RET_V
OFF_A = OFF_U + D_MODEL
OFF_GR = OFF_A + D_MODEL
OFF_GC = OFF_GR + D_MODEL
D_IN = OFF_GC + D_MODEL

LANES = 128
BF16_TILE_ROWS = 16
T_MIX = 256
CHUNK = 256
HIST = 32
ROW_CHUNK = 64
T_FFN = 256
CONV_B = 8
MIB = 1024 * 1024
V7X_VMEM_BYTES = 64 * MIB
VMEM_LIMIT = V7X_VMEM_BYTES - 8 * MIB


def _rms(x, g):
    ms = jnp.mean(x * x, axis=-1, keepdims=True)
    return x * lax.rsqrt(ms + EPS) * g


def _rotary(t, cos, sin):
    t1, t2 = t[:, :HALF], t[:, HALF:]
    return jnp.concatenate([t1 * cos - t2 * sin, t2 * cos + t1 * sin], axis=-1)


def _standardize(y):
    mu = jnp.mean(y, axis=-1, keepdims=True)
    yc = y - mu
    var = jnp.mean(yc * yc, axis=-1, keepdims=True)
    return yc * lax.rsqrt(var + EPS)


def _dot(a, b):
    return jnp.dot(a, b, preferred_element_type=F32)


def _w(packed):
    return pltpu.bitcast(packed, BF16)


def _const_spec(shape):
    return pl.BlockSpec(shape, lambda *_: (0,) * len(shape), pipeline_mode=pl.Buffered(1))


def _mod_kernel(cs_ref, cp_ref, w_ref, b_ref, os_ref, op_ref):
    wb = w_ref[...].astype(BF16)
    for c_ref, o_ref in ((cs_ref, os_ref), (cp_ref, op_ref)):
        o_ref[...] = _dot(jax.nn.silu(c_ref[...]).astype(BF16), wb) + b_ref[...]


def _modulation(c_sample, c_prompt, w_ada, b_ada):
    n_out = w_ada.shape[1]
    tn = D_MODEL
    groups = (c_sample.shape[0], c_prompt.shape[0])
    return pl.pallas_call(
        _mod_kernel,
        grid=(n_out // tn,),
        in_specs=[pl.BlockSpec((rows, D_MODEL), lambda j: (0, 0)) for rows in groups] + [
            pl.BlockSpec((D_MODEL, tn), lambda j: (0, j)),
            pl.BlockSpec((1, tn), lambda j: (0, j)),
        ],
        out_specs=[pl.BlockSpec((rows, tn), lambda j: (0, j)) for rows in groups],
        out_shape=[jax.ShapeDtypeStruct((rows, n_out), F32) for rows in groups],
        compiler_params=pltpu.CompilerParams(dimension_semantics=("arbitrary",)),
        name="adaln_mod",
    )(c_sample, c_prompt, w_ada, b_ada)


def _mixer_kernel(x_ref, mod_ref, gpre_ref, gpost_ref, cos_ref, sin_ref, decay_ref, rs_ref,
                  w_in_ref, convw_ref, convb_ref, lng_ref, lnb_ref, wco_ref, gng_ref, wro_ref, wo_ref,
                  wg_f_ref, wu_f_ref, wd_f_ref,
                  x1_ref, rstate_ref, cstate_ref, wg_p_ref, wu_p_ref, wd_p_ref,
                  full_ref, conv_ref, gated_ref):
    t = pl.program_id(1)
    T = T_MIX
    n_col = D_MODEL // LANES

    _pack_slabs((wg_f_ref, wu_f_ref, wd_f_ref), (wg_p_ref, wu_p_ref, wd_p_ref))

    @pl.when(t == 0)
    def _():
        rstate_ref[...] = jnp.zeros_like(rstate_ref)
        full_ref[:, 0:HIST, :] = jnp.zeros((n_col, HIST, LANES), F32)

    mod = mod_ref[0]
    sh1 = mod[:, 0:D_MODEL]
    sc1 = mod[:, D_MODEL:2 * D_MODEL]
    gt1 = mod[:, 2 * D_MODEL:3 * D_MODEL]
    hb = (_rms(x_ref[0], gpre_ref[...]) * (1.0 + sc1) + sh1).astype(BF16)

    def proj(lo, width):
        return _dot(hb, _w(w_in_ref[:, lo:lo + width]))

    glu = proj(OFF_U, D_MODEL) * jax.nn.sigmoid(proj(OFF_A, D_MODEL))
    for c in range(n_col):
        full_ref[c, HIST:HIST + T, :] = glu[:, c * LANES:(c + 1) * LANES]

    @pl.when(t == pl.num_programs(1) - 1)
    def _():
        for c in range(n_col):
            cstate_ref[0, :, c * LANES:(c + 1) * LANES] = full_ref[c, HIST + T - CONV_STATE:HIST + T, :]

    def conv_block(c):
        lanes = slice(c * LANES, (c + 1) * LANES)
        for r0 in range(0, T, ROW_CHUNK):
            acc = jnp.broadcast_to(convb_ref[:, lanes], (ROW_CHUNK, LANES))
            for j in range(CONV_WIDTH):
                start = HIST - CONV_STATE + j + r0
                acc = acc + full_ref[c, start:start + ROW_CHUNK, :] * convw_ref[j:j + 1, lanes]
            conv_ref[c, r0:r0 + ROW_CHUNK, :] = acc

    assert T == CHUNK
    zq = proj(OFF_Q, RET_QK)
    zk = proj(OFF_K, RET_QK)
    zv = proj(OFF_V, RET_V).astype(BF16)
    hq = lambda h: slice(h * DK, (h + 1) * DK)
    hv = lambda h: slice(h * DV, (h + 1) * DV)
    qb, qc, kb, kd = [], [], [], []
    for h in range(N_HEADS):
        q = _rotary(zq[:, hq(h)], cos_ref[...], sin_ref[...]) * (DK ** -0.5)
        k = _rotary(zk[:, hq(h)], cos_ref[...], sin_ref[...])
        qb.append(q.astype(BF16))
        qc.append((q * rs_ref[:, h:h + 1]).astype(BF16))
        kb.append(k.astype(BF16))
        kd.append((k * rs_ref[:, N_HEADS + h:N_HEADS + h + 1]).astype(BF16))
    sc = [(lax.dot_general(qb[h], kb[h], (((1,), (1,)), ((), ())), preferred_element_type=F32)
           * decay_ref[h]).astype(BF16) for h in range(N_HEADS)]
    for c in range(0, n_col // 2):
        conv_block(c)
    ys = []
    for h in range(N_HEADS):
        s_prev = rstate_ref[0, h]
        y = _dot(jnp.concatenate([sc[h], qc[h]], axis=-1),
                 jnp.concatenate([zv[:, hv(h)], s_prev.astype(BF16)], axis=0))
        upd = lax.dot_general(kd[h], zv[:, hv(h)], (((0,), (0,)), ((), ())), preferred_element_type=F32)
        rstate_ref[0, h] = rs_ref[0:1, 2 * N_HEADS + h:2 * N_HEADS + h + 1] * s_prev + upd
        ys.append(y)
    zg = proj(OFF_G, RET_V)
    for c in range(n_col // 2, n_col):
        conv_block(c)
    for h in range(N_HEADS):
        yn = _standardize(ys[h]) * gng_ref[:, hv(h)]
        gated_ref[:, hv(h)] = (jax.nn.silu(zg[:, hv(h)]) * yn).astype(BF16)

    full_ref[:, 0:HIST, :] = full_ref[:, T:T + HIST, :]
    conv = jnp.concatenate([conv_ref[c] for c in range(n_col)], axis=-1)
    ln = _standardize(conv) * lng_ref[...] + lnb_ref[...]
    conv_out = _dot(jax.nn.silu(ln).astype(BF16), _w(wco_ref[...]))
    ret_out = _dot(gated_ref[...], _w(wro_ref[...]))

    merged = (jax.nn.sigmoid(proj(OFF_GR, D_MODEL)) * ret_out
              + jax.nn.sigmoid(proj(OFF_GC, D_MODEL)) * conv_out)
    mix = _dot(merged.astype(BF16), _w(wo_ref[...]))
    x1_ref[0] = x_ref[0] + gt1 * _rms(mix, gpost_ref[...])


def _slab_spec(total_rows, cols, steps, step_of, packed):
    rows = next(r for r in range(BF16_TILE_ROWS, total_rows + 1, BF16_TILE_ROWS)
                if total_rows % r == 0 and total_rows // r <= steps)
    last = total_rows // rows - 1
    return pl.BlockSpec((rows // 2 if packed else rows, cols),
                        lambda *idx: (jnp.minimum(step_of(*idx), last), 0))


def _pack_slabs(srcs, dsts):
    for src, dst in zip(srcs, dsts):
        dst[...] = pltpu.bitcast(src[...].astype(BF16), jnp.uint32)


def _prompt_mixer(x, mod_p, g_pre1, g_post1, cos_p, sin_p, decay, rs_tab, w_in_b, conv_w, conv_b,
                  ln_g, ln_b, w_conv_out_b, gn_g, w_ret_out_b, w_out_b, w_gate, w_up, w_down):
    B, L, _ = x.shape
    T = T_MIX
    nt = L // T
    d_ff = w_gate.shape[1]
    row = lambda b, t: (0, 0)
    step_of = lambda b, t: b * nt + t
    slabs = [(D_MODEL, d_ff), (D_MODEL, d_ff), (d_ff, D_MODEL)]
    in_specs = [
        pl.BlockSpec((1, T, D_MODEL), lambda b, t: (b, t, 0)),
        pl.BlockSpec((1, 1, 6 * D_MODEL), lambda b, t: (b, 0, 0)),
        pl.BlockSpec((1, D_MODEL), row),
        pl.BlockSpec((1, D_MODEL), row),
        pl.BlockSpec((T, HALF), lambda b, t: (t, 0)),
        pl.BlockSpec((T, HALF), lambda b, t: (t, 0)),
        _const_spec((N_HEADS, CHUNK, CHUNK)),
        _const_spec((CHUNK, LANES)),
        _const_spec((D_MODEL // 2, D_IN)),
        pl.BlockSpec((CONV_WIDTH, D_MODEL), row),
        pl.BlockSpec((1, D_MODEL), row),
        pl.BlockSpec((1, D_MODEL), row),
        pl.BlockSpec((1, D_MODEL), row),
        _const_spec((D_MODEL // 2, D_MODEL)),
        pl.BlockSpec((1, RET_V), row),
        _const_spec((RET_V // 2, D_MODEL)),
        _const_spec((D_MODEL // 2, D_MODEL)),
    ] + [_slab_spec(r, c, B * nt, step_of, packed=False) for r, c in slabs]
    out_specs = [
        pl.BlockSpec((1, T, D_MODEL), lambda b, t: (b, t, 0)),
        pl.BlockSpec((1, N_HEADS, DK, DV), lambda b, t: (b, 0, 0, 0)),
        pl.BlockSpec((1, CONV_STATE, D_MODEL), lambda b, t: (b, 0, 0)),
    ] + [_slab_spec(r, c, B * nt, step_of, packed=True) for r, c in slabs]
    out_shape = [
        jax.ShapeDtypeStruct((B, L, D_MODEL), F32),
        jax.ShapeDtypeStruct((B, N_HEADS, DK, DV), F32),
        jax.ShapeDtypeStruct((B, CONV_STATE, D_MODEL), F32),
    ] + [jax.ShapeDtypeStruct((r // 2, c), jnp.uint32) for r, c in slabs]
    return pl.pallas_call(
        _mixer_kernel,
        grid=(B, L // T),
        in_specs=in_specs,
        out_specs=out_specs,
        out_shape=out_shape,
        scratch_shapes=[
            pltpu.VMEM((D_MODEL // LANES, HIST + T, LANES), F32),
            pltpu.VMEM((D_MODEL // LANES, T, LANES), F32),
            pltpu.VMEM((T, RET_V), BF16),
        ],
        compiler_params=pltpu.CompilerParams(
            dimension_semantics=("arbitrary", "arbitrary"),
            vmem_limit_bytes=VMEM_LIMIT),
        name="prompt_mixer",
    )(x, mod_p, g_pre1, g_post1, cos_p, sin_p, decay, rs_tab, w_in_b, conv_w, conv_b,
      ln_g, ln_b, w_conv_out_b, gn_g, w_ret_out_b, w_out_b, w_gate, w_up, w_down)


def _ffn_body(x, sh2, sc2, gt2, gpre, gpost, wg_ref, wu_ref, wd_ref):
    hb = (_rms(x, gpre) * (1.0 + sc2) + sh2).astype(BF16)
    act = (jax.nn.silu(_dot(hb, _w(wg_ref[...]))) * _dot(hb, _w(wu_ref[...]))).astype(BF16)
    return x + gt2 * _rms(_dot(act, _w(wd_ref[...])), gpost)


def _sample_ret_step(q, k, v, cos, sin, gam_ref, s0_ref, snew_ref, y_ref, i):
    q = _rotary(q, cos, sin) * (DK ** -0.5)
    k = _rotary(k, cos, sin)
    qk = jnp.sum(q * k, axis=-1, keepdims=True)
    cols = jnp.concatenate([k, q, jnp.zeros((LANES - 2 * N_HEADS, DK), F32)], axis=0).T
    for h in range(N_HEADS):
        s0 = s0_ref[i, h]
        v_h = v[h:h + 1, :]
        gam = gam_ref[h:h + 1, :]
        snew_ref[i, h] = gam * s0 + cols[:, h:h + 1] * v_h
        qs = jnp.sum(cols[:, N_HEADS + h:N_HEADS + h + 1] * s0, axis=0, keepdims=True)
        y_ref[i, h:h + 1, :] = qk[h:h + 1, :] * v_h + gam * qs


def _ffn_ret_kernel(x_ref, mod_ref, gpre_ref, gpost_ref, wg_ref, wu_ref, wd_ref,
                    q_ref, k_ref, v_ref, cos_ref, sin_ref, gam_ref, s0_ref,
                    u_ref, a_ref, cst_ref, convw_ref, convb_ref,
                    o_ref, snew_ref, y_ref, cnew_ref, conv_ref, *, conv_steps):
    step = pl.program_id(0) * pl.num_programs(1) + pl.program_id(1)

    @pl.when(step < conv_steps)
    def _():
        _sample_conv_step(u_ref, a_ref, cst_ref, convw_ref, convb_ref, cnew_ref, conv_ref)

    mod = mod_ref[0]
    o_ref[0] = _ffn_body(x_ref[0], mod[:, 3 * D_MODEL:4 * D_MODEL], mod[:, 4 * D_MODEL:5 * D_MODEL],
                         mod[:, 5 * D_MODEL:6 * D_MODEL], gpre_ref[...], gpost_ref[...],
                         wg_ref, wu_ref, wd_ref)
    for i in range(q_ref.shape[0]):
        _sample_ret_step(q_ref[i], k_ref[i], v_ref[i], cos_ref[...], sin_ref[...], gam_ref,
                         s0_ref, snew_ref, y_ref, i)


def _prompt_ffn_sample_ret(x1, mod_p, g_pre2, g_post2, wg_b, wu_b, wd_b,
                           q_s, k_s, v_s, cos_s, sin_s, gam_tab, state,
                           z_s, state_conv_t, conv_w, conv_b):
    B, L, _ = x1.shape
    nb = q_s.shape[0]
    d_ff = wg_b.shape[1]
    nt = L // T_FFN
    steps = B * nt
    sb = nb // steps
    assert sb * steps == nb
    conv_steps = nb // CONV_B
    assert conv_steps * CONV_B == nb and conv_steps <= steps
    row = lambda b, t: (0, 0)
    blk = lambda b, t: (b * nt + t, 0, 0)
    cblk = lambda b, t: jnp.minimum(b * nt + t, conv_steps - 1)
    return pl.pallas_call(
        functools.partial(_ffn_ret_kernel, conv_steps=conv_steps),
        grid=(B, nt),
        in_specs=[
            pl.BlockSpec((1, T_FFN, D_MODEL), lambda b, t: (b, t, 0)),
            pl.BlockSpec((1, 1, 6 * D_MODEL), lambda b, t: (b, 0, 0)),
            pl.BlockSpec((1, D_MODEL), row),
            pl.BlockSpec((1, D_MODEL), row),
            _const_spec((D_MODEL // 2, d_ff)),
            _const_spec((D_MODEL // 2, d_ff)),
            _const_spec((d_ff // 2, D_MODEL)),
            pl.BlockSpec((sb, N_HEADS, DK), blk),
            pl.BlockSpec((sb, N_HEADS, DK), blk),
            pl.BlockSpec((sb, N_HEADS, DV), blk),
            pl.BlockSpec((1, HALF), row),
            pl.BlockSpec((1, HALF), row),
            pl.BlockSpec((N_HEADS, DV), row),
            pl.BlockSpec((sb, N_HEADS, DK, DV), lambda b, t: (b * nt + t, 0, 0, 0)),
            pl.BlockSpec((CONV_B, D_MODEL), lambda b, t: (cblk(b, t), OFF_U // D_MODEL)),
            pl.BlockSpec((CONV_B, D_MODEL), lambda b, t: (cblk(b, t), OFF_A // D_MODEL)),
            pl.BlockSpec((1, CONV_STATE, CONV_B, D_MODEL), lambda b, t: (0, 0, cblk(b, t), 0)),
            pl.BlockSpec((CONV_WIDTH, D_MODEL), row),
            pl.BlockSpec((1, D_MODEL), row),
        ],
        out_specs=[
            pl.BlockSpec((1, T_FFN, D_MODEL), lambda b, t: (b, t, 0)),
            pl.BlockSpec((sb, N_HEADS, DK, DV), lambda b, t: (b * nt + t, 0, 0, 0)),
            pl.BlockSpec((sb, N_HEADS, DV), blk),
            pl.BlockSpec((1, CONV_STATE, CONV_B, D_MODEL), lambda b, t: (0, 0, cblk(b, t), 0)),
            pl.BlockSpec((CONV_B, D_MODEL), lambda b, t: (cblk(b, t), 0)),
        ],
        out_shape=[
            jax.ShapeDtypeStruct((B, L, D_MODEL), F32),
            jax.ShapeDtypeStruct((nb, N_HEADS, DK, DV), F32),
            jax.ShapeDtypeStruct((nb, N_HEADS, DV), F32),
            jax.ShapeDtypeStruct((1, CONV_STATE, nb, D_MODEL), F32),
            jax.ShapeDtypeStruct((nb, D_MODEL), F32),
        ],
        compiler_params=pltpu.CompilerParams(
            dimension_semantics=("arbitrary", "arbitrary"),
            vmem_limit_bytes=VMEM_LIMIT),
        name="prompt_ffn_sample_ret",
    )(x1, mod_p, g_pre2, g_post2, wg_b, wu_b, wd_b, q_s, k_s, v_s, cos_s, sin_s, gam_tab, state,
      z_s, z_s, state_conv_t, conv_w, conv_b)


def _sample_in_kernel(x_ref, sh_ref, sc_ref, gpre_ref, w_ref, wro_f_ref, wco_f_ref, wo_f_ref,
                      z_ref, wp_ref, wro_p_ref, wco_p_ref, wo_p_ref):
    hb = (_rms(x_ref[...], gpre_ref[...]) * (1.0 + sc_ref[...]) + sh_ref[...]).astype(BF16)
    wb = w_ref[...].astype(BF16)
    z_ref[...] = _dot(hb, wb)
    wp_ref[...] = pltpu.bitcast(wb, jnp.uint32)
    _pack_slabs((wro_f_ref, wco_f_ref, wo_f_ref), (wro_p_ref, wco_p_ref, wo_p_ref))


def _sample_in(xs, mod, g_pre1, w_in, w_ret_out, w_conv_out, w_out):
    rows = xs.shape[0]
    tn = D_MODEL
    steps = D_IN // tn
    slabs = [w_ret_out.shape, w_conv_out.shape, w_out.shape]
    step_of = lambda j: j
    return pl.pallas_call(
        _sample_in_kernel,
        grid=(steps,),
        in_specs=[
            pl.BlockSpec((rows, D_MODEL), lambda j: (0, 0)),
            pl.BlockSpec((rows, D_MODEL), lambda j: (0, 0)),
            pl.BlockSpec((rows, D_MODEL), lambda j: (0, 1)),
            pl.BlockSpec((1, D_MODEL), lambda j: (0, 0)),
            pl.BlockSpec((D_MODEL, tn), lambda j: (0, j)),
        ] + [_slab_spec(r, c, steps, step_of, packed=False) for r, c in slabs],
        out_specs=[
            pl.BlockSpec((rows, tn), lambda j: (0, j)),
            pl.BlockSpec((D_MODEL // 2, tn), lambda j: (0, j)),
        ] + [_slab_spec(r, c, steps, step_of, packed=True) for r, c in slabs],
        out_shape=[
            jax.ShapeDtypeStruct((rows, D_IN), F32),
            jax.ShapeDtypeStruct((D_MODEL // 2, D_IN), jnp.uint32),
        ] + [jax.ShapeDtypeStruct((r // 2, c), jnp.uint32) for r, c in slabs],
        compiler_params=pltpu.CompilerParams(dimension_semantics=("arbitrary",)),
        name="sample_in_proj",
    )(xs, mod, mod, g_pre1, w_in, w_ret_out, w_conv_out, w_out)


def _sample_conv_step(u_ref, a_ref, st_ref, w_ref, b_ref, new_ref, conv_ref):
    glu = u_ref[...] * jax.nn.sigmoid(a_ref[...])
    acc = glu * w_ref[CONV_STATE:CONV_WIDTH, :] + b_ref[...]
    for j in range(CONV_STATE):
        acc = acc + st_ref[0, j] * w_ref[j:j + 1, :]
    conv_ref[...] = acc
    for j in range(CONV_STATE - 1):
        new_ref[0, j] = st_ref[0, j + 1]
    new_ref[0, CONV_STATE - 1] = glu


def _sample_post_kernel(x_ref, mod_ref, z_ref, y_ref, conv_ref, gng_ref, wro_ref, lng_ref, lnb_ref,
                        wco_ref, wo_ref, gpost1_ref, gpre2_ref, gpost2_ref, wg_ref, wu_ref, wd_ref, o_ref):
    x = x_ref[...]
    rows = x.shape[0]

    def mod(i):
        return mod_ref[:, i * D_MODEL:(i + 1) * D_MODEL]

    ret_out = jnp.zeros((rows, D_MODEL), F32)
    for h in range(N_HEADS):
        sl = slice(h * DV, (h + 1) * DV)
        yn = _standardize(y_ref[:, sl]) * gng_ref[:, sl]
        gated = (jax.nn.silu(z_ref[:, OFF_G + h * DV:OFF_G + (h + 1) * DV]) * yn).astype(BF16)
        ret_out = ret_out + _dot(gated, _w(wro_ref[h * DV // 2:(h + 1) * DV // 2, :]))
    ln = _standardize(conv_ref[...]) * lng_ref[...] + lnb_ref[...]
    conv_out = _dot(jax.nn.silu(ln).astype(BF16), _w(wco_ref[...]))
    merged = (jax.nn.sigmoid(z_ref[:, OFF_GR:OFF_GR + D_MODEL]) * ret_out
              + jax.nn.sigmoid(z_ref[:, OFF_GC:OFF_GC + D_MODEL]) * conv_out)
    mix = _dot(merged.astype(BF16), _w(wo_ref[...]))
    x1 = x + mod(2) * _rms(mix, gpost1_ref[...])
    o_ref[...] = _ffn_body(x1, mod(3), mod(4), mod(5), gpre2_ref[...], gpost2_ref[...],
                           wg_ref, wu_ref, wd_ref)


def _sample_post(xs, mod_s, z_s, y_s, conv_s, gn_g, w_ret_out_b, ln_g, ln_b, w_conv_out_b, w_out_b,
                 g_post1, g_pre2, g_post2, wg_b, wu_b, wd_b):
    return pl.pallas_call(
        _sample_post_kernel,
        out_shape=jax.ShapeDtypeStruct(xs.shape, F32),
        compiler_params=pltpu.CompilerParams(vmem_limit_bytes=VMEM_LIMIT),
        name="sample_post",
    )(xs, mod_s, z_s, y_s, conv_s, gn_g, w_ret_out_b, ln_g, ln_b, w_conv_out_b, w_out_b,
      g_post1, g_pre2, g_post2, wg_b, wu_b, wd_b)


def _rope_tables(pos):
    inv_freq = ROPE_BASE ** (-np.arange(0, DK, 2, dtype=np.float64) / DK)
    ang = np.asarray(pos, np.float64)[:, None] * inv_freq[None, :]
    return jnp.asarray(np.cos(ang), F32), jnp.asarray(np.sin(ang), F32)


def _log_gamma():
    return np.log(1.0 - np.exp(np.linspace(np.log(1.0 / 32.0), np.log(1.0 / 512.0), N_HEADS)))


def _decay_tables(chunk):
    lg = _log_gamma()
    idx = np.arange(chunk, dtype=np.float64)
    diff = idx[:, None] - idx[None, :]
    decay = np.where(diff >= 0, np.exp(lg[:, None, None] * np.maximum(diff, 0.0)), 0.0)
    cross = np.exp(lg[None, :] * (idx[:, None] + 1.0))
    k_dec = np.exp(lg[None, :] * (chunk - 1.0 - idx[:, None]))
    full = np.broadcast_to(np.exp(lg * chunk)[None, :], (chunk, N_HEADS))
    rs = np.concatenate([cross, k_dec, full], axis=1)
    rs = np.pad(rs, ((0, 0), (0, LANES - rs.shape[1])))
    return jnp.asarray(decay, F32), jnp.asarray(rs, F32)


def kernel(x_prompt, x_sample, c_prompt, c_sample, state_ret, state_conv, w_in, w_ada, b_ada, g_pre1, g_post1, g_pre2, g_post2, conv_w, conv_b, conv_ln_g, conv_ln_b, w_conv_out, ret_gn_g, w_ret_out, w_out, w_ffn_gate, w_ffn_up, w_ffn_down):
    depth = w_in.shape[0]
    assert depth == 1, "single-layer step"
    B, L, _ = x_prompt.shape
    nb = x_sample.shape[0]

    cos_p, sin_p = _rope_tables(np.arange(L))
    cos_s, sin_s = _rope_tables(PAST_LEN + np.arange(x_sample.shape[1]))
    decay, rs_tab = _decay_tables(CHUNK)
    gam_tab = jnp.asarray(np.broadcast_to(np.exp(_log_gamma())[:, None], (N_HEADS, DV)), F32)

    mod, mod_p = _modulation(c_sample, c_prompt, w_ada[0], b_ada)
    mod_p = mod_p.reshape(B, 1, 6 * D_MODEL)

    xs = x_sample.reshape(nb, D_MODEL)
    z_s, w_in_b, w_ret_out_b, w_conv_out_b, w_out_b = _sample_in(
        xs, mod, g_pre1, w_in[0], w_ret_out[0], w_conv_out[0], w_out[0])
    q_s = z_s[:, OFF_Q:OFF_Q + RET_QK].reshape(nb, N_HEADS, DK)
    k_s = z_s[:, OFF_K:OFF_K + RET_QK].reshape(nb, N_HEADS, DK)
    v_s = z_s[:, OFF_V:OFF_V + RET_V].reshape(nb, N_HEADS, DV)

    x1_p, ret_p, conv_p, wg_b, wu_b, wd_b = _prompt_mixer(
        x_prompt, mod_p, g_pre1, g_post1, cos_p, sin_p, decay, rs_tab, w_in_b, conv_w[0], conv_b,
        conv_ln_g, conv_ln_b, w_conv_out_b, ret_gn_g, w_ret_out_b, w_out_b,
        w_ffn_gate[0], w_ffn_up[0], w_ffn_down[0])
    y_p, ret_s, yr_s, conv_new_t, conv_s = _prompt_ffn_sample_ret(
        x1_p, mod_p, g_pre2, g_post2, wg_b, wu_b, wd_b, q_s, k_s, v_s, cos_s, sin_s, gam_tab, state_ret[0],
        z_s, state_conv.transpose(0, 2, 1, 3), conv_w[0], conv_b)
    conv_new_s = conv_new_t.transpose(0, 2, 1, 3)
    y_s = _sample_post(xs, mod, z_s, yr_s.reshape(nb, RET_V), conv_s, ret_gn_g, w_ret_out_b,
                       conv_ln_g, conv_ln_b, w_conv_out_b, w_out_b, g_post1, g_pre2, g_post2,
                       wg_b, wu_b, wd_b)

    return (y_p, y_s.reshape(x_sample.shape), ret_p[None], ret_s[None], conv_p[None], conv_new_s)
```

```python
import functools

import numpy as np

import jax
import jax.numpy as jnp
from jax import lax
from jax.experimental import pallas as pl
from jax.experimental.pallas import tpu as pltpu

F32 = jnp.float32
BF16 = jnp.bfloat16

D_MODEL = 1024
N_HEADS = 4
DK = 256
DV = 512
HALF = DK // 2
RET_QK = N_HEADS * DK
RET_V = N_HEADS * DV
CONV_WIDTH = 31
CONV_STATE = CONV_WIDTH - 1
EPS = 1e-6
ROPE_BASE = 10000.0
PAST_LEN = 16384

OFF_Q = 0
OFF_K = RET_QK
OFF_V = 2 * RET_QK
OFF_G = OFF_V + RET_V
OFF_U = OFF_G + RET_V
OFF_A = OFF_U + D_MODEL
OFF_GR = OFF_A + D_MODEL
OFF_GC = OFF_GR + D_MODEL
D_IN = OFF_GC + D_MODEL

LANES = 128
BF16_TILE_ROWS = 16
T_MIX = 256
CHUNK = 256
HIST = 32
ROW_CHUNK = 64
T_FFN = 256
CONV_B = 8
MIB = 1024 * 1024
V7X_VMEM_BYTES = 64 * MIB
VMEM_LIMIT = V7X_VMEM_BYTES - 8 * MIB


def _rms(x, g):
    ms = jnp.mean(x * x, axis=-1, keepdims=True)
    return x * lax.rsqrt(ms + EPS) * g


def _rotary(t, cos, sin):
    t1, t2 = t[:, :HALF], t[:, HALF:]
    return jnp.concatenate([t1 * cos - t2 * sin, t2 * cos + t1 * sin], axis=-1)


def _standardize(y):
    mu = jnp.mean(y, axis=-1, keepdims=True)
    yc = y - mu
    var = jnp.mean(yc * yc, axis=-1, keepdims=True)
    return yc * lax.rsqrt(var + EPS)


def _dot(a, b):
    return jnp.dot(a, b, preferred_element_type=F32)


def _w(packed):
    return pltpu.bitcast(packed, BF16)


def _const_spec(shape):
    return pl.BlockSpec(shape, lambda *_: (0,) * len(shape), pipeline_mode=pl.Buffered(1))


def _mod_kernel(cs_ref, cp_ref, w_ref, b_ref, os_ref, op_ref):
    wb = w_ref[...].astype(BF16)
    for c_ref, o_ref in ((cs_ref, os_ref), (cp_ref, op_ref)):
        o_ref[...] = _dot(jax.nn.silu(c_ref[...]).astype(BF16), wb) + b_ref[...]


def _modulation(c_sample, c_prompt, w_ada, b_ada):
    n_out = w_ada.shape[1]
    tn = D_MODEL
    groups = (c_sample.shape[0], c_prompt.shape[0])
    return pl.pallas_call(
        _mod_kernel,
        grid=(n_out // tn,),
        in_specs=[pl.BlockSpec((rows, D_MODEL), lambda j: (0, 0)) for rows in groups] + [
            pl.BlockSpec((D_MODEL, tn), lambda j: (0, j)),
            pl.BlockSpec((1, tn), lambda j: (0, j)),
        ],
        out_specs=[pl.BlockSpec((rows, tn), lambda j: (0, j)) for rows in groups],
        out_shape=[jax.ShapeDtypeStruct((rows, n_out), F32) for rows in groups],
        compiler_params=pltpu.CompilerParams(dimension_semantics=("arbitrary",)),
        name="adaln_mod",
    )(c_sample, c_prompt, w_ada, b_ada)


def _mixer_kernel(x_ref, mod_ref, gpre_ref, gpost_ref, cos_ref, sin_ref, decay_ref, rs_ref,
                  w_in_ref, convw_ref, convb_ref, lng_ref, lnb_ref, wco_ref, gng_ref, wro_ref, wo_ref,
                  wg_f_ref, wu_f_ref, wd_f_ref,
                  x1_ref, rstate_ref, cstate_ref, wg_p_ref, wu_p_ref, wd_p_ref,
                  full_ref, conv_ref, gated_ref):
    t = pl.program_id(1)
    T = T_MIX
    n_col = D_MODEL // LANES

    @pl.when(t == 0)
    def _():
        rstate_ref[...] = jnp.zeros_like(rstate_ref)
        full_ref[:, 0:HIST, :] = jnp.zeros((n_col, HIST, LANES), F32)

    _pack_slabs((wg_f_ref, wu_f_ref, wd_f_ref), (wg_p_ref, wu_p_ref, wd_p_ref))

    mod = mod_ref[0]
    sh1 = mod[:, 0:D_MODEL]
    sc1 = mod[:, D_MODEL:2 * D_MODEL]
    gt1 = mod[:, 2 * D_MODEL:3 * D_MODEL]
    hb = (_rms(x_ref[0], gpre_ref[...]) * (1.0 + sc1) + sh1).astype(BF16)

    def proj(lo, width):
        return _dot(hb, _w(w_in_ref[:, lo:lo + width]))

    glu = proj(OFF_U, D_MODEL) * jax.nn.sigmoid(proj(OFF_A, D_MODEL))
    for c in range(n_col):
        full_ref[c, HIST:HIST + T, :] = glu[:, c * LANES:(c + 1) * LANES]

    @pl.when(t == pl.num_programs(1) - 1)
    def _():
        for c in range(n_col):
            cstate_ref[0, :, c * LANES:(c + 1) * LANES] = full_ref[c, HIST + T - CONV_STATE:HIST + T, :]

    def conv_block(c):
        lanes = slice(c * LANES, (c + 1) * LANES)
        for r0 in range(0, T, ROW_CHUNK):
            acc = jnp.broadcast_to(convb_ref[:, lanes], (ROW_CHUNK, LANES))
            for j in range(CONV_WIDTH):
                start = HIST - CONV_STATE + j + r0
                acc = acc + full_ref[c, start:start + ROW_CHUNK, :] * convw_ref[j:j + 1, lanes]
            conv_ref[c, r0:r0 + ROW_CHUNK, :] = acc

    assert T == CHUNK
    zq = proj(OFF_Q, RET_QK)
    zk = proj(OFF_K, RET_QK)
    zv = proj(OFF_V, RET_V).astype(BF16)
    hq = lambda h: slice(h * DK, (h + 1) * DK)
    hv = lambda h: slice(h * DV, (h + 1) * DV)
    qb, kb, kd = [], [], []
    for h in range(N_HEADS):
        q = _rotary(zq[:, hq(h)], cos_ref[...], sin_ref[...]) * (DK ** -0.5)
        k = _rotary(zk[:, hq(h)], cos_ref[...], sin_ref[...])
        qb.append(q.astype(BF16))
        kb.append(k.astype(BF16))
        kd.append((k * rs_ref[:, N_HEADS + h:N_HEADS + h + 1]).astype(BF16))
    sc = [(lax.dot_general(qb[h], kb[h], (((1,), (1,)), ((), ())), preferred_element_type=F32)
           * decay_ref[h]).astype(BF16) for h in range(N_HEADS)]
    for c in range(0, n_col // 2):
        conv_block(c)
    ys = []
    for h in range(N_HEADS):
        s_prev = rstate_ref[0, h]
        y = _dot(sc[h], zv[:, hv(h)])
        y = y + _dot(qb[h], s_prev.astype(BF16)) * rs_ref[:, h:h + 1]
        upd = lax.dot_general(kd[h], zv[:, hv(h)], (((0,), (0,)), ((), ())), preferred_element_type=F32)
        rstate_ref[0, h] = rs_ref[0:1, 2 * N_HEADS + h:2 * N_HEADS + h + 1] * s_prev + upd
        ys.append(y)
    zg = proj(OFF_G, RET_V)
    for c in range(n_col // 2, n_col):
        conv_block(c)
    for h in range(N_HEADS):
        yn = _standardize(ys[h]) * gng_ref[:, hv(h)]
        gated_ref[:, hv(h)] = (jax.nn.silu(zg[:, hv(h)]) * yn).astype(BF16)

    full_ref[:, 0:HIST, :] = full_ref[:, T:T + HIST, :]
    conv = jnp.concatenate([conv_ref[c] for c in range(n_col)], axis=-1)
    ln = _standardize(conv) * lng_ref[...] + lnb_ref[...]
    conv_out = _dot(jax.nn.silu(ln).astype(BF16), _w(wco_ref[...]))
    ret_out = _dot(gated_ref[...], _w(wro_ref[...]))

    merged = (jax.nn.sigmoid(proj(OFF_GR, D_MODEL)) * ret_out
              + jax.nn.sigmoid(proj(OFF_GC, D_MODEL)) * conv_out)
    mix = _dot(merged.astype(BF16), _w(wo_ref[...]))
    x1_ref[0] = x_ref[0] + gt1 * _rms(mix, gpost_ref[...])


def _slab_spec(total_rows, cols, steps, step_of, packed):
    rows = next(r for r in range(BF16_TILE_ROWS, total_rows + 1, BF16_TILE_ROWS)
                if total_rows % r == 0 and total_rows // r <= steps)
    last = total_rows // rows - 1
    return pl.BlockSpec((rows // 2 if packed else rows, cols),
                        lambda *idx: (jnp.minimum(step_of(*idx), last), 0))


def _pack_slabs(srcs, dsts):
    for src, dst in zip(srcs, dsts):
        dst[...] = pltpu.bitcast(src[...].astype(BF16), jnp.uint32)


def _prompt_mixer(x, mod_p, g_pre1, g_post1, cos_p, sin_p, decay, rs_tab, w_in_b, conv_w, conv_b,
                  ln_g, ln_b, w_conv_out_b, gn_g, w_ret_out_b, w_out_b, w_gate, w_up, w_down):
    B, L, _ = x.shape
    T = T_MIX
    nt = L // T
    d_ff = w_gate.shape[1]
    row = lambda b, t: (0, 0)
    step_of = lambda b, t: b * nt + t
    slabs = [(D_MODEL, d_ff), (D_MODEL, d_ff), (d_ff, D_MODEL)]
    in_specs = [
        pl.BlockSpec((1, T, D_MODEL), lambda b, t: (b, t, 0)),
        pl.BlockSpec((1, 1, 6 * D_MODEL), lambda b, t: (b, 0, 0)),
        pl.BlockSpec((1, D_MODEL), row),
        pl.BlockSpec((1, D_MODEL), row),
        pl.BlockSpec((T, HALF), lambda b, t: (t, 0)),
        pl.BlockSpec((T, HALF), lambda b, t: (t, 0)),
        _const_spec((N_HEADS, CHUNK, CHUNK)),
        _const_spec((CHUNK, LANES)),
        _const_spec((D_MODEL // 2, D_IN)),
        pl.BlockSpec((CONV_WIDTH, D_MODEL), row),
        pl.BlockSpec((1, D_MODEL), row),
        pl.BlockSpec((1, D_MODEL), row),
        pl.BlockSpec((1, D_MODEL), row),
        _const_spec((D_MODEL // 2, D_MODEL)),
        pl.BlockSpec((1, RET_V), row),
        _const_spec((RET_V // 2, D_MODEL)),
        _const_spec((D_MODEL // 2, D_MODEL)),
    ] + [_slab_spec(r, c, B * nt, step_of, packed=False) for r, c in slabs]
    out_specs = [
        pl.BlockSpec((1, T, D_MODEL), lambda b, t: (b, t, 0)),
        pl.BlockSpec((1, N_HEADS, DK, DV), lambda b, t: (b, 0, 0, 0)),
        pl.BlockSpec((1, CONV_STATE, D_MODEL), lambda b, t: (b, 0, 0)),
    ] + [_slab_spec(r, c, B * nt, step_of, packed=True) for r, c in slabs]
    out_shape = [
        jax.ShapeDtypeStruct((B, L, D_MODEL), F32),
        jax.ShapeDtypeStruct((B, N_HEADS, DK, DV), F32),
        jax.ShapeDtypeStruct((B, CONV_STATE, D_MODEL), F32),
    ] + [jax.ShapeDtypeStruct((r // 2, c), jnp.uint32) for r, c in slabs]
    return pl.pallas_call(
        _mixer_kernel,
        grid=(B, L // T),
        in_specs=in_specs,
        out_specs=out_specs,
        out_shape=out_shape,
        scratch_shapes=[
            pltpu.VMEM((D_MODEL // LANES, HIST + T, LANES), F32),
            pltpu.VMEM((D_MODEL // LANES, T, LANES), F32),
            pltpu.VMEM((T, RET_V), BF16),
        ],
        compiler_params=pltpu.CompilerParams(
            dimension_semantics=("arbitrary", "arbitrary"),
            vmem_limit_bytes=VMEM_LIMIT),
        name="prompt_mixer",
    )(x, mod_p, g_pre1, g_post1, cos_p, sin_p, decay, rs_tab, w_in_b, conv_w, conv_b,
      ln_g, ln_b, w_conv_out_b, gn_g, w_ret_out_b, w_out_b, w_gate, w_up, w_down)


def _ffn_body(x, sh2, sc2, gt2, gpre, gpost, wg_ref, wu_ref, wd_ref):
    hb = (_rms(x, gpre) * (1.0 + sc2) + sh2).astype(BF16)
    act = (jax.nn.silu(_dot(hb, _w(wg_ref[...]))) * _dot(hb, _w(wu_ref[...]))).astype(BF16)
    return x + gt2 * _rms(_dot(act, _w(wd_ref[...])), gpost)


def _sample_ret_step(q, k, v, cos, sin, gam_ref, s0_ref, snew_ref, y_ref, i):
    q = _rotary(q, cos, sin) * (DK ** -0.5)
    k = _rotary(k, cos, sin)
    qk = jnp.sum(q * k, axis=-1, keepdims=True)
    cols = jnp.concatenate([k, q, jnp.zeros((LANES - 2 * N_HEADS, DK), F32)], axis=0).T
    for h in range(N_HEADS):
        s0 = s0_ref[i, h]
        v_h = v[h:h + 1, :]
        gam = gam_ref[h:h + 1, :]
        snew_ref[i, h] = gam * s0 + cols[:, h:h + 1] * v_h
        qs = jnp.sum(cols[:, N_HEADS + h:N_HEADS + h + 1] * s0, axis=0, keepdims=True)
        y_ref[i, h:h + 1, :] = qk[h:h + 1, :] * v_h + gam * qs


def _ffn_ret_kernel(x_ref, mod_ref, gpre_ref, gpost_ref, wg_ref, wu_ref, wd_ref,
                    q_ref, k_ref, v_ref, cos_ref, sin_ref, gam_ref, s0_ref,
                    u_ref, a_ref, cst_ref, convw_ref, convb_ref,
                    o_ref, snew_ref, y_ref, cnew_ref, conv_ref, *, conv_steps):
    step = pl.program_id(0) * pl.num_programs(1) + pl.program_id(1)

    @pl.when(step < conv_steps)
    def _():
        _sample_conv_step(u_ref, a_ref, cst_ref, convw_ref, convb_ref, cnew_ref, conv_ref)

    mod = mod_ref[0]
    o_ref[0] = _ffn_body(x_ref[0], mod[:, 3 * D_MODEL:4 * D_MODEL], mod[:, 4 * D_MODEL:5 * D_MODEL],
                         mod[:, 5 * D_MODEL:6 * D_MODEL], gpre_ref[...], gpost_ref[...],
                         wg_ref, wu_ref, wd_ref)
    for i in range(q_ref.shape[0]):
        _sample_ret_step(q_ref[i], k_ref[i], v_ref[i], cos_ref[...], sin_ref[...], gam_ref,
                         s0_ref, snew_ref, y_ref, i)


def _prompt_ffn_sample_ret(x1, mod_p, g_pre2, g_post2, wg_b, wu_b, wd_b,
                           q_s, k_s, v_s, cos_s, sin_s, gam_tab, state,
                           z_s, state_conv_t, conv_w, conv_b):
    B, L, _ = x1.shape
    nb = q_s.shape[0]
    d_ff = wg_b.shape[1]
    nt = L // T_FFN
    steps = B * nt
    sb = nb // steps
    assert sb * steps == nb
    conv_steps = nb // CONV_B
    assert conv_steps * CONV_B == nb and conv_steps <= steps
    row = lambda b, t: (0, 0)
    blk = lambda b, t: (b * nt + t, 0, 0)
    cblk = lambda b, t: jnp.minimum(b * nt + t, conv_steps - 1)
    return pl.pallas_call(
        functools.partial(_ffn_ret_kernel, conv_steps=conv_steps),
        grid=(B, nt),
        in_specs=[
            pl.BlockSpec((1, T_FFN, D_MODEL), lambda b, t: (b, t, 0)),
            pl.BlockSpec((1, 1, 6 * D_MODEL), lambda b, t: (b, 0, 0)),
            pl.BlockSpec((1, D_MODEL), row),
            pl.BlockSpec((1, D_MODEL), row),
            _const_spec((D_MODEL // 2, d_ff)),
            _const_spec((D_MODEL // 2, d_ff)),
            _const_spec((d_ff // 2, D_MODEL)),
            pl.BlockSpec((sb, N_HEADS, DK), blk),
            pl.BlockSpec((sb, N_HEADS, DK), blk),
            pl.BlockSpec((sb, N_HEADS, DV), blk),
            pl.BlockSpec((1, HALF), row),
            pl.BlockSpec((1, HALF), row),
            pl.BlockSpec((N_HEADS, DV), row),
            pl.BlockSpec((sb, N_HEADS, DK, DV), lambda b, t: (b * nt + t, 0, 0, 0)),
            pl.BlockSpec((CONV_B, D_MODEL), lambda b, t: (cblk(b, t), OFF_U // D_MODEL)),
            pl.BlockSpec((CONV_B, D_MODEL), lambda b, t: (cblk(b, t), OFF_A // D_MODEL)),
            pl.BlockSpec((1, CONV_STATE, CONV_B, D_MODEL), lambda b, t: (0, 0, cblk(b, t), 0)),
            pl.BlockSpec((CONV_WIDTH, D_MODEL), row),
            pl.BlockSpec((1, D_MODEL), row),
        ],
        out_specs=[
            pl.BlockSpec((1, T_FFN, D_MODEL), lambda b, t: (b, t, 0)),
            pl.BlockSpec((sb, N_HEADS, DK, DV), lambda b, t: (b * nt + t, 0, 0, 0)),
            pl.BlockSpec((sb, N_HEADS, DV), blk),
            pl.BlockSpec((1, CONV_STATE, CONV_B, D_MODEL), lambda b, t: (0, 0, cblk(b, t), 0)),
            pl.BlockSpec((CONV_B, D_MODEL), lambda b, t: (cblk(b, t), 0)),
        ],
        out_shape=[
            jax.ShapeDtypeStruct((B, L, D_MODEL), F32),
            jax.ShapeDtypeStruct((nb, N_HEADS, DK, DV), F32),
            jax.ShapeDtypeStruct((nb, N_HEADS, DV), F32),
            jax.ShapeDtypeStruct((1, CONV_STATE, nb, D_MODEL), F32),
            jax.ShapeDtypeStruct((nb, D_MODEL), F32),
        ],
        compiler_params=pltpu.CompilerParams(
            dimension_semantics=("arbitrary", "arbitrary"),
            vmem_limit_bytes=VMEM_LIMIT),
        name="prompt_ffn_sample_ret",
    )(x1, mod_p, g_pre2, g_post2, wg_b, wu_b, wd_b, q_s, k_s, v_s, cos_s, sin_s, gam_tab, state,
      z_s, z_s, state_conv_t, conv_w, conv_b)


def _sample_in_kernel(x_ref, sh_ref, sc_ref, gpre_ref, w_ref, wro_f_ref, wco_f_ref, wo_f_ref,
                      z_ref, wp_ref, wro_p_ref, wco_p_ref, wo_p_ref):
    hb = (_rms(x_ref[...], gpre_ref[...]) * (1.0 + sc_ref[...]) + sh_ref[...]).astype(BF16)
    wb = w_ref[...].astype(BF16)
    z_ref[...] = _dot(hb, wb)
    wp_ref[...] = pltpu.bitcast(wb, jnp.uint32)
    _pack_slabs((wro_f_ref, wco_f_ref, wo_f_ref), (wro_p_ref, wco_p_ref, wo_p_ref))


def _sample_in(xs, mod, g_pre1, w_in, w_ret_out, w_conv_out, w_out):
    rows = xs.shape[0]
    tn = D_MODEL
    steps = D_IN // tn
    slabs = [w_ret_out.shape, w_conv_out.shape, w_out.shape]
    step_of = lambda j: j
    return pl.pallas_call(
        _sample_in_kernel,
        grid=(steps,),
        in_specs=[
            pl.BlockSpec((rows, D_MODEL), lambda j: (0, 0)),
            pl.BlockSpec((rows, D_MODEL), lambda j: (0, 0)),
            pl.BlockSpec((rows, D_MODEL), lambda j: (0, 1)),
            pl.BlockSpec((1, D_MODEL), lambda j: (0, 0)),
            pl.BlockSpec((D_MODEL, tn), lambda j: (0, j)),
        ] + [_slab_spec(r, c, steps, step_of, packed=False) for r, c in slabs],
        out_specs=[
            pl.BlockSpec((rows, tn), lambda j: (0, j)),
            pl.BlockSpec((D_MODEL // 2, tn), lambda j: (0, j)),
        ] + [_slab_spec(r, c, steps, step_of, packed=True) for r, c in slabs],
        out_shape=[
            jax.ShapeDtypeStruct((rows, D_IN), F32),
            jax.ShapeDtypeStruct((D_MODEL // 2, D_IN), jnp.uint32),
        ] + [jax.ShapeDtypeStruct((r // 2, c), jnp.uint32) for r, c in slabs],
        compiler_params=pltpu.CompilerParams(dimension_semantics=("arbitrary",)),
        name="sample_in_proj",
    )(xs, mod, mod, g_pre1, w_in, w_ret_out, w_conv_out, w_out)


def _sample_conv_step(u_ref, a_ref, st_ref, w_ref, b_ref, new_ref, conv_ref):
    glu = u_ref[...] * jax.nn.sigmoid(a_ref[...])
    acc = glu * w_ref[CONV_STATE:CONV_WIDTH, :] + b_ref[...]
    for j in range(CONV_STATE):
        acc = acc + st_ref[0, j] * w_ref[j:j + 1, :]
    conv_ref[...] = acc
    for j in range(CONV_STATE - 1):
        new_ref[0, j] = st_ref[0, j + 1]
    new_ref[0, CONV_STATE - 1] = glu


def _sample_post_kernel(x_ref, mod_ref, z_ref, y_ref, conv_ref, gng_ref, wro_ref, lng_ref, lnb_ref,
                        wco_ref, wo_ref, gpost1_ref, gpre2_ref, gpost2_ref, wg_ref, wu_ref, wd_ref, o_ref):
    x = x_ref[...]
    rows = x.shape[0]

    def mod(i):
        return mod_ref[:, i * D_MODEL:(i + 1) * D_MODEL]

    ret_out = jnp.zeros((rows, D_MODEL), F32)
    for h in range(N_HEADS):
        sl = slice(h * DV, (h + 1) * DV)
        yn = _standardize(y_ref[:, sl]) * gng_ref[:, sl]
        gated = (jax.nn.silu(z_ref[:, OFF_G + h * DV:OFF_G + (h + 1) * DV]) * yn).astype(BF16)
        ret_out = ret_out + _dot(gated, _w(wro_ref[h * DV // 2:(h + 1) * DV // 2, :]))
    ln = _standardize(conv_ref[...]) * lng_ref[...] + lnb_ref[...]
    conv_out = _dot(jax.nn.silu(ln).astype(BF16), _w(wco_ref[...]))
    merged = (jax.nn.sigmoid(z_ref[:, OFF_GR:OFF_GR + D_MODEL]) * ret_out
              + jax.nn.sigmoid(z_ref[:, OFF_GC:OFF_GC + D_MODEL]) * conv_out)
    mix = _dot(merged.astype(BF16), _w(wo_ref[...]))
    x1 = x + mod(2) * _rms(mix, gpost1_ref[...])
    o_ref[...] = _ffn_body(x1, mod(3), mod(4), mod(5), gpre2_ref[...], gpost2_ref[...],
                           wg_ref, wu_ref, wd_ref)


def _sample_post(xs, mod_s, z_s, y_s, conv_s, gn_g, w_ret_out_b, ln_g, ln_b, w_conv_out_b, w_out_b,
                 g_post1, g_pre2, g_post2, wg_b, wu_b, wd_b):
    return pl.pallas_call(
        _sample_post_kernel,
        out_shape=jax.ShapeDtypeStruct(xs.shape, F32),
        compiler_params=pltpu.CompilerParams(vmem_limit_bytes=VMEM_LIMIT),
        name="sample_post",
    )(xs, mod_s, z_s, y_s, conv_s, gn_g, w_ret_out_b, ln_g, ln_b, w_conv_out_b, w_out_b,
      g_post1, g_pre2, g_post2, wg_b, wu_b, wd_b)


def _rope_tables(pos):
    inv_freq = ROPE_BASE ** (-np.arange(0, DK, 2, dtype=np.float64) / DK)
    ang = np.asarray(pos, np.float64)[:, None] * inv_freq[None, :]
    return jnp.asarray(np.cos(ang), F32), jnp.asarray(np.sin(ang), F32)


def _log_gamma():
    return np.log(1.0 - np.exp(np.linspace(np.log(1.0 / 32.0), np.log(1.0 / 512.0), N_HEADS)))


def _decay_tables(chunk):
    lg = _log_gamma()
    idx = np.arange(chunk, dtype=np.float64)
    diff = idx[:, None] - idx[None, :]
    decay = np.where(diff >= 0, np.exp(lg[:, None, None] * np.maximum(diff, 0.0)), 0.0)
    cross = np.exp(lg[None, :] * (idx[:, None] + 1.0))
    k_dec = np.exp(lg[None, :] * (chunk - 1.0 - idx[:, None]))
    full = np.broadcast_to(np.exp(lg * chunk)[None, :], (chunk, N_HEADS))
    rs = np.concatenate([cross, k_dec, full], axis=1)
    rs = np.pad(rs, ((0, 0), (0, LANES - rs.shape[1])))
    return jnp.asarray(decay, F32), jnp.asarray(rs, F32)


def kernel(x_prompt, x_sample, c_prompt, c_sample, state_ret, state_conv, w_in, w_ada, b_ada, g_pre1, g_post1, g_pre2, g_post2, conv_w, conv_b, conv_ln_g, conv_ln_b, w_conv_out, ret_gn_g, w_ret_out, w_out, w_ffn_gate, w_ffn_up, w_ffn_down):
    depth = w_in.shape[0]
    assert depth == 1, "single-layer step"
    B, L, _ = x_prompt.shape
    nb = x_sample.shape[0]

    cos_p, sin_p = _rope_tables(np.arange(L))
    cos_s, sin_s = _rope_tables(PAST_LEN + np.arange(x_sample.shape[1]))
    decay, rs_tab = _decay_tables(CHUNK)
    gam_tab = jnp.asarray(np.broadcast_to(np.exp(_log_gamma())[:, None], (N_HEADS, DV)), F32)

    mod, mod_p = _modulation(c_sample, c_prompt, w_ada[0], b_ada)
    mod_p = mod_p.reshape(B, 1, 6 * D_MODEL)

    xs = x_sample.reshape(nb, D_MODEL)
    z_s, w_in_b, w_ret_out_b, w_conv_out_b, w_out_b = _sample_in(
        xs, mod, g_pre1, w_in[0], w_ret_out[0], w_conv_out[0], w_out[0])
    q_s = z_s[:, OFF_Q:OFF_Q + RET_QK].reshape(nb, N_HEADS, DK)
    k_s = z_s[:, OFF_K:OFF_K + RET_QK].reshape(nb, N_HEADS, DK)
    v_s = z_s[:, OFF_V:OFF_V + RET_V].reshape(nb, N_HEADS, DV)

    x1_p, ret_p, conv_p, wg_b, wu_b, wd_b = _prompt_mixer(
        x_prompt, mod_p, g_pre1, g_post1, cos_p, sin_p, decay, rs_tab, w_in_b, conv_w[0], conv_b,
        conv_ln_g, conv_ln_b, w_conv_out_b, ret_gn_g, w_ret_out_b, w_out_b,
        w_ffn_gate[0], w_ffn_up[0], w_ffn_down[0])
    y_p, ret_s, yr_s, conv_new_t, conv_s = _prompt_ffn_sample_ret(
        x1_p, mod_p, g_pre2, g_post2, wg_b, wu_b, wd_b, q_s, k_s, v_s, cos_s, sin_s, gam_tab, state_ret[0],
        z_s, state_conv.transpose(0, 2, 1, 3), conv_w[0], conv_b)
    conv_new_s = conv_new_t.transpose(0, 2, 1, 3)
    y_s = _sample_post(xs, mod, z_s, yr_s.reshape(nb, RET_V), conv_s, ret_gn_g, w_ret_out_b,
                       conv_ln_g, conv_ln_b, w_conv_out_b, w_out_b, g_post1, g_pre2, g_post2,
                       wg_b, wu_b, wd_b)

    return (y_p, y_s.reshape(x_sample.shape), ret_p[None], ret_s[None], conv_p[None], conv_new_s)
```
